```python
import math
import jax, jax.numpy as jnp
from jax import lax
import numpy as np

D_MODEL = 4096
BATCH = 2
SEQ = 8192
DEPTH = 1
DEC_BATCH = 8
DEC_SEQ = 32
PAST_LEN = 4096

CHUNK = 64
EPS = 1e-6
MLA_HEADS = 32
Q_LORA = 1024
KV_LORA = 512
QK_NOPE = 128
QK_ROPE = 64
V_HEAD = 128
ROPE_THETA = 10000.0
Q_BLOCK = 128
SCALE = (QK_NOPE + QK_ROPE) ** -0.5
SSD_EXPAND = 2
SSD_INNER = SSD_EXPAND * D_MODEL
SSD_HEADDIM = 64
SSD_HEADS = SSD_INNER // SSD_HEADDIM
SSD_STATE = 128
SSD_GROUPS = 8
SSD_CONV = 4
SSD_CONV_DIM = SSD_INNER + 2 * SSD_GROUPS * SSD_STATE
N_GROUPS = 8
EXPERTS_PER_GROUP = 8
N_EXPERTS = N_GROUPS * EXPERTS_PER_GROUP
TOP_K = 2
D_EXPERT = 1024
MOE_BLOCK = 128
IN_SIZES = (Q_LORA, KV_LORA, QK_ROPE, SSD_INNER, SSD_CONV_DIM, SSD_HEADS, D_MODEL, D_MODEL)
N_IN = Q_LORA + KV_LORA + QK_ROPE + SSD_INNER + SSD_CONV_DIM + SSD_HEADS + 2 * D_MODEL

kernel_name = 'hybrid_mla_ssd_hmoe_stream_step'


def rmsnorm(x, w):
    xf = x.astype(jnp.float32)
    y = xf * lax.rsqrt(jnp.mean(xf * xf, axis=-1, keepdims=True) + EPS)
    return (y * w.astype(jnp.float32)).astype(x.dtype)


def rope(x, pos):
    half = QK_ROPE // 2
    inv_freq = ROPE_THETA ** (-jnp.arange(half, dtype=jnp.float32) / half)
    ang = pos.astype(jnp.float32)[:, None] * inv_freq[None, :]
    ang = ang.reshape((ang.shape[0],) + (1,) * (x.ndim - 3) + (half,))
    cos, sin = jnp.cos(ang), jnp.sin(ang)
    x1 = x[..., :half].astype(jnp.float32)
    x2 = x[..., half:].astype(jnp.float32)
    return jnp.concatenate([x1 * cos - x2 * sin, x2 * cos + x1 * sin], axis=-1).astype(x.dtype)


def split_in(proj):
    idx = [int(v) for v in np.cumsum(IN_SIZES)[:-1]]
    return jnp.split(proj, idx, axis=-1)


def mla_project(c_q, c_kv, k_r, pos, q_norm_w, kv_norm_w, w_uq):
    b, l = c_q.shape[:2]
    q = (rmsnorm(c_q, q_norm_w) @ w_uq).reshape(b, l, MLA_HEADS, QK_NOPE + QK_ROPE)
    q_nope = q[..., :QK_NOPE]
    q_rope = rope(q[..., QK_NOPE:], pos)
    ckv = rmsnorm(c_kv, kv_norm_w)
    krope = rope(k_r, pos)
    return q_nope, q_rope, ckv, krope


def mla_prompt(q_nope, q_rope, ckv, krope, w_ukv):
    b, l = ckv.shape[:2]
    kv = (ckv @ w_ukv).reshape(b, l, MLA_HEADS, QK_NOPE + V_HEAD)
    k_nope, v = kv[..., :QK_NOPE], kv[..., QK_NOPE:]
    key_chunk = jnp.arange(l) // CHUNK

    def block(i):
        start = i * Q_BLOCK
        qn = lax.dynamic_slice_in_dim(q_nope, start, Q_BLOCK, axis=1)
        qr = lax.dynamic_slice_in_dim(q_rope, start, Q_BLOCK, axis=1)
        s = (jnp.einsum('bqhd,bkhd->bhqk', qn, k_nope, preferred_element_type=jnp.float32)
             + jnp.einsum('bqhd,bkd->bhqk', qr, krope, preferred_element_type=jnp.float32)) * SCALE
        q_chunk = (start + jnp.arange(Q_BLOCK)) // CHUNK
        s = jnp.where(key_chunk[None, :] <= q_chunk[:, None], s, -jnp.inf)
        p = jax.nn.softmax(s, axis=-1).astype(v.dtype)
        return jnp.einsum('bhqk,bkhd->bqhd', p, v)

    o = lax.map(block, jnp.arange(l // Q_BLOCK))
    return jnp.moveaxis(o, 0, 1).reshape(b, l, MLA_HEADS * V_HEAD)


def mla_sample(q_nope, q_rope, ckv_all, krope_all, w_ukv):
    b, l = q_nope.shape[:2]
    w = w_ukv.reshape(KV_LORA, MLA_HEADS, QK_NOPE + V_HEAD)
    w_uk, w_uv = w[..., :QK_NOPE], w[..., QK_NOPE:]
    q_lat = jnp.einsum('bqhd,chd->bqhc', q_nope, w_uk)
    s = (jnp.einsum('bqhc,bkc->bhqk', q_lat, ckv_all, preferred_element_type=jnp.float32)
         + jnp.einsum('bqhd,bkd->bhqk', q_rope, krope_all, preferred_element_type=jnp.float32)) * SCALE
    p = jax.nn.softmax(s, axis=-1).astype(ckv_all.dtype)
    o_lat = jnp.einsum('bhqk,bkc->bqhc', p, ckv_all)
    return jnp.einsum('bqhc,chd->bqhd', o_lat, w_uv).reshape(b, l, MLA_HEADS * V_HEAD)


def causal_conv(u, prev, w, bias):
    L = u.shape[1]
    up = jnp.concatenate([prev.astype(u.dtype), u], axis=1)
    acc = bias
    for k in range(SSD_CONV):
        acc = acc + up[:, k:k + L] * w[k]
    return jax.nn.silu(acc), up[:, L:]


def ssd_scan(x, dt, A, Bm, Cm, h0):
    b, l = x.shape[:2]
    G, R, P, N = SSD_GROUPS, SSD_HEADS // SSD_GROUPS, SSD_HEADDIM, SSD_STATE
    Lc = min(CHUNK, l)
    nc = l // Lc
    xf = x.astype(jnp.float32).reshape(b, nc, Lc, G, R, P)
    dtc = dt.reshape(b, nc, Lc, G, R)
    Bc = Bm.astype(jnp.float32).reshape(b, nc, Lc, G, N)
    Cc = Cm.astype(jnp.float32).reshape(b, nc, Lc, G, N)
    a_cs = jnp.cumsum(dtc * A.reshape(G, R), axis=2)
    xdt = xf * dtc[..., None]
    mask = jnp.tril(jnp.ones((Lc, Lc), dtype=bool))[:, :, None, None]
    seg = a_cs[:, :, :, None] - a_cs[:, :, None, :]
    decay = jnp.exp(jnp.where(mask, seg, -jnp.inf))
    cb = jnp.einsum('bclgn,bcsgn->bclsg', Cc, Bc)
    y_diag = jnp.einsum('bclsgr,bcsgrp->bclgrp', cb[..., None] * decay, xdt)
    to_end = jnp.exp(a_cs[:, :, -1:] - a_cs)
    states = jnp.einsum('bclgn,bclgrp->bcgrpn', Bc, xdt * to_end[..., None])
    chunk_decay = jnp.exp(a_cs[:, :, -1])

    def step(h, inp):
        s_c, d_c = inp
        return h * d_c[..., None, None] + s_c, h

    h_init = h0.astype(jnp.float32).reshape(b, G, R, P, N)
    h_last, h_prev = lax.scan(step, h_init, (jnp.swapaxes(states, 0, 1), jnp.swapaxes(chunk_decay, 0, 1)))
    h_prev = jnp.swapaxes(h_prev, 0, 1)
    y_off = jnp.einsum('bclgn,bcgrpn->bclgrp', Cc, h_prev) * jnp.exp(a_cs)[..., None]
    y = (y_diag + y_off).reshape(b, l, SSD_HEADS, P)
    return y, h_last.reshape(b, SSD_HEADS, P, N)


def ssd_branch(xbc_raw, z, dt_raw, conv_prev, h_prev, conv_w, conv_b, dt_bias, a_log, d_skip, ssd_norm_w):
    b, l = xbc_raw.shape[:2]
    GN = SSD_GROUPS * SSD_STATE
    xbc, conv_new = causal_conv(xbc_raw, conv_prev, conv_w, conv_b)
    xs = xbc[..., :SSD_INNER].reshape(b, l, SSD_HEADS, SSD_HEADDIM)
    Bm = xbc[..., SSD_INNER:SSD_INNER + GN].reshape(b, l, SSD_GROUPS, SSD_STATE)
    Cm = xbc[..., SSD_INNER + GN:].reshape(b, l, SSD_GROUPS, SSD_STATE)
    dt = jax.nn.softplus(dt_raw.astype(jnp.float32) + dt_bias.astype(jnp.float32))
    A = -jnp.exp(a_log.astype(jnp.float32))
    y, h_last = ssd_scan(xs, dt, A, Bm, Cm, h_prev)
    y = y + xs.astype(jnp.float32) * d_skip.astype(jnp.float32)[:, None]
    y = y.reshape(b, l, SSD_INNER) * jax.nn.silu(z.astype(jnp.float32))
    yg = y.reshape(b, l, SSD_GROUPS, SSD_INNER // SSD_GROUPS)
    yg = yg * lax.rsqrt(jnp.mean(yg * yg, axis=-1, keepdims=True) + EPS)
    y = yg.reshape(b, l, SSD_INNER) * ssd_norm_w.astype(jnp.float32)
    return y.astype(z.dtype), conv_new, h_last.astype(h_prev.dtype)


def hier_moe(x, w_group, b_group, w_erouter, b_erouter, w_gate, w_up, w_down):
    b, l, d = x.shape
    T = b * l
    xf = x.reshape(T, d)
    g_logits = (xf @ w_group).astype(jnp.float32) + b_group.astype(jnp.float32)
    g_sel = jnp.argmax(g_logits, axis=-1)
    g_w = jnp.max(jax.nn.softmax(g_logits, axis=-1), axis=-1)
    e_logits = ((xf @ w_erouter).astype(jnp.float32) + b_erouter.astype(jnp.float32)).reshape(T, N_GROUPS, EXPERTS_PER_GROUP)
    e_in = e_logits[jnp.arange(T), g_sel]
    e_val, e_idx = lax.top_k(e_in, TOP_K)
    e_w = jax.nn.softmax(e_val, axis=-1) * g_w[:, None]
    expert_id = (g_sel[:, None] * EXPERTS_PER_GROUP + e_idx).reshape(-1).astype(jnp.int32)
    token_id = jnp.repeat(jnp.arange(T, dtype=jnp.int32), TOP_K)
    w_flat = e_w.reshape(-1)
    A = T * TOP_K
    order = jnp.argsort(expert_id)
    e_sorted = expert_id[order]
    counts = jnp.zeros((N_EXPERTS,), jnp.int32).at[expert_id].add(1)
    padded = (counts + MOE_BLOCK - 1) // MOE_BLOCK * MOE_BLOCK
    pad_end = jnp.cumsum(padded)
    pad_start = pad_end - padded
    start = jnp.cumsum(counts) - counts
    dest = pad_start[e_sorted] + jnp.arange(A, dtype=jnp.int32) - start[e_sorted]
    n_blocks = -(-A // MOE_BLOCK) + N_EXPERTS
    P = n_blocks * MOE_BLOCK
    row_token = jnp.full((P,), T, jnp.int32).at[dest].set(token_id[order])
    row_w = jnp.zeros((P,), x.dtype).at[dest].set(w_flat[order].astype(x.dtype))
    block_expert = jnp.minimum(jnp.searchsorted(pad_end, jnp.arange(n_blocks) * MOE_BLOCK, side='right'), N_EXPERTS - 1)
    x_pad = jnp.concatenate([xf, jnp.zeros((1, d), xf.dtype)], axis=0)
    xb = x_pad[row_token].reshape(n_blocks, MOE_BLOCK, d)

    def run_block(args):
        xblk, e = args
        h = jax.nn.silu(xblk @ w_gate[e]) * (xblk @ w_up[e])
        return h @ w_down[e]

    yb = lax.map(run_block, (xb, block_expert)).reshape(P, d)
    y = jax.ops.segment_sum(yb * row_w[:, None], row_token, num_segments=T + 1)[:T]
    return y.reshape(b, l, d)


def trunk_layer(x, pos, past_ckv, past_krope, conv_prev, h_prev, lw):
    u = rmsnorm(x, lw['norm1_w'])
    c_q, c_kv, k_r, z, xbc, dt_raw, g_mla, g_ssd = split_in(u @ lw['w_in'])
    q_nope, q_rope, ckv, krope = mla_project(c_q, c_kv, k_r, pos, lw['q_norm_w'], lw['kv_norm_w'], lw['w_uq'])
    if past_ckv is None:
        o_mla = mla_prompt(q_nope, q_rope, ckv, krope, lw['w_ukv'])
    else:
        ckv_all = jnp.concatenate([past_ckv.astype(ckv.dtype), ckv], axis=1)
        krope_all = jnp.concatenate([past_krope.astype(krope.dtype), krope], axis=1)
        o_mla = mla_sample(q_nope, q_rope, ckv_all, krope_all, lw['w_ukv'])
    o_ssd, conv_new, h_new = ssd_branch(xbc, z, dt_raw, conv_prev, h_prev, lw['conv_w'], lw['conv_b'],
                                        lw['dt_bias'], lw['a_log'], lw['d_skip'], lw['ssd_norm_w'])
    merged = jax.nn.sigmoid(g_mla) * (o_mla @ lw['w_mla_o']) + jax.nn.sigmoid(g_ssd) * (o_ssd @ lw['w_ssd_o'])
    h = x + merged @ lw['w_out']
    h = h + hier_moe(rmsnorm(h, lw['norm2_w']), lw['w_group'], lw['b_group'], lw['w_erouter'], lw['b_erouter'],
                     lw['w_gate'], lw['w_up'], lw['w_down'])
    return h, ckv, krope, conv_new, h_new


def _normal(k, shape, scale):
    return scale * jax.random.normal(k, shape, jnp.float32)


def setup_inputs(seed: int = 0) -> dict:
    key = jax.random.key(seed)
    ks = jax.random.split(key, 32)
    dt0 = jnp.exp(jax.random.uniform(ks[14], (DEPTH, SSD_HEADS), jnp.float32, math.log(1e-3), math.log(1e-1)))
    return {
        'x_prompt': _normal(ks[0], (BATCH, SEQ, D_MODEL), 1.0),
        'x_sample': _normal(ks[1], (DEC_BATCH, DEC_SEQ, D_MODEL), 1.0),
        'cache_ckv': _normal(ks[2], (DEPTH, DEC_BATCH, PAST_LEN, KV_LORA), 1.0),
        'cache_krope': _normal(ks[3], (DEPTH, DEC_BATCH, PAST_LEN, QK_ROPE), 1.0),
        'state_conv': _normal(ks[4], (DEPTH, DEC_BATCH, SSD_CONV - 1, SSD_CONV_DIM), 1.0),
        'state_ssm': _normal(ks[5], (DEPTH, DEC_BATCH, SSD_HEADS, SSD_HEADDIM, SSD_STATE), 0.1),
        'norm1_w': 1.0 + _normal(ks[6], (DEPTH, D_MODEL), 0.01),
        'w_in': _normal(ks[7], (DEPTH, D_MODEL, N_IN), D_MODEL ** -0.5),
        'q_norm_w': 1.0 + _normal(ks[8], (DEPTH, Q_LORA), 0.01),
        'kv_norm_w': 1.0 + _normal(ks[9], (DEPTH, KV_LORA), 0.01),
        'w_uq': _normal(ks[10], (DEPTH, Q_LORA, MLA_HEADS * (QK_NOPE + QK_ROPE)), Q_LORA ** -0.5),
        'w_ukv': _normal(ks[11], (DEPTH, KV_LORA, MLA_HEADS * (QK_NOPE + V_HEAD)), KV_LORA ** -0.5),
        'conv_w': _normal(ks[12], (DEPTH, SSD_CONV, SSD_CONV_DIM), SSD_CONV ** -0.5),
        'conv_b': _normal(ks[13], (DEPTH, SSD_CONV_DIM), 0.01),
        'dt_bias': dt0 + jnp.log(-jnp.expm1(-dt0)),
        'a_log': jnp.log(jax.random.uniform(ks[15], (DEPTH, SSD_HEADS), jnp.float32, 1.0, 16.0)),
        'd_skip': 1.0 + _normal(ks[16], (DEPTH, SSD_HEADS), 0.01),
        'ssd_norm_w': 1.0 + _normal(ks[17], (DEPTH, SSD_INNER), 0.01),
        'w_mla_o': _normal(ks[18], (DEPTH, MLA_HEADS * V_HEAD, D_MODEL), (MLA_HEADS * V_HEAD) ** -0.5),
        'w_ssd_o': _normal(ks[19], (DEPTH, SSD_INNER, D_MODEL), SSD_INNER ** -0.5),
        'w_out': _normal(ks[20], (DEPTH, D_MODEL, D_MODEL), D_MODEL ** -0.5),
        'norm2_w': 1.0 + _normal(ks[21], (DEPTH, D_MODEL), 0.01),
        'w_group': _normal(ks[22], (DEPTH, D_MODEL, N_GROUPS), D_MODEL ** -0.5),
        'b_group': _normal(ks[23], (DEPTH, N_GROUPS), 0.01),
        'w_erouter': _normal(ks[24], (DEPTH, D_MODEL, N_EXPERTS), D_MODEL ** -0.5),
        'b_erouter': _normal(ks[25], (DEPTH, N_EXPERTS), 0.01),
        'w_gate': _normal(ks[26], (DEPTH, N_EXPERTS, D_MODEL, D_EXPERT), D_MODEL ** -0.5),
        'w_up': _normal(ks[27], (DEPTH, N_EXPERTS, D_MODEL, D_EXPERT), D_MODEL ** -0.5),
        'w_down': _normal(ks[28], (DEPTH, N_EXPERTS, D_EXPERT, D_MODEL), D_EXPERT ** -0.5),
        'final_norm_w': 1.0 + _normal(ks[29], (D_MODEL,), 0.01),
    }


def reference(x_prompt, x_sample, cache_ckv, cache_krope, state_conv, state_ssm,
              norm1_w, w_in, q_norm_w, kv_norm_w, w_uq, w_ukv, conv_w, conv_b, dt_bias, a_log, d_skip,
              ssd_norm_w, w_mla_o, w_ssd_o, w_out, norm2_w, w_group, b_group, w_erouter, b_erouter,
              w_gate, w_up, w_down, final_norm_w):
    bp, lp = x_prompt.shape[:2]
    pos_p = jnp.arange(lp)
    pos_s = cache_ckv.shape[2] + jnp.arange(x_sample.shape[1])
    xp, xs = x_prompt, x_sample
    ckv_p, kr_p, cv_p, ssm_p = [], [], [], []
    ckv_s, kr_s, cv_s, ssm_s = [], [], [], []
    for i in range(DEPTH):
        lw = {
            'norm1_w': norm1_w[i], 'w_in': w_in[i], 'q_norm_w': q_norm_w[i], 'kv_norm_w': kv_norm_w[i],
            'w_uq': w_uq[i], 'w_ukv': w_ukv[i], 'conv_w': conv_w[i], 'conv_b': conv_b[i],
            'dt_bias': dt_bias[i], 'a_log': a_log[i], 'd_skip': d_skip[i], 'ssd_norm_w': ssd_norm_w[i],
            'w_mla_o': w_mla_o[i], 'w_ssd_o': w_ssd_o[i], 'w_out': w_out[i], 'norm2_w': norm2_w[i],
            'w_group': w_group[i], 'b_group': b_group[i], 'w_erouter': w_erouter[i], 'b_erouter': b_erouter[i],
            'w_gate': w_gate[i], 'w_up': w_up[i], 'w_down': w_down[i],
        }
        conv0 = jnp.zeros((bp, SSD_CONV - 1, SSD_CONV_DIM), xp.dtype)
        h0 = jnp.zeros((bp, SSD_HEADS, SSD_HEADDIM, SSD_STATE), xp.dtype)
        xp, a1, a2, a3, a4 = trunk_layer(xp, pos_p, None, None, conv0, h0, lw)
        ckv_p.append(a1); kr_p.append(a2); cv_p.append(a3); ssm_p.append(a4)
        xs, b1, b2, b3, b4 = trunk_layer(xs, pos_s, cache_ckv[i], cache_krope[i], state_conv[i], state_ssm[i], lw)
        ckv_s.append(b1); kr_s.append(b2); cv_s.append(b3); ssm_s.append(b4)
    y_prompt = rmsnorm(xp, final_norm_w)
    y_sample = rmsnorm(xs, final_norm_w)
    return (y_prompt, y_sample,
            jnp.stack(ckv_p), jnp.stack(kr_p), jnp.stack(cv_p), jnp.stack(ssm_p),
            jnp.stack(ckv_s), jnp.stack(kr_s), jnp.stack(cv_s), jnp.stack(ssm_s))
```

```python
import functools
import math

import numpy as np
import jax
import jax.numpy as jnp
from jax import lax
from jax.experimental import pallas as pl
from jax.experimental.pallas import tpu as pltpu

F32 = jnp.float32
BF16 = jnp.bfloat16
HI = lax.Precision.HIGHEST

D_MODEL = 4096
BATCH = 2
SEQ = 8192
DEC_BATCH = 8
DEC_SEQ = 32
PAST_LEN = 4096
CHUNK = 64
EPS = 1e-6
MLA_HEADS = 32
Q_LORA = 1024
KV_LORA = 512
QK_NOPE = 128
QK_ROPE = 64
V_HEAD = 128
ROPE_THETA = 10000.0
SCALE = (QK_NOPE + QK_ROPE) ** -0.5
SSD_INNER = 2 * D_MODEL
SSD_HEADDIM = 64
SSD_HEADS = SSD_INNER // SSD_HEADDIM
SSD_STATE = 128
SSD_GROUPS = 8
SSD_RANK = SSD_HEADS // SSD_GROUPS
SSD_GCOLS = SSD_RANK * SSD_HEADDIM
SSD_CONV = 4
SSD_CONV_DIM = SSD_INNER + 2 * SSD_GROUPS * SSD_STATE
N_GROUPS = 8
EXPERTS_PER_GROUP = 8
N_EXPERTS = N_GROUPS * EXPERTS_PER_GROUP
TOP_K = 2
D_EXPERT = 1024

T_PROMPT = BATCH * SEQ
T_SAMPLE = DEC_BATCH * DEC_SEQ
T_ALL = T_PROMPT + T_SAMPLE

OFF_CQ = 0
OFF_CKV = OFF_CQ + Q_LORA
OFF_KR = OFF_CKV + KV_LORA
OFF_Z = OFF_KR + QK_ROPE
OFF_XBC = OFF_Z + SSD_INNER
OFF_DT = OFF_XBC + SSD_CONV_DIM
OFF_GMLA = OFF_DT + SSD_HEADS
OFF_GSSD = OFF_GMLA + D_MODEL
N_IN = OFF_GSSD + D_MODEL

BIG_Z = 0
BIG_XBC = SSD_INNER
BIG_GMLA = BIG_XBC + SSD_CONV_DIM
BIG_GSSD = BIG_GMLA + D_MODEL
BIG_N = BIG_GSSD + D_MODEL
SM_CQ = 0
SM_CKV = Q_LORA
SM_DT = SM_CKV + KV_LORA
SM_KR = SM_DT + SSD_HEADS
SM_KRS = SM_KR + QK_ROPE
SM_N = SM_KRS + QK_ROPE

V7X_VMEM_LIMIT = 56 * 1024 * 1024
ROW_TILE = 256
MM_TM = 1280
MOE_TM = 256
MOE_TF = 256
MOE_TN = 1024
ATTN_T = 512


def _cparams(sem):
    return pltpu.CompilerParams(dimension_semantics=sem, vmem_limit_bytes=V7X_VMEM_LIMIT)


def _dot_nt(a, b):
    return lax.dot_general(a, b, (((1,), (1,)), ((), ())), preferred_element_type=F32)


def _rms(x, w):
    return x * lax.rsqrt(jnp.mean(x * x, axis=-1, keepdims=True) + EPS) * w


def _rms_rows_body(x_ref, w_ref, o_ref):
    o_ref[...] = _rms(x_ref[...], w_ref[...]).astype(o_ref.dtype)


def _rms_rows(x, w, out_dtype):
    m, d = x.shape
    return pl.pallas_call(
        _rms_rows_body,
        grid=(m // ROW_TILE,),
        in_specs=[pl.BlockSpec((ROW_TILE, d), lambda i: (i, 0)),
                  pl.BlockSpec((1, d), lambda i: (0, 0))],
        out_specs=pl.BlockSpec((ROW_TILE, d), lambda i: (i, 0)),
        out_shape=jax.ShapeDtypeStruct((m, d), out_dtype),
        compiler_params=_cparams(("parallel",)),
        name="rms_rows",
    )(x, w.reshape(1, d))


def _final_body(h_ref, y_ref, w_ref, o_ref):
    o_ref[...] = _rms(h_ref[...] + y_ref[...], w_ref[...])


def _final_norm(h, y_moe, w):
    m, d = h.shape
    return pl.pallas_call(
        _final_body,
        grid=(m // ROW_TILE,),
        in_specs=[pl.BlockSpec((ROW_TILE, d), lambda i: (i, 0)),
                  pl.BlockSpec((ROW_TILE, d), lambda i: (i, 0)),
                  pl.BlockSpec((1, d), lambda i: (0, 0))],
        out_specs=pl.BlockSpec((ROW_TILE, d), lambda i: (i, 0)),
        out_shape=jax.ShapeDtypeStruct((m, d), F32),
        compiler_params=_cparams(("parallel",)),
        name="final_norm",
    )(h, y_moe, w.reshape(1, d))


def _mm_body(epilogue, a_ref, w_ref, *rest):
    o_ref = rest[-1]
    acc = jnp.dot(a_ref[...], w_ref[...], preferred_element_type=F32)
    if epilogue is not None:
        acc = epilogue(acc, *[r[...] for r in rest[:-1]])
    o_ref[...] = acc.astype(o_ref.dtype)


def _mm(a, w, *, tm, tn, out_dtype, extras=(), epilogue=None, name="mm"):
    m, k = a.shape
    n = w.shape[1]
    in_specs = [pl.BlockSpec((tm, k), lambda i, j: (i, 0)),
                pl.BlockSpec((k, tn), lambda i, j: (0, j))]
    args = [a, w]
    for arr, off in extras:
        in_specs.append(pl.BlockSpec((tm, tn), functools.partial(lambda i, j, off: (i, j + off), off=off)))
        args.append(arr)
    return pl.pallas_call(
        functools.partial(_mm_body, epilogue),
        grid=(m // tm, n // tn),
        in_specs=in_specs,
        out_specs=pl.BlockSpec((tm, tn), lambda i, j: (i, j)),
        out_shape=jax.ShapeDtypeStruct((m, n), out_dtype),
        compiler_params=_cparams(("parallel", "parallel")),
        name=name,
    )(*args)


def _mla_prep_body(s_ref, qw_ref, kvw_ref, cos_ref, sin_ref, cqn_ref, ckv_ref, ckvb_ref, kr_ref):
    cqn_ref[...] = _rms(s_ref[:, SM_CQ:SM_CQ + Q_LORA], qw_ref[...]).astype(BF16)
    c = _rms(s_ref[:, SM_CKV:SM_CKV + KV_LORA], kvw_ref[...])
    ckv_ref[...] = c
    ckvb_ref[...] = c.astype(BF16)
    kr_ref[...] = (s_ref[:, SM_KR:SM_KR + QK_ROPE] * cos_ref[...]
                   + s_ref[:, SM_KRS:SM_KRS + QK_ROPE] * sin_ref[...])


def _mla_prep(small, q_norm_w, kv_norm_w, cos2, sin2):
    m = small.shape[0]
    row = lambda n: pl.BlockSpec((ROW_TILE, n), lambda i: (i, 0))
    vec = lambda n: pl.BlockSpec((1, n), lambda i: (0, 0))
    return pl.pallas_call(
        _mla_prep_body,
        grid=(m // ROW_TILE,),
        in_specs=[row(SM_N), vec(Q_LORA), vec(KV_LORA), row(QK_ROPE), row(QK_ROPE)],
        out_specs=[row(Q_LORA), row(KV_LORA), row(KV_LORA), row(QK_ROPE)],
        out_shape=[jax.ShapeDtypeStruct((m, Q_LORA), BF16),
                   jax.ShapeDtypeStruct((m, KV_LORA), F32),
                   jax.ShapeDtypeStruct((m, KV_LORA), BF16),
                   jax.ShapeDtypeStruct((m, QK_ROPE), F32)],
        compiler_params=_cparams(("parallel",)),
        name="mla_prep",
    )(small, q_norm_w.reshape(1, -1), kv_norm_w.reshape(1, -1), cos2, sin2)


def _q_rope_body(a_ref, w_ref, ws_ref, cos_ref, sin_ref, o_ref):
    reps = o_ref.shape[1] // cos_ref.shape[1]
    a = a_ref[...]
    r = jnp.dot(a, w_ref[...], preferred_element_type=F32)
    rs = jnp.dot(a, ws_ref[...], preferred_element_type=F32)
    cos = jnp.tile(cos_ref[...], (1, reps))
    sin = jnp.tile(sin_ref[...], (1, reps))
    o_ref[...] = (r * cos + rs * sin).astype(o_ref.dtype)


def _q_rope(cqn, w_r, w_rs, cos128, sin128, *, tm, tn):
    m, k = cqn.shape
    n = w_r.shape[1]
    return pl.pallas_call(
        _q_rope_body,
        grid=(m // tm, n // tn),
        in_specs=[pl.BlockSpec((tm, k), lambda i, j: (i, 0)),
                  pl.BlockSpec((k, tn), lambda i, j: (0, j)),
                  pl.BlockSpec((k, tn), lambda i, j: (0, j)),
                  pl.BlockSpec((tm, 128), lambda i, j: (i, 0)),
                  pl.BlockSpec((tm, 128), lambda i, j: (i, 0))],
        out_specs=pl.BlockSpec((tm, tn), lambda i, j: (i, j)),
        out_shape=jax.ShapeDtypeStruct((m, n), BF16),
        compiler_params=_cparams(("parallel", "parallel")),
        name="q_rope",
    )(cqn, w_r, w_rs, cos128, sin128)


def _attn_prompt_body(qt_ref, kt_ref, qn_ref, qr_ref, kv_ref, kr_ref, o_ref, m_sc, l_sc, acc_sc):
    p = pl.program_id(2)
    qi = qt_ref[p]
    ki = kt_ref[p]
    t = qn_ref.shape[0]

    @pl.when(ki == 0)
    def _():
        m_sc[...] = jnp.full(m_sc.shape, -jnp.inf, F32)
        l_sc[...] = jnp.zeros(l_sc.shape, F32)
        acc_sc[...] = jnp.zeros(acc_sc.shape, F32)

    def step(masked):
        krb = kr_ref[...].astype(BF16)
        if masked:
            rc = lax.broadcasted_iota(jnp.int32, (t, t), 0) // CHUNK
            cc = lax.broadcasted_iota(jnp.int32, (t, t), 1) // CHUNK
            ok = cc <= rc
        for hh in range(2):
            qn = qn_ref[:, hh * QK_NOPE:(hh + 1) * QK_NOPE]
            qr = qr_ref[:, hh * QK_ROPE:(hh + 1) * QK_ROPE]
            kn = kv_ref[:, hh * 256:hh * 256 + QK_NOPE]
            v = kv_ref[:, hh * 256 + QK_NOPE:(hh + 1) * 256]
            s = (_dot_nt(qn, kn) + _dot_nt(qr, krb)) * SCALE
            if masked:
                s = jnp.where(ok, s, -jnp.inf)
            m_old = m_sc[hh]
            m_new = jnp.maximum(m_old, jnp.max(s, axis=1, keepdims=True))
            alpha = jnp.exp(m_old - m_new)
            pm = jnp.exp(s - m_new[:, :1])
            l_sc[hh] = alpha * l_sc[hh] + jnp.sum(pm, axis=1, keepdims=True)
            acc_sc[hh] = alpha * acc_sc[hh] + jnp.dot(pm.astype(BF16), v, preferred_element_type=F32)
            m_sc[hh] = m_new

    @pl.when(ki < qi)
    def _():
        step(False)

    @pl.when(ki == qi)
    def _():
        step(True)
        for hh in range(2):
            o_ref[:, hh * V_HEAD:(hh + 1) * V_HEAD] = (acc_sc[hh] / l_sc[hh]).astype(o_ref.dtype)


def _attn_prompt(qn, qr, kv, kr):
    t = ATTN_T
    nq = SEQ // t
    pairs = [(i, j) for i in range(nq) for j in range(i + 1)]
    qt = jnp.asarray(np.array([p[0] for p in pairs], np.int32))
    kt = jnp.asarray(np.array([p[1] for p in pairs], np.int32))
    qrow = lambda b, hp, p, qt, kt: (b * nq + qt[p], hp)
    krow = lambda b, hp, p, qt, kt: (b * nq + kt[p], hp)
    grid_spec = pltpu.PrefetchScalarGridSpec(
        num_scalar_prefetch=2,
        grid=(BATCH, MLA_HEADS // 2, len(pairs)),
        in_specs=[pl.BlockSpec((t, 2 * QK_NOPE), qrow),
                  pl.BlockSpec((t, 2 * QK_ROPE), qrow),
                  pl.BlockSpec((t, 2 * (QK_NOPE + V_HEAD)), krow),
                  pl.BlockSpec((t, QK_ROPE), lambda b, hp, p, qt, kt: (b * nq + kt[p], 0))],
        out_specs=pl.BlockSpec((t, 2 * V_HEAD), qrow),
        scratch_shapes=[pltpu.VMEM((2, t, 128), F32), pltpu.VMEM((2, t, 128), F32),
                        pltpu.VMEM((2, t, V_HEAD), F32)],
    )
    return pl.pallas_call(
        _attn_prompt_body,
        grid_spec=grid_spec,
        out_shape=jax.ShapeDtypeStruct((T_PROMPT, MLA_HEADS * V_HEAD), BF16),
        compiler_params=_cparams(("parallel", "parallel", "arbitrary")),
        name="attn_prompt",
    )(qt, kt, qn, qr, kv, kr)


def _bmm_body(a_ref, w_ref, o_ref):
    o_ref[0] = jnp.dot(a_ref[0], w_ref[0], preferred_element_type=F32).astype(o_ref.dtype)


def _bmm(a, w, out_dtype, name):
    h, m, k = a.shape
    n = w.shape[2]
    return pl.pallas_call(
        _bmm_body,
        grid=(h,),
        in_specs=[pl.BlockSpec((1, m, k), lambda i: (i, 0, 0)),
                  pl.BlockSpec((1, k, n), lambda i: (i, 0, 0))],
        out_specs=pl.BlockSpec((1, m, n), lambda i: (i, 0, 0)),
        out_shape=jax.ShapeDtypeStruct((h, m, n), out_dtype),
        compiler_params=_cparams(("parallel",)),
        name=name,
    )(a, w)


def _attn_sample_body(ql_ref, qr_ref, cc_ref, ck_ref, nc_ref, nk_ref, o_ref, ccb_sc, ckb_sc):
    @pl.when(pl.program_id(1) == 0)
    def _():
        ccb_sc[...] = cc_ref[0].astype(BF16)
        ckb_sc[...] = ck_ref[0].astype(BF16)

    ql = ql_ref[0]
    qr = qr_ref[0]
    ncb = nc_ref[...].astype(BF16)
    nkb = nk_ref[...].astype(BF16)
    s1 = (_dot_nt(ql, ccb_sc[...]) + _dot_nt(qr, ckb_sc[...])) * SCALE
    s2 = (_dot_nt(ql, ncb) + _dot_nt(qr, nkb)) * SCALE
    m = jnp.maximum(jnp.max(s1, axis=1, keepdims=True), jnp.max(s2, axis=1, keepdims=True))
    p1 = jnp.exp(s1 - m)
    p2 = jnp.exp(s2 - m)
    l = jnp.sum(p1, axis=1, keepdims=True) + jnp.sum(p2, axis=1, keepdims=True)
    o = (jnp.dot(p1.astype(BF16), ccb_sc[...], preferred_element_type=F32)
         + jnp.dot(p2.astype(BF16), ncb, preferred_element_type=F32))
    o_ref[0] = (o / l).astype(o_ref.dtype)


def _attn_sample(q_lat, q_rope, cache_ckv, cache_krope, ckv, krope):
    rows = q_lat.shape[1]
    tr = 256
    new_blk = T_PROMPT // DEC_SEQ
    return pl.pallas_call(
        _attn_sample_body,
        grid=(DEC_BATCH, rows // tr),
        in_specs=[pl.BlockSpec((1, tr, KV_LORA), lambda b, r: (b, r, 0)),
                  pl.BlockSpec((1, tr, QK_ROPE), lambda b, r: (b, r, 0)),
                  pl.BlockSpec((1, PAST_LEN, KV_LORA), lambda b, r: (b, 0, 0)),
                  pl.BlockSpec((1, PAST_LEN, QK_ROPE), lambda b, r: (b, 0, 0)),
                  pl.BlockSpec((DEC_SEQ, KV_LORA), lambda b, r: (new_blk + b, 0)),
                  pl.BlockSpec((DEC_SEQ, QK_ROPE), lambda b, r: (new_blk + b, 0))],
        out_specs=pl.BlockSpec((1, tr, KV_LORA), lambda b, r: (b, r, 0)),
        out_shape=jax.ShapeDtypeStruct((DEC_BATCH, rows, KV_LORA), BF16),
        scratch_shapes=[pltpu.VMEM((PAST_LEN, KV_LORA), BF16), pltpu.VMEM((PAST_LEN, QK_ROPE), BF16)],
        compiler_params=_cparams(("parallel", "arbitrary")),
        name="attn_sample",
    )(q_lat, q_rope, cache_ckv, cache_krope, ckv, krope)


def _conv_body(x_ref, prev_ref, w_ref, b_ref, o_ref, ext_sc):
    tl = x_ref.shape[0]

    @pl.when(pl.program_id(2) == 0)
    def _():
        ext_sc[0:8, :] = prev_ref[0]

    @pl.when(pl.program_id(2) != 0)
    def _():
        ext_sc[0:8, :] = ext_sc[tl:tl + 8, :]

    ext_sc[8:8 + tl, :] = x_ref[...]
    acc = b_ref[...] + ext_sc[5:5 + tl, :] * w_ref[0:1, :]
    for k in range(1, SSD_CONV):
        acc = acc + ext_sc[5 + k:5 + k + tl, :] * w_ref[k:k + 1, :]
    o_ref[...] = acc * jax.nn.sigmoid(acc)


def _conv_silu(big, prev8, conv_w, conv_b, *, nb, seq, tl, row_off):
    tc = 1024
    nrt = seq // tl
    rb0 = row_off // tl
    cb0 = BIG_XBC // tc
    return pl.pallas_call(
        _conv_body,
        grid=(nb, SSD_CONV_DIM // tc, nrt),
        in_specs=[pl.BlockSpec((tl, tc), lambda b, j, r: (rb0 + b * nrt + r, cb0 + j)),
                  pl.BlockSpec((1, 8, tc), lambda b, j, r: (b, 0, j)),
                  pl.BlockSpec((SSD_CONV, tc), lambda b, j, r: (0, j)),
                  pl.BlockSpec((1, tc), lambda b, j, r: (0, j))],
        out_specs=pl.BlockSpec((tl, tc), lambda b, j, r: (b * nrt + r, j)),
        out_shape=jax.ShapeDtypeStruct((nb * seq, SSD_CONV_DIM), F32),
        scratch_shapes=[pltpu.VMEM((tl + 8, tc), F32)],
        compiler_params=_cparams(("parallel", "parallel", "arbitrary")),
        name="conv_silu",
    )(big, prev8, conv_w, conv_b.reshape(1, -1))


def _softplus(x):
    return jnp.maximum(x, 0.0) + jnp.log(1.0 + jnp.exp(-jnp.abs(x)))


def _ssd_body(has_h0, x_ref, b_ref, c_ref, z_ref, dt_ref, dtt_ref, bias_ref, biast_ref, al_ref, alt_ref,
              dsk_ref, nw_ref, *rest):
    if has_h0:
        h0_ref, y_ref, hout_ref, h_sc, yd_sc = rest
    else:
        y_ref, hout_ref, h_sc, yd_sc = rest
    lc = x_ref.shape[0]
    R, P = SSD_RANK, SSD_HEADDIM

    @pl.when(pl.program_id(2) == 0)
    def _():
        if has_h0:
            h_sc[...] = h0_ref[0]
        else:
            h_sc[...] = jnp.zeros(h_sc.shape, F32)

    dt = _softplus(dt_ref[0] + bias_ref[0])
    dtt = _softplus(dtt_ref[0, 0] + biast_ref[0])
    da = dt * (-jnp.exp(al_ref[0]))
    dat = dtt * (-jnp.exp(alt_ref[0]))
    ri = lax.broadcasted_iota(jnp.int32, (lc, lc), 0)
    ci = lax.broadcasted_iota(jnp.int32, (lc, lc), 1)
    tril = ri >= ci
    a_cs = jnp.dot(tril.astype(F32), da, precision=HI, preferred_element_type=F32)
    a_cst = jnp.dot(dat, (ri <= ci).astype(F32), precision=HI, preferred_element_type=F32)
    er = lax.broadcasted_iota(jnp.int32, (R, R * P), 0)
    ec = lax.broadcasted_iota(jnp.int32, (R, R * P), 1) // P
    expand = (er == ec).astype(F32)
    etr = lax.broadcasted_iota(jnp.int32, (R * P, R), 0) // P
    etc = lax.broadcasted_iota(jnp.int32, (R * P, R), 1)
    expand_t = (etr == etc).astype(F32)
    a_exp = jnp.dot(a_cs, expand, precision=HI, preferred_element_type=F32)
    dt_exp = jnp.dot(dt, expand, precision=HI, preferred_element_type=F32)
    a_last = a_exp[lc - 1:lc, :]
    a_expt = jnp.dot(expand_t, a_cst, precision=HI, preferred_element_type=F32)
    dec_col = jnp.exp(a_expt[:, lc - 1:lc])

    x = x_ref[...]
    xdt = x * dt_exp
    bm = b_ref[...].astype(BF16)
    cm = c_ref[...].astype(BF16)
    h = h_sc[...]
    cb = _dot_nt(cm, bm)
    y_off = _dot_nt(cm, h.astype(BF16)) * jnp.exp(a_exp)
    xdt_b = xdt.astype(BF16)
    for r in range(R):
        seg = a_cs[:, r:r + 1] - a_cst[r:r + 1, :]
        decay = jnp.exp(jnp.where(tril, seg, -jnp.inf))
        mr = (cb * decay).astype(BF16)
        yd_sc[:, r * P:(r + 1) * P] = jnp.dot(mr, xdt_b[:, r * P:(r + 1) * P], preferred_element_type=F32)
    wgt = (xdt * jnp.exp(a_last - a_exp)).astype(BF16)
    states = lax.dot_general(wgt, bm, (((0,), (0,)), ((), ())), preferred_element_type=F32)
    h_new = h * dec_col + states
    h_sc[...] = h_new
    hout_ref[0] = h_new

    y = yd_sc[...] + y_off + x * dsk_ref[...]
    z = z_ref[...]
    y = y * (z * jax.nn.sigmoid(z))
    y = y * lax.rsqrt(jnp.mean(y * y, axis=-1, keepdims=True) + EPS) * nw_ref[...]
    y_ref[...] = y.astype(y_ref.dtype)


def _ssd(xbc, big, dt_g, dtt_g, h0, dt_bias, a_log, d_skip, ssd_norm_w, *, nb, seq, lc, z_row_off):
    nc = seq // lc
    G, R, GC, N = SSD_GROUPS, SSD_RANK, SSD_GCOLS, SSD_STATE
    zb0 = z_row_off // lc
    row = lambda b, g, c: b * nc + c
    in_specs = [
        pl.BlockSpec((lc, GC), lambda b, g, c: (row(b, g, c), g)),
        pl.BlockSpec((lc, N), lambda b, g, c: (row(b, g, c), SSD_INNER // N + g)),
        pl.BlockSpec((lc, N), lambda b, g, c: (row(b, g, c), SSD_INNER // N + G + g)),
        pl.BlockSpec((lc, GC), lambda b, g, c: (zb0 + row(b, g, c), BIG_Z // GC + g)),
        pl.BlockSpec((1, lc, R), lambda b, g, c: (g, row(b, g, c), 0)),
        pl.BlockSpec((1, 1, R, lc), lambda b, g, c: (g, row(b, g, c), 0, 0)),
        pl.BlockSpec((1, 1, R), lambda b, g, c: (g, 0, 0)),
        pl.BlockSpec((1, R, 1), lambda b, g, c: (g, 0, 0)),
        pl.BlockSpec((1, 1, R), lambda b, g, c: (g, 0, 0)),
        pl.BlockSpec((1, R, 1), lambda b, g, c: (g, 0, 0)),
        pl.BlockSpec((1, GC), lambda b, g, c: (0, g)),
        pl.BlockSpec((1, GC), lambda b, g, c: (0, g)),
    ]
    args = [xbc, xbc, xbc, big, dt_g, dtt_g,
            dt_bias.reshape(G, 1, R), dt_bias.reshape(G, R, 1),
            a_log.reshape(G, 1, R), a_log.reshape(G, R, 1),
            jnp.repeat(d_skip, SSD_HEADDIM).reshape(1, SSD_INNER), ssd_norm_w.reshape(1, SSD_INNER)]
    if h0 is not None:
        in_specs.append(pl.BlockSpec((1, GC, N), lambda b, g, c: (b, g, 0)))
        args.append(h0)
    return pl.pallas_call(
        functools.partial(_ssd_body, h0 is not None),
        grid=(nb, G, nc),
        in_specs=in_specs,
        out_specs=[pl.BlockSpec((lc, GC), lambda b, g, c: (row(b, g, c), g)),
                   pl.BlockSpec((1, GC, N), lambda b, g, c: (b, g, 0))],
        out_shape=[jax.ShapeDtypeStruct((nb * seq, SSD_INNER), BF16),
                   jax.ShapeDtypeStruct((nb, SSD_HEADS * SSD_HEADDIM, N), F32)],
        scratch_shapes=[pltpu.VMEM((GC, N), F32), pltpu.VMEM((lc, GC), F32)],
        compiler_params=_cparams(("parallel", "parallel", "arbitrary")),
        name="ssd",
    )(*args)


def _norm_router_body(h_ref, w_ref, wr_ref, br_ref, xn_ref, lg_ref):
    xn = _rms(h_ref[...], w_ref[...])
    xn_ref[...] = xn.astype(BF16)
    lg_ref[...] = jnp.dot(xn, wr_ref[...], precision=HI, preferred_element_type=F32) + br_ref[...]


def _norm_router(h, norm_w, w_router, b_router):
    m, d = h.shape
    n = w_router.shape[1]
    return pl.pallas_call(
        _norm_router_body,
        grid=(m // ROW_TILE,),
        in_specs=[pl.BlockSpec((ROW_TILE, d), lambda i: (i, 0)),
                  pl.BlockSpec((1, d), lambda i: (0, 0)),
                  pl.BlockSpec((d, n), lambda i: (0, 0)),
                  pl.BlockSpec((1, n), lambda i: (0, 0))],
        out_specs=[pl.BlockSpec((ROW_TILE, d), lambda i: (i, 0)),
                   pl.BlockSpec((ROW_TILE, n), lambda i: (i, 0))],
        out_shape=[jax.ShapeDtypeStruct((m, d), BF16), jax.ShapeDtypeStruct((m, n), F32)],
        compiler_params=_cparams(("parallel",)),
        name="norm_router",
    )(h, norm_w.reshape(1, d), w_router, b_router)


def _moe_up_body(be_ref, first_ref, nused_ref, x_ref, wg_ref, wu_ref, o_ref, wgb_sc, wub_sc):
    blk = pl.program_id(1)

    @pl.when(first_ref[blk] == 1)
    def _():
        wgb_sc[...] = wg_ref[0].astype(BF16)
        wub_sc[...] = wu_ref[0].astype(BF16)

    @pl.when(blk < nused_ref[0])
    def _():
        x = x_ref[...]
        g = jnp.dot(x, wgb_sc[...], preferred_element_type=F32)
        u = jnp.dot(x, wub_sc[...], preferred_element_type=F32)
        o_ref[...] = (g * jax.nn.sigmoid(g) * u).astype(o_ref.dtype)

    @pl.when(blk >= nused_ref[0])
    def _():
        o_ref[...] = jnp.zeros(o_ref.shape, o_ref.dtype)


def _moe_up(be, first, nused, xs, w_gate, w_up):
    p, d = xs.shape
    nblk = p // MOE_TM
    grid_spec = pltpu.PrefetchScalarGridSpec(
        num_scalar_prefetch=3,
        grid=(D_EXPERT // MOE_TF, nblk),
        in_specs=[pl.BlockSpec((MOE_TM, d), lambda f, i, be, fi, nu: (i, 0)),
                  pl.BlockSpec((1, d, MOE_TF), lambda f, i, be, fi, nu: (be[i], 0, f)),
                  pl.BlockSpec((1, d, MOE_TF), lambda f, i, be, fi, nu: (be[i], 0, f))],
        out_specs=pl.BlockSpec((MOE_TM, MOE_TF), lambda f, i, be, fi, nu: (i, f)),
        scratch_shapes=[pltpu.VMEM((d, MOE_TF), BF16), pltpu.VMEM((d, MOE_TF), BF16)],
    )
    return pl.pallas_call(
        _moe_up_body,
        grid_spec=grid_spec,
        out_shape=jax.ShapeDtypeStruct((p, D_EXPERT), BF16),
        compiler_params=_cparams(("arbitrary", "arbitrary")),
        name="moe_up",
    )(be, first, nused, xs, w_gate, w_up)


def _moe_down_body(be_ref, first_ref, nused_ref, h_ref, wd_ref, rw_ref, o_ref, wdb_sc):
    blk = pl.program_id(1)

    @pl.when(first_ref[blk] == 1)
    def _():
        wdb_sc[...] = wd_ref[0].astype(BF16)

    @pl.when(blk < nused_ref[0])
    def _():
        o_ref[...] = jnp.dot(h_ref[...], wdb_sc[...], preferred_element_type=F32) * rw_ref[...]

    @pl.when(blk >= nused_ref[0])
    def _():
        o_ref[...] = jnp.zeros(o_ref.shape, o_ref.dtype)


def _moe_down(be, first, nused, hact, w_down, row_w):
    p, f = hact.shape
    d = w_down.shape[2]
    nblk = p // MOE_TM
    grid_spec = pltpu.PrefetchScalarGridSpec(
        num_scalar_prefetch=3,
        grid=(d // MOE_TN, nblk),
        in_specs=[pl.BlockSpec((MOE_TM, f), lambda n, i, be, fi, nu: (i, 0)),
                  pl.BlockSpec((1, f, MOE_TN), lambda n, i, be, fi, nu: (be[i], 0, n)),
                  pl.BlockSpec((MOE_TM, 1), lambda n, i, be, fi, nu: (i, 0))],
        out_specs=pl.BlockSpec((MOE_TM, MOE_TN), lambda n, i, be, fi, nu: (i, n)),
        scratch_shapes=[pltpu.VMEM((f, MOE_TN), BF16)],
    )
    return pl.pallas_call(
        _moe_down_body,
        grid_spec=grid_spec,
        out_shape=jax.ShapeDtypeStruct((p, d), F32),
        compiler_params=_cparams(("arbitrary", "arbitrary")),
        name="moe_down",
    )(be, first, nused, hact, w_down, row_w)


def _route(logits):
    t = logits.shape[0]
    g_logits = logits[:, :N_GROUPS]
    g_sel = jnp.argmax(g_logits, axis=-1)
    g_w = jnp.max(jax.nn.softmax(g_logits, axis=-1), axis=-1)
    e_logits = logits[:, N_GROUPS:N_GROUPS + N_EXPERTS].reshape(t, N_GROUPS, EXPERTS_PER_GROUP)
    e_in = jnp.take_along_axis(e_logits, g_sel[:, None, None], axis=1)[:, 0]
    e_val, e_idx = lax.top_k(e_in, TOP_K)
    e_w = jax.nn.softmax(e_val, axis=-1) * g_w[:, None]
    expert_id = (g_sel[:, None] * EXPERTS_PER_GROUP + e_idx).reshape(-1).astype(jnp.int32)
    a = t * TOP_K
    token_id = jnp.repeat(jnp.arange(t, dtype=jnp.int32), TOP_K)
    order = jnp.argsort(expert_id)
    e_sorted = expert_id[order]
    counts = jnp.zeros((N_EXPERTS,), jnp.int32).at[expert_id].add(1)
    padded = (counts + MOE_TM - 1) // MOE_TM * MOE_TM
    pad_end = jnp.cumsum(padded)
    pad_start = pad_end - padded
    start = jnp.cumsum(counts) - counts
    dest = pad_start[e_sorted] + jnp.arange(a, dtype=jnp.int32) - start[e_sorted]
    nblk = a // MOE_TM + N_EXPERTS
    rows = nblk * MOE_TM
    row_token = jnp.zeros((rows,), jnp.int32).at[dest].set(token_id[order])
    row_w = jnp.zeros((rows,), F32).at[dest].set(e_w.reshape(-1)[order])
    pos = jnp.zeros((a,), jnp.int32).at[order].set(dest).reshape(t, TOP_K)
    nused = (pad_end[-1] // MOE_TM).astype(jnp.int32)
    blk = jnp.arange(nblk, dtype=jnp.int32)
    be = jnp.minimum(jnp.searchsorted(pad_end, blk * MOE_TM, side='right'), N_EXPERTS - 1).astype(jnp.int32)
    be = jnp.where(blk < nused, be, be[jnp.maximum(nused - 1, 0)])
    first = jnp.concatenate([jnp.ones((1,), jnp.int32), (be[1:] != be[:-1]).astype(jnp.int32)])
    return row_token, row_w, pos, be, first, nused.reshape(1)


def _rope_tables():
    half = QK_ROPE // 2
    inv_freq = ROPE_THETA ** (-jnp.arange(half, dtype=F32) / half)
    pos = jnp.concatenate([jnp.tile(jnp.arange(SEQ), BATCH),
                           jnp.tile(PAST_LEN + jnp.arange(DEC_SEQ), DEC_BATCH)]).astype(F32)
    ang = pos[:, None] * inv_freq[None, :]
    cos, sin = jnp.cos(ang), jnp.sin(ang)
    cos2 = jnp.concatenate([cos, cos], axis=1)
    sin2 = jnp.concatenate([-sin, sin], axis=1)
    return cos2, sin2


def kernel(x_prompt, x_sample, cache_ckv, cache_krope, state_conv, state_ssm, norm1_w, w_in, q_norm_w, kv_norm_w, w_uq, w_ukv, conv_w, conv_b, dt_bias, a_log, d_skip, ssd_norm_w, w_mla_o, w_ssd_o, w_out, norm2_w, w_group, b_group, w_erouter, b_erouter, w_gate, w_up, w_down, final_norm_w):
    swap = np.concatenate([np.arange(QK_ROPE // 2, QK_ROPE), np.arange(QK_ROPE // 2)])
    x_all = jnp.concatenate([x_prompt.reshape(T_PROMPT, D_MODEL), x_sample.reshape(T_SAMPLE, D_MODEL)], axis=0)
    cos2, sin2 = _rope_tables()
    cos128, sin128 = jnp.tile(cos2, (1, 2)), jnp.tile(sin2, (1, 2))

    wi = w_in[0]
    w_kr = wi[:, OFF_KR:OFF_KR + QK_ROPE]
    w_small = jnp.concatenate([wi[:, OFF_CQ:OFF_KR], wi[:, OFF_DT:OFF_DT + SSD_HEADS], w_kr, w_kr[:, swap]],
                              axis=1).astype(BF16)
    w_big = jnp.concatenate([wi[:, OFF_Z:OFF_DT], wi[:, OFF_GMLA:]], axis=1).astype(BF16)
    wq = w_uq[0].reshape(Q_LORA, MLA_HEADS, QK_NOPE + QK_ROPE)
    wq_nope = wq[:, :, :QK_NOPE].reshape(Q_LORA, MLA_HEADS * QK_NOPE).astype(BF16)
    wq_rope = wq[:, :, QK_NOPE:]
    wq_r = wq_rope.reshape(Q_LORA, MLA_HEADS * QK_ROPE).astype(BF16)
    wq_rs = wq_rope[:, :, swap].reshape(Q_LORA, MLA_HEADS * QK_ROPE).astype(BF16)
    w_ukv_b = w_ukv[0].astype(BF16)
    wkv3 = w_ukv[0].reshape(KV_LORA, MLA_HEADS, QK_NOPE + V_HEAD)
    w_uk_t = jnp.transpose(wkv3[:, :, :QK_NOPE], (1, 2, 0)).astype(BF16)
    w_uv_h = jnp.transpose(wkv3[:, :, QK_NOPE:], (1, 0, 2)).astype(BF16)

    u = _rms_rows(x_all, norm1_w[0], BF16)
    small = _mm(u, w_small, tm=MM_TM, tn=SM_N // 2, out_dtype=F32, name="proj_small")
    big = _mm(u, w_big, tm=MM_TM, tn=512, out_dtype=F32, name="proj_big")

    cqn, ckv, ckv_b, krope = _mla_prep(small, q_norm_w[0], kv_norm_w[0], cos2, sin2)
    qn = _mm(cqn, wq_nope, tm=MM_TM, tn=1024, out_dtype=BF16, name="q_nope")
    qr = _q_rope(cqn, wq_r, wq_rs, cos128, sin128, tm=MM_TM, tn=1024)
    kv = _mm(ckv_b, w_ukv_b, tm=MM_TM, tn=1024, out_dtype=BF16, name="kv_up")
    o_mla_p = _attn_prompt(qn, qr, kv, krope)

    qn_s = qn[T_PROMPT:].reshape(T_SAMPLE, MLA_HEADS, QK_NOPE).transpose(1, 0, 2)
    q_lat = _bmm(qn_s, w_uk_t, BF16, "q_absorb")
    q_lat = q_lat.reshape(MLA_HEADS, DEC_BATCH, DEC_SEQ, KV_LORA).transpose(1, 0, 2, 3)
    q_lat = q_lat.reshape(DEC_BATCH, MLA_HEADS * DEC_SEQ, KV_LORA)
    qr_s = qr[T_PROMPT:].reshape(DEC_BATCH, DEC_SEQ, MLA_HEADS, QK_ROPE).transpose(0, 2, 1, 3)
    qr_s = qr_s.reshape(DEC_BATCH, MLA_HEADS * DEC_SEQ, QK_ROPE)
    o_lat = _attn_sample(q_lat, qr_s, cache_ckv[0], cache_krope[0], ckv, krope)
    o_lat = o_lat.reshape(DEC_BATCH, MLA_HEADS, DEC_SEQ, KV_LORA).transpose(1, 0, 2, 3)
    o_lat = o_lat.reshape(MLA_HEADS, T_SAMPLE, KV_LORA)
    o_mla_s = _bmm(o_lat, w_uv_h, BF16, "v_absorb")
    o_mla_s = o_mla_s.transpose(1, 0, 2).reshape(T_SAMPLE, MLA_HEADS * V_HEAD)
    o_mla = jnp.concatenate([o_mla_p, o_mla_s], axis=0)

    prev_p = jnp.zeros((BATCH, 8, SSD_CONV_DIM), F32)
    prev_s = jnp.concatenate([jnp.zeros((DEC_BATCH, 8 - (SSD_CONV - 1), SSD_CONV_DIM), F32), state_conv[0]], axis=1)
    xbc_p = _conv_silu(big, prev_p, conv_w[0], conv_b[0], nb=BATCH, seq=SEQ, tl=512, row_off=0)
    xbc_s = _conv_silu(big, prev_s, conv_w[0], conv_b[0], nb=DEC_BATCH, seq=DEC_SEQ, tl=DEC_SEQ, row_off=T_PROMPT)
    dt_raw = small[:, SM_DT:SM_DT + SSD_HEADS]

    def dt_layouts(d, lc):
        rows = d.shape[0]
        dg = d.reshape(rows, SSD_GROUPS, SSD_RANK).transpose(1, 0, 2)
        dgt = dg.reshape(SSD_GROUPS, rows // lc, lc, SSD_RANK).transpose(0, 1, 3, 2)
        return dg, dgt

    dt_p, dtt_p = dt_layouts(dt_raw[:T_PROMPT], CHUNK)
    dt_s, dtt_s = dt_layouts(dt_raw[T_PROMPT:], DEC_SEQ)
    h0_s = state_ssm[0].reshape(DEC_BATCH, SSD_HEADS * SSD_HEADDIM, SSD_STATE)
    y_p, ssm_p = _ssd(xbc_p, big, dt_p, dtt_p, None, dt_bias[0], a_log[0], d_skip[0], ssd_norm_w[0],
                      nb=BATCH, seq=SEQ, lc=CHUNK, z_row_off=0)
    y_s, ssm_s = _ssd(xbc_s, big, dt_s, dtt_s, h0_s, dt_bias[0], a_log[0], d_skip[0], ssd_norm_w[0],
                      nb=DEC_BATCH, seq=DEC_SEQ, lc=DEC_SEQ, z_row_off=T_PROMPT)
    o_ssd = jnp.concatenate([y_p, y_s], axis=0)

    gate = lambda acc, g: jax.nn.sigmoid(g) * acc
    m1 = _mm(o_mla, w_mla_o[0].astype(BF16), tm=640, tn=1024, out_dtype=F32,
             extras=[(big, BIG_GMLA // 1024)], epilogue=gate, name="mla_out")
    merged = _mm(o_ssd, w_ssd_o[0].astype(BF16), tm=640, tn=512, out_dtype=BF16,
                 extras=[(big, BIG_GSSD // 512), (m1, 0)],
                 epilogue=lambda acc, g, m: m + jax.nn.sigmoid(g) * acc, name="ssd_out")
    h = _mm(merged, w_out[0].astype(BF16), tm=640, tn=1024, out_dtype=F32,
            extras=[(x_all, 0)], epilogue=lambda acc, x: x + acc, name="out_proj")

    n_r = 128
    w_router = jnp.concatenate([w_group[0], w_erouter[0],
                                jnp.zeros((D_MODEL, n_r - N_GROUPS - N_EXPERTS), F32)], axis=1)
    b_router = jnp.concatenate([b_group[0], b_erouter[0], jnp.zeros((n_r - N_GROUPS - N_EXPERTS,), F32)])
    xn, logits = _norm_router(h, norm2_w[0], w_router, b_router.reshape(1, n_r))
    row_token, row_w, pos, be, first, nused = _route(logits)
    xs = jnp.take(xn, row_token, axis=0)
    hact = _moe_up(be, first, nused, xs, w_gate[0], w_up[0])
    yb = _moe_down(be, first, nused, hact, w_down[0], row_w.reshape(-1, 1))
    y_moe = jnp.take(yb, pos[:, 0], axis=0) + jnp.take(yb, pos[:, 1], axis=0)
    y_all = _final_norm(h, y_moe, final_norm_w)

    y_prompt = y_all[:T_PROMPT].reshape(BATCH, SEQ, D_MODEL)
    y_sample = y_all[T_PROMPT:].reshape(DEC_BATCH, DEC_SEQ, D_MODEL)
    ckv_p = ckv[:T_PROMPT].reshape(1, BATCH, SEQ, KV_LORA)
    ckv_s = ckv[T_PROMPT:].reshape(1, DEC_BATCH, DEC_SEQ, KV_LORA)
    kr_p = krope[:T_PROMPT].reshape(1, BATCH, SEQ, QK_ROPE)
    kr_s = krope[T_PROMPT:].reshape(1, DEC_BATCH, DEC_SEQ, QK_ROPE)
    xbc_raw = big[:, BIG_XBC:BIG_XBC + SSD_CONV_DIM]
    conv_p = xbc_raw[:T_PROMPT].reshape(BATCH, SEQ, SSD_CONV_DIM)[:, SEQ - (SSD_CONV - 1):][None]
    conv_s = xbc_raw[T_PROMPT:].reshape(DEC_BATCH, DEC_SEQ, SSD_CONV_DIM)[:, DEC_SEQ - (SSD_CONV - 1):][None]
    ssm_p = ssm_p.reshape(1, BATCH, SSD_HEADS, SSD_HEADDIM, SSD_STATE)
    ssm_s = ssm_s.reshape(1, DEC_BATCH, SSD_HEADS, SSD_HEADDIM, SSD_STATE)
    return (y_prompt, y_sample, ckv_p, kr_p, conv_p, ssm_p, ckv_s, kr_s, conv_s, ssm_s)
```

```python
import functools
import math

import numpy as np
import jax
import jax.numpy as jnp
from jax import lax
from jax.experimental import pallas as pl
from jax.experimental.pallas import tpu as pltpu

F32 = jnp.float32
BF16 = jnp.bfloat16
HI = lax.Precision.HIGHEST

D_MODEL = 4096
BATCH = 2
SEQ = 8192
DEC_BATCH = 8
DEC_SEQ = 32
PAST_LEN = 4096
CHUNK = 64
EPS = 1e-6
MLA_HEADS = 32
Q_LORA = 1024
KV_LORA = 512
QK_NOPE = 128
QK_ROPE = 64
V_HEAD = 128
ROPE_THETA = 10000.0
SCALE = (QK_NOPE + QK_ROPE) ** -0.5
QSCALE = SCALE * math.log2(math.e)
SSD_INNER = 2 * D_MODEL
SSD_HEADDIM = 64
SSD_HEADS = SSD_INNER // SSD_HEADDIM
SSD_STATE = 128
SSD_GROUPS = 8
SSD_RANK = SSD_HEADS // SSD_GROUPS
SSD_GCOLS = SSD_RANK * SSD_HEADDIM
SSD_CONV = 4
SSD_CONV_DIM = SSD_INNER + 2 * SSD_GROUPS * SSD_STATE
N_GROUPS = 8
EXPERTS_PER_GROUP = 8
N_EXPERTS = N_GROUPS * EXPERTS_PER_GROUP
TOP_K = 2
D_EXPERT = 1024

T_PROMPT = BATCH * SEQ
T_SAMPLE = DEC_BATCH * DEC_SEQ
T_ALL = T_PROMPT + T_SAMPLE

OFF_CQ = 0
OFF_CKV = OFF_CQ + Q_LORA
OFF_KR = OFF_CKV + KV_LORA
OFF_Z = OFF_KR + QK_ROPE
OFF_XBC = OFF_Z + SSD_INNER
OFF_DT = OFF_XBC + SSD_CONV_DIM
OFF_GMLA = OFF_DT + SSD_HEADS
OFF_GSSD = OFF_GMLA + D_MODEL
N_IN = OFF_GSSD + D_MODEL

BIG_Z = 0
BIG_XBC = SSD_INNER
BIG_GMLA = BIG_XBC + SSD_CONV_DIM
BIG_GSSD = BIG_GMLA + D_MODEL
BIG_N = BIG_GSSD + D_MODEL
SM_CQ = 0
SM_CKV = Q_LORA
SM_DT = SM_CKV + KV_LORA
SM_KR = SM_DT + SSD_HEADS
SM_KRS = SM_KR + QK_ROPE
SM_N = SM_KRS + QK_ROPE

V7X_VMEM_LIMIT = 56 * 1024 * 1024
ROW_TILE = 256
MM_TM = 1280
MOE_TM = 256
MOE_TF = 256
MOE_TN = 1024
ATTN_TQ = 2048
ATTN_TK = 512


def _cparams(sem):
    return pltpu.CompilerParams(dimension_semantics=sem, vmem_limit_bytes=V7X_VMEM_LIMIT)


def _dot_nt(a, b):
    return lax.dot_general(a, b, (((1,), (1,)), ((), ())), preferred_element_type=F32)


def _rms(x, w):
    return x * lax.rsqrt(jnp.mean(x * x, axis=-1, keepdims=True) + EPS) * w


def _rms_rows_body(x_ref, w_ref, o_ref):
    o_ref[...] = _rms(x_ref[...], w_ref[...]).astype(o_ref.dtype)


def _rms_rows(x, w, out_dtype):
    m, d = x.shape
    return pl.pallas_call(
        _rms_rows_body,
        grid=(m // ROW_TILE,),
        in_specs=[pl.BlockSpec((ROW_TILE, d), lambda i: (i, 0)),
                  pl.BlockSpec((1, d), lambda i: (0, 0))],
        out_specs=pl.BlockSpec((ROW_TILE, d), lambda i: (i, 0)),
        out_shape=jax.ShapeDtypeStruct((m, d), out_dtype),
        compiler_params=_cparams(("parallel",)),
        name="rms_rows",
    )(x, w.reshape(1, d))


def _final_body(h_ref, y_ref, w_ref, o_ref):
    o_ref[...] = _rms(h_ref[...] + y_ref[...], w_ref[...])


def _final_norm(h, y_moe, w):
    m, d = h.shape
    return pl.pallas_call(
        _final_body,
        grid=(m // ROW_TILE,),
        in_specs=[pl.BlockSpec((ROW_TILE, d), lambda i: (i, 0)),
                  pl.BlockSpec((ROW_TILE, d), lambda i: (i, 0)),
                  pl.BlockSpec((1, d), lambda i: (0, 0))],
        out_specs=pl.BlockSpec((ROW_TILE, d), lambda i: (i, 0)),
        out_shape=jax.ShapeDtypeStruct((m, d), F32),
        compiler_params=_cparams(("parallel",)),
        name="final_norm",
    )(h, y_moe, w.reshape(1, d))


def _mm_body(epilogue, a_ref, w_ref, *rest):
    o_ref = rest[-1]
    acc = jnp.dot(a_ref[...], w_ref[...], preferred_element_type=F32)
    if epilogue is not None:
        acc = epilogue(acc, *[r[...] for r in rest[:-1]])
    o_ref[...] = acc.astype(o_ref.dtype)


def _mm(a, w, *, tm, tn, out_dtype, extras=(), epilogue=None, name="mm"):
    m, k = a.shape
    n = w.shape[1]
    in_specs = [pl.BlockSpec((tm, k), lambda i, j: (i, 0)),
                pl.BlockSpec((k, tn), lambda i, j: (0, j))]
    args = [a, w]
    for arr, off in extras:
        in_specs.append(pl.BlockSpec((tm, tn), functools.partial(lambda i, j, off: (i, j + off), off=off)))
        args.append(arr)
    return pl.pallas_call(
        functools.partial(_mm_body, epilogue),
        grid=(m // tm, n // tn),
        in_specs=in_specs,
        out_specs=pl.BlockSpec((tm, tn), lambda i, j: (i, j)),
        out_shape=jax.ShapeDtypeStruct((m, n), out_dtype),
        compiler_params=_cparams(("parallel", "parallel")),
        name=name,
    )(*args)


def _mla_prep_body(s_ref, qw_ref, kvw_ref, cos_ref, sin_ref, cqn_ref, ckv_ref, ckvb_ref, kr_ref, krb_ref):
    cqn_ref[...] = _rms(s_ref[:, SM_CQ:SM_CQ + Q_LORA], qw_ref[...]).astype(BF16)
    c = _rms(s_ref[:, SM_CKV:SM_CKV + KV_LORA], kvw_ref[...])
    ckv_ref[...] = c
    ckvb_ref[...] = c.astype(BF16)
    kr = (s_ref[:, SM_KR:SM_KR + QK_ROPE] * cos_ref[...]
          + s_ref[:, SM_KRS:SM_KRS + QK_ROPE] * sin_ref[...])
    kr_ref[...] = kr
    krb_ref[...] = kr.astype(BF16)


def _mla_prep(small, q_norm_w, kv_norm_w, cos2, sin2):
    m = small.shape[0]
    row = lambda n: pl.BlockSpec((ROW_TILE, n), lambda i: (i, 0))
    vec = lambda n: pl.BlockSpec((1, n), lambda i: (0, 0))
    return pl.pallas_call(
        _mla_prep_body,
        grid=(m // ROW_TILE,),
        in_specs=[row(SM_N), vec(Q_LORA), vec(KV_LORA), row(QK_ROPE), row(QK_ROPE)],
        out_specs=[row(Q_LORA), row(KV_LORA), row(KV_LORA), row(QK_ROPE), row(QK_ROPE)],
        out_shape=[jax.ShapeDtypeStruct((m, Q_LORA), BF16),
                   jax.ShapeDtypeStruct((m, KV_LORA), F32),
                   jax.ShapeDtypeStruct((m, KV_LORA), BF16),
                   jax.ShapeDtypeStruct((m, QK_ROPE), F32),
                   jax.ShapeDtypeStruct((m, QK_ROPE), BF16)],
        compiler_params=_cparams(("parallel",)),
        name="mla_prep",
    )(small, q_norm_w.reshape(1, -1), kv_norm_w.reshape(1, -1), cos2, sin2)


def _q_rope_body(a_ref, w_ref, ws_ref, cos_ref, sin_ref, o_ref):
    reps = o_ref.shape[1] // cos_ref.shape[1]
    a = a_ref[...]
    r = jnp.dot(a, w_ref[...], preferred_element_type=F32)
    rs = jnp.dot(a, ws_ref[...], preferred_element_type=F32)
    cos = jnp.tile(cos_ref[...], (1, reps))
    sin = jnp.tile(sin_ref[...], (1, reps))
    o_ref[...] = ((r * cos + rs * sin) * QSCALE).astype(o_ref.dtype)


def _q_rope(cqn, w_r, w_rs, cos128, sin128, *, tm, tn):
    m, k = cqn.shape
    n = w_r.shape[1]
    return pl.pallas_call(
        _q_rope_body,
        grid=(m // tm, n // tn),
        in_specs=[pl.BlockSpec((tm, k), lambda i, j: (i, 0)),
                  pl.BlockSpec((k, tn), lambda i, j: (0, j)),
                  pl.BlockSpec((k, tn), lambda i, j: (0, j)),
                  pl.BlockSpec((tm, 128), lambda i, j: (i, 0)),
                  pl.BlockSpec((tm, 128), lambda i, j: (i, 0))],
        out_specs=pl.BlockSpec((tm, tn), lambda i, j: (i, j)),
        out_shape=jax.ShapeDtypeStruct((m, n), BF16),
        compiler_params=_cparams(("parallel", "parallel")),
        name="q_rope",
    )(cqn, w_r, w_rs, cos128, sin128)


def _mm_t_body(w_ref, a_ref, o_ref):
    o_ref[0] = _dot_nt(w_ref[...], a_ref[...]).astype(o_ref.dtype)


def _mm_t(w_t, a, *, rows, tm, tn, out_dtype, name):
    n, k = w_t.shape
    return pl.pallas_call(
        _mm_t_body,
        grid=(rows // tm, n // tn),
        in_specs=[pl.BlockSpec((tn, k), lambda i, j: (j, 0)),
                  pl.BlockSpec((tm, k), lambda i, j: (i, 0))],
        out_specs=pl.BlockSpec((1, tn, tm), lambda i, j: (i, j, 0)),
        out_shape=jax.ShapeDtypeStruct((rows // tm, n, tm), out_dtype),
        compiler_params=_cparams(("parallel", "parallel")),
        name=name,
    )(w_t, a)


ATTN_KP = 256


ATTN_VP = 144


def _attn_prompt_body(qn_ref, qr_ref, kn_ref, kr_ref, vt_ref, o_ref, kf_sc, vf_sc, qf_sc, acc_sc):
    qi = pl.program_id(2)
    tq = qn_ref.shape[0]
    tk = vt_ref.shape[2]
    seq = kn_ref.shape[0]
    pad = ATTN_KP - QK_NOPE - QK_ROPE

    @pl.when(qi == 0)
    def _():
        def fill(i, c):
            r = pl.ds(pl.multiple_of(i * tk, tk), tk)
            kr = kr_ref[r, :]
            for hh in range(2):
                c0 = hh * ATTN_KP
                kf_sc[r, c0:c0 + QK_NOPE] = kn_ref[r, hh * QK_NOPE:(hh + 1) * QK_NOPE]
                kf_sc[r, c0 + QK_NOPE:c0 + QK_NOPE + QK_ROPE] = kr
                kf_sc[r, c0 + QK_NOPE + QK_ROPE:c0 + ATTN_KP] = jnp.zeros((tk, pad), BF16)
                vf_sc[i, hh * ATTN_VP:hh * ATTN_VP + V_HEAD, :] = vt_ref[i, hh * V_HEAD:(hh + 1) * V_HEAD, :]
                vf_sc[i, hh * ATTN_VP + V_HEAD:(hh + 1) * ATTN_VP, :] = jnp.ones((ATTN_VP - V_HEAD, tk), BF16)
            return c
        lax.fori_loop(0, seq // tk, fill, 0)

    for hh in range(2):
        c0 = hh * ATTN_KP
        qf_sc[:, c0:c0 + QK_NOPE] = qn_ref[:, hh * QK_NOPE:(hh + 1) * QK_NOPE]
        qf_sc[:, c0 + QK_NOPE:c0 + QK_NOPE + QK_ROPE] = qr_ref[:, hh * QK_ROPE:(hh + 1) * QK_ROPE]
        qf_sc[:, c0 + QK_NOPE + QK_ROPE:c0 + ATTN_KP] = jnp.zeros((tq, pad), BF16)
    acc_sc[...] = jnp.zeros(acc_sc.shape, F32)

    def block(kb, carry, diag):
        r = pl.ds(pl.multiple_of(kb * tk, tk), tk)
        q0 = 0 if diag is None else diag * tk
        nq_cols = tq - q0
        out = []
        for hh in range(2):
            m = carry[hh][:, q0:]
            st = _dot_nt(kf_sc[r, hh * ATTN_KP:(hh + 1) * ATTN_KP], qf_sc[q0:, hh * ATTN_KP:(hh + 1) * ATTN_KP])
            if diag is not None:
                kc = lax.broadcasted_iota(jnp.int32, (tk, nq_cols), 0) // CHUNK
                qc = lax.broadcasted_iota(jnp.int32, (tk, nq_cols), 1) // CHUNK
                st = jnp.where(kc <= qc, st, -jnp.inf)
            m_new = jnp.maximum(m, jnp.max(st, axis=0, keepdims=True))
            alpha = jnp.exp2(m - m_new)
            pt = jnp.exp2(st - m_new).astype(BF16)
            vt = vf_sc[kb, hh * ATTN_VP:(hh + 1) * ATTN_VP, :]
            acc_sc[hh, :, q0:] = alpha * acc_sc[hh, :, q0:] + jnp.dot(vt, pt, preferred_element_type=F32)
            out.append(m_new if q0 == 0 else jnp.concatenate([carry[hh][:, :q0], m_new], axis=1))
        return tuple(out)

    n_diag = tq // tk
    n_full = qi * n_diag
    init = (jnp.full((1, tq), -jnp.inf, F32),) * 2
    carry = lax.fori_loop(0, n_full, lambda kb, c: block(kb, c, None), init)
    for d in range(n_diag):
        carry = block(n_full + d, carry, d)
    for hh in range(2):
        o = acc_sc[hh, 0:V_HEAD, :] / acc_sc[hh, V_HEAD:V_HEAD + 1, :]
        o_ref[:, hh * V_HEAD:(hh + 1) * V_HEAD] = o.T.astype(o_ref.dtype)


def _attn_prompt(qn, qr, kn, krb, vt, *, nb, seq, heads, tq):
    tk = vt.shape[2]
    nq = seq // tq
    nk = seq // tk
    return pl.pallas_call(
        _attn_prompt_body,
        grid=(nb, heads // 2, nq),
        in_specs=[pl.BlockSpec((tq, 2 * QK_NOPE), lambda b, hp, qi: (b * nq + qi, hp)),
                  pl.BlockSpec((tq, 2 * QK_ROPE), lambda b, hp, qi: (b * nq + qi, hp)),
                  pl.BlockSpec((seq, 2 * QK_NOPE), lambda b, hp, qi: (b, hp)),
                  pl.BlockSpec((seq, QK_ROPE), lambda b, hp, qi: (b, 0)),
                  pl.BlockSpec((nk, 2 * V_HEAD, tk), lambda b, hp, qi: (b, hp, 0))],
        out_specs=pl.BlockSpec((tq, 2 * V_HEAD), lambda b, hp, qi: (b * nq + qi, hp)),
        out_shape=jax.ShapeDtypeStruct((nb * seq, heads * V_HEAD), BF16),
        scratch_shapes=[pltpu.VMEM((seq, 2 * ATTN_KP), BF16), pltpu.VMEM((nk, 2 * ATTN_VP, tk), BF16),
                        pltpu.VMEM((tq, 2 * ATTN_KP), BF16), pltpu.VMEM((2, ATTN_VP, tq), F32)],
        compiler_params=_cparams(("parallel", "parallel", "arbitrary")),
        name="attn_prompt",
    )(qn, qr, kn, krb, vt)


def _bmm_body(a_ref, w_ref, o_ref):
    o_ref[0] = jnp.dot(a_ref[0], w_ref[0], preferred_element_type=F32).astype(o_ref.dtype)


def _bmm(a, w, out_dtype, name):
    h, m, k = a.shape
    n = w.shape[2]
    return pl.pallas_call(
        _bmm_body,
        grid=(h,),
        in_specs=[pl.BlockSpec((1, m, k), lambda i: (i, 0, 0)),
                  pl.BlockSpec((1, k, n), lambda i: (i, 0, 0))],
        out_specs=pl.BlockSpec((1, m, n), lambda i: (i, 0, 0)),
        out_shape=jax.ShapeDtypeStruct((h, m, n), out_dtype),
        compiler_params=_cparams(("parallel",)),
        name=name,
    )(a, w)


def _attn_sample_body(ql_ref, qr_ref, cc_ref, ck_ref, nc_ref, nk_ref, o_ref, ccb_sc, ckb_sc):
    @pl.when(pl.program_id(1) == 0)
    def _():
        ccb_sc[...] = cc_ref[0].astype(BF16)
        ckb_sc[...] = ck_ref[0].astype(BF16)

    ql = ql_ref[0]
    qr = qr_ref[0]
    ncb = nc_ref[...].astype(BF16)
    nkb = nk_ref[...].astype(BF16)
    s1 = _dot_nt(ql, ccb_sc[...]) + _dot_nt(qr, ckb_sc[...])
    s2 = _dot_nt(ql, ncb) + _dot_nt(qr, nkb)
    m = jnp.maximum(jnp.max(s1, axis=1, keepdims=True), jnp.max(s2, axis=1, keepdims=True))
    p1 = jnp.exp2(s1 - m)
    p2 = jnp.exp2(s2 - m)
    l = jnp.sum(p1, axis=1, keepdims=True) + jnp.sum(p2, axis=1, keepdims=True)
    o = (jnp.dot(p1.astype(BF16), ccb_sc[...], preferred_element_type=F32)
         + jnp.dot(p2.astype(BF16), ncb, preferred_element_type=F32))
    o_ref[0] = (o / l).astype(o_ref.dtype)


def _attn_sample(q_lat, q_rope, cache_ckv, cache_krope, ckv, krope):
    rows = q_lat.shape[1]
    tr = 256
    new_blk = T_PROMPT // DEC_SEQ
    return pl.pallas_call(
        _attn_sample_body,
        grid=(DEC_BATCH, rows // tr),
        in_specs=[pl.BlockSpec((1, tr, KV_LORA), lambda b, r: (b, r, 0)),
                  pl.BlockSpec((1, tr, QK_ROPE), lambda b, r: (b, r, 0)),
                  pl.BlockSpec((1, PAST_LEN, KV_LORA), lambda b, r: (b, 0, 0)),
                  pl.BlockSpec((1, PAST_LEN, QK_ROPE), lambda b, r: (b, 0, 0)),
                  pl.BlockSpec((DEC_SEQ, KV_LORA), lambda b, r: (new_blk + b, 0)),
                  pl.BlockSpec((DEC_SEQ, QK_ROPE), lambda b, r: (new_blk + b, 0))],
        out_specs=pl.BlockSpec((1, tr, KV_LORA), lambda b, r: (b, r, 0)),
        out_shape=jax.ShapeDtypeStruct((DEC_BATCH, rows, KV_LORA), BF16),
        scratch_shapes=[pltpu.VMEM((PAST_LEN, KV_LORA), BF16), pltpu.VMEM((PAST_LEN, QK_ROPE), BF16)],
        compiler_params=_cparams(("parallel", "arbitrary")),
        name="attn_sample",
    )(q_lat, q_rope, cache_ckv, cache_krope, ckv, krope)


def _conv_body(x_ref, prev_ref, w_ref, b_ref, o_ref, ext_sc):
    tl = x_ref.shape[0]

    @pl.when(pl.program_id(2) == 0)
    def _():
        ext_sc[0:8, :] = prev_ref[0]

    @pl.when(pl.program_id(2) != 0)
    def _():
        ext_sc[0:8, :] = ext_sc[tl:tl + 8, :]

    ext_sc[8:8 + tl, :] = x_ref[...]
    acc = b_ref[...] + ext_sc[5:5 + tl, :] * w_ref[0:1, :]
    for k in range(1, SSD_CONV):
        acc = acc + ext_sc[5 + k:5 + k + tl, :] * w_ref[k:k + 1, :]
    o_ref[...] = acc * jax.nn.sigmoid(acc)


def _conv_silu(big, prev8, conv_w, conv_b, *, nb, seq, tl, row_off):
    tc = 1024
    nrt = seq // tl
    rb0 = row_off // tl
    cb0 = BIG_XBC // tc
    return pl.pallas_call(
        _conv_body,
        grid=(nb, SSD_CONV_DIM // tc, nrt),
        in_specs=[pl.BlockSpec((tl, tc), lambda b, j, r: (rb0 + b * nrt + r, cb0 + j)),
                  pl.BlockSpec((1, 8, tc), lambda b, j, r: (b, 0, j)),
                  pl.BlockSpec((SSD_CONV, tc), lambda b, j, r: (0, j)),
                  pl.BlockSpec((1, tc), lambda b, j, r: (0, j))],
        out_specs=pl.BlockSpec((tl, tc), lambda b, j, r: (b * nrt + r, j)),
        out_shape=jax.ShapeDtypeStruct((nb * seq, SSD_CONV_DIM), F32),
        scratch_shapes=[pltpu.VMEM((tl + 8, tc), F32)],
        compiler_params=_cparams(("parallel", "parallel", "arbitrary")),
        name="conv_silu",
    )(big, prev8, conv_w, conv_b.reshape(1, -1))


def _softplus(x):
    return jnp.maximum(x, 0.0) + jnp.log(1.0 + jnp.exp(-jnp.abs(x)))


def _dot_sel(sel, a, *, sel_left):
    hi = a.astype(BF16)
    r1 = a - hi.astype(F32)
    mid = r1.astype(BF16)
    lo = (r1 - mid.astype(F32)).astype(BF16)
    out = None
    for term in (hi, mid, lo):
        d = (jnp.dot(sel, term, preferred_element_type=F32) if sel_left
             else jnp.dot(term, sel, preferred_element_type=F32))
        out = d if out is None else out + d
    return out


def _ssd_body(has_h0, x_ref, b_ref, c_ref, z_ref, dt_ref, dtt_ref, bias_ref, biast_ref, al_ref, alt_ref,
              dsk_ref, nw_ref, *rest):
    if has_h0:
        h0_ref, y_ref, hout_ref, ht_sc, yd_sc = rest
    else:
        y_ref, hout_ref, ht_sc, yd_sc = rest
    lc = x_ref.shape[0]
    R, P = SSD_RANK, SSD_HEADDIM
    c_idx = pl.program_id(2)

    @pl.when(c_idx == 0)
    def _():
        if has_h0:
            ht_sc[...] = h0_ref[0].T
        else:
            ht_sc[...] = jnp.zeros(ht_sc.shape, F32)

    dt = _softplus(dt_ref[0] + bias_ref[0])
    dtt = _softplus(dtt_ref[0, 0] + biast_ref[0])
    da = dt * (-jnp.exp(al_ref[0]))
    dat = dtt * (-jnp.exp(alt_ref[0]))
    ri = lax.broadcasted_iota(jnp.int32, (lc, lc), 0)
    ci = lax.broadcasted_iota(jnp.int32, (lc, lc), 1)
    tril = ri >= ci
    a_cs = _dot_sel(jnp.where(tril, 1.0, 0.0).astype(BF16), da, sel_left=True)
    a_cst = _dot_sel(jnp.where(ri <= ci, 1.0, 0.0).astype(BF16), dat, sel_left=False)
    er = lax.broadcasted_iota(jnp.int32, (R, R * P), 0)
    ec = lax.broadcasted_iota(jnp.int32, (R, R * P), 1) // P
    expand = jnp.where(er == ec, 1.0, 0.0).astype(BF16)
    a_exp = _dot_sel(expand, a_cs, sel_left=False)
    dt_exp = _dot_sel(expand, dt, sel_left=False)
    a_last = a_exp[lc - 1:lc, :]

    x = x_ref[...]
    xdt = x * dt_exp
    bm = b_ref[...].astype(BF16)
    cm = c_ref[...].astype(BF16)
    ht = ht_sc[...]
    cb = _dot_nt(cm, bm)
    y_off = jnp.dot(cm, ht.astype(BF16), preferred_element_type=F32) * jnp.exp(a_exp)
    xdt_b = xdt.astype(BF16)
    for r in range(R):
        seg = a_cs[:, r:r + 1] - a_cst[r:r + 1, :]
        decay = jnp.exp(jnp.where(tril, seg, -jnp.inf))
        mr = (cb * decay).astype(BF16)
        yd_sc[:, r * P:(r + 1) * P] = jnp.dot(mr, xdt_b[:, r * P:(r + 1) * P], preferred_element_type=F32)
    wgt = (xdt * jnp.exp(a_last - a_exp)).astype(BF16)
    states_t = lax.dot_general(bm, wgt, (((0,), (0,)), ((), ())), preferred_element_type=F32)
    ht_new = ht * jnp.exp(a_last) + states_t
    ht_sc[...] = ht_new

    @pl.when(c_idx == pl.num_programs(2) - 1)
    def _():
        hout_ref[0] = ht_new.T

    y = yd_sc[...] + y_off + x * dsk_ref[...]
    z = z_ref[...]
    y = y * (z * jax.nn.sigmoid(z))
    y = y * lax.rsqrt(jnp.mean(y * y, axis=-1, keepdims=True) + EPS) * nw_ref[...]
    y_ref[...] = y.astype(y_ref.dtype)


def _ssd(xbc, big, dt_g, dtt_g, h0, dt_bias, a_log, d_skip, ssd_norm_w, *, nb, seq, lc, z_row_off):
    nc = seq // lc
    G, R, GC, N = SSD_GROUPS, SSD_RANK, SSD_GCOLS, SSD_STATE
    zb0 = z_row_off // lc
    row = lambda b, g, c: b * nc + c
    in_specs = [
        pl.BlockSpec((lc, GC), lambda b, g, c: (row(b, g, c), g)),
        pl.BlockSpec((lc, N), lambda b, g, c: (row(b, g, c), SSD_INNER // N + g)),
        pl.BlockSpec((lc, N), lambda b, g, c: (row(b, g, c), SSD_INNER // N + G + g)),
        pl.BlockSpec((lc, GC), lambda b, g, c: (zb0 + row(b, g, c), BIG_Z // GC + g)),
        pl.BlockSpec((1, lc, R), lambda b, g, c: (g, row(b, g, c), 0)),
        pl.BlockSpec((1, 1, R, lc), lambda b, g, c: (g, row(b, g, c), 0, 0)),
        pl.BlockSpec((1, 1, R), lambda b, g, c: (g, 0, 0)),
        pl.BlockSpec((1, R, 1), lambda b, g, c: (g, 0, 0)),
        pl.BlockSpec((1, 1, R), lambda b, g, c: (g, 0, 0)),
        pl.BlockSpec((1, R, 1), lambda b, g, c: (g, 0, 0)),
        pl.BlockSpec((1, GC), lambda b, g, c: (0, g)),
        pl.BlockSpec((1, GC), lambda b, g, c: (0, g)),
    ]
    args = [xbc, xbc, xbc, big, dt_g, dtt_g,
            dt_bias.reshape(G, 1, R), dt_bias.reshape(G, R, 1),
            a_log.reshape(G, 1, R), a_log.reshape(G, R, 1),
            jnp.repeat(d_skip, SSD_HEADDIM).reshape(1, SSD_INNER), ssd_norm_w.reshape(1, SSD_INNER)]
    if h0 is not None:
        in_specs.append(pl.BlockSpec((1, GC, N), lambda b, g, c: (b, g, 0)))
        args.append(h0)
    return pl.pallas_call(
        functools.partial(_ssd_body, h0 is not None),
        grid=(nb, G, nc),
        in_specs=in_specs,
        out_specs=[pl.BlockSpec((lc, GC), lambda b, g, c: (row(b, g, c), g)),
                   pl.BlockSpec((1, GC, N), lambda b, g, c: (b, g, 0))],
        out_shape=[jax.ShapeDtypeStruct((nb * seq, SSD_INNER), BF16),
                   jax.ShapeDtypeStruct((nb, SSD_HEADS * SSD_HEADDIM, N), F32)],
        scratch_shapes=[pltpu.VMEM((N, GC), F32), pltpu.VMEM((lc, GC), F32)],
        compiler_params=_cparams(("parallel", "parallel", "arbitrary")),
        name="ssd",
    )(*args)


def _norm_router_body(h_ref, w_ref, wr_ref, br_ref, xn_ref, lg_ref):
    xn = _rms(h_ref[...], w_ref[...])
    xn_ref[...] = xn.astype(BF16)
    lg_ref[...] = jnp.dot(xn, wr_ref[...], precision=HI, preferred_element_type=F32) + br_ref[...]


def _norm_router(h, norm_w, w_router, b_router):
    m, d = h.shape
    n = w_router.shape[1]
    return pl.pallas_call(
        _norm_router_body,
        grid=(m // ROW_TILE,),
        in_specs=[pl.BlockSpec((ROW_TILE, d), lambda i: (i, 0)),
                  pl.BlockSpec((1, d), lambda i: (0, 0)),
                  pl.BlockSpec((d, n), lambda i: (0, 0)),
                  pl.BlockSpec((1, n), lambda i: (0, 0))],
        out_specs=[pl.BlockSpec((ROW_TILE, d), lambda i: (i, 0)),
                   pl.BlockSpec((ROW_TILE, n), lambda i: (i, 0))],
        out_shape=[jax.ShapeDtypeStruct((m, d), BF16), jax.ShapeDtypeStruct((m, n), F32)],
        compiler_params=_cparams(("parallel",)),
        name="norm_router",
    )(h, norm_w.reshape(1, d), w_router, b_router)


def _moe_up_body(be_ref, first_ref, nused_ref, x_ref, wg_ref, wu_ref, o_ref, wgb_sc, wub_sc):
    blk = pl.program_id(1)

    @pl.when(first_ref[blk] == 1)
    def _():
        wgb_sc[...] = wg_ref[0].astype(BF16)
        wub_sc[...] = wu_ref[0].astype(BF16)

    @pl.when(blk < nused_ref[0])
    def _():
        x = x_ref[...]
        g = jnp.dot(x, wgb_sc[...], preferred_element_type=F32)
        u = jnp.dot(x, wub_sc[...], preferred_element_type=F32)
        o_ref[...] = (g * jax.nn.sigmoid(g) * u).astype(o_ref.dtype)

    @pl.when(blk >= nused_ref[0])
    def _():
        o_ref[...] = jnp.zeros(o_ref.shape, o_ref.dtype)


def _moe_up(be, first, nused, xs, w_gate, w_up):
    p, d = xs.shape
    nblk = p // MOE_TM
    grid_spec = pltpu.PrefetchScalarGridSpec(
        num_scalar_prefetch=3,
        grid=(D_EXPERT // MOE_TF, nblk),
        in_specs=[pl.BlockSpec((MOE_TM, d), lambda f, i, be, fi, nu: (i, 0)),
                  pl.BlockSpec((1, d, MOE_TF), lambda f, i, be, fi, nu: (be[i], 0, f)),
                  pl.BlockSpec((1, d, MOE_TF), lambda f, i, be, fi, nu: (be[i], 0, f))],
        out_specs=pl.BlockSpec((MOE_TM, MOE_TF), lambda f, i, be, fi, nu: (i, f)),
        scratch_shapes=[pltpu.VMEM((d, MOE_TF), BF16), pltpu.VMEM((d, MOE_TF), BF16)],
    )
    return pl.pallas_call(
        _moe_up_body,
        grid_spec=grid_spec,
        out_shape=jax.ShapeDtypeStruct((p, D_EXPERT), BF16),
        compiler_params=_cparams(("arbitrary", "arbitrary")),
        name="moe_up",
    )(be, first, nused, xs, w_gate, w_up)


def _moe_down_body(be_ref, first_ref, nused_ref, h_ref, wd_ref, rw_ref, o_ref, wdb_sc):
    blk = pl.program_id(1)

    @pl.when(first_ref[blk] == 1)
    def _():
        wdb_sc[...] = wd_ref[0].astype(BF16)

    @pl.when(blk < nused_ref[0])
    def _():
        o_ref[...] = jnp.dot(h_ref[...], wdb_sc[...], preferred_element_type=F32) * rw_ref[...]

    @pl.when(blk >= nused_ref[0])
    def _():
        o_ref[...] = jnp.zeros(o_ref.shape, o_ref.dtype)


def _moe_down(be, first, nused, hact, w_down, row_w):
    p, f = hact.shape
    d = w_down.shape[2]
    nblk = p // MOE_TM
    grid_spec = pltpu.PrefetchScalarGridSpec(
        num_scalar_prefetch=3,
        grid=(d // MOE_TN, nblk),
        in_specs=[pl.BlockSpec((MOE_TM, f), lambda n, i, be, fi, nu: (i, 0)),
                  pl.BlockSpec((1, f, MOE_TN), lambda n, i, be, fi, nu: (be[i], 0, n)),
                  pl.BlockSpec((MOE_TM, 1), lambda n, i, be, fi, nu: (i, 0))],
        out_specs=pl.BlockSpec((MOE_TM, MOE_TN), lambda n, i, be, fi, nu: (i, n)),
        scratch_shapes=[pltpu.VMEM((f, MOE_TN), BF16)],
    )
    return pl.pallas_call(
        _moe_down_body,
        grid_spec=grid_spec,
        out_shape=jax.ShapeDtypeStruct((p, d), F32),
        compiler_params=_cparams(("arbitrary", "arbitrary")),
        name="moe_down",
    )(be, first, nused, hact, w_down, row_w)


def _route(logits):
    t = logits.shape[0]
    g_logits = logits[:, :N_GROUPS]
    g_sel = jnp.argmax(g_logits, axis=-1)
    g_w = jnp.max(jax.nn.softmax(g_logits, axis=-1), axis=-1)
    e_logits = logits[:, N_GROUPS:N_GROUPS + N_EXPERTS].reshape(t, N_GROUPS, EXPERTS_PER_GROUP)
    e_in = jnp.take_along_axis(e_logits, g_sel[:, None, None], axis=1)[:, 0]
    e_val, e_idx = lax.top_k(e_in, TOP_K)
    e_w = jax.nn.softmax(e_val, axis=-1) * g_w[:, None]
    expert_id = (g_sel[:, None] * EXPERTS_PER_GROUP + e_idx).reshape(-1).astype(jnp.int32)
    a = t * TOP_K
    token_id = jnp.repeat(jnp.arange(t, dtype=jnp.int32), TOP_K)
    order = jnp.argsort(expert_id)
    e_sorted = expert_id[order]
    counts = jnp.zeros((N_EXPERTS,), jnp.int32).at[expert_id].add(1)
    padded = (counts + MOE_TM - 1) // MOE_TM * MOE_TM
    pad_end = jnp.cumsum(padded)
    pad_start = pad_end - padded
    start = jnp.cumsum(counts) - counts
    dest = pad_start[e_sorted] + jnp.arange(a, dtype=jnp.int32) - start[e_sorted]
    nblk = a // MOE_TM + N_EXPERTS
    rows = nblk * MOE_TM
    row_token = jnp.zeros((rows,), jnp.int32).at[dest].set(token_id[order])
    row_w = jnp.zeros((rows,), F32).at[dest].set(e_w.reshape(-1)[order])
    pos = jnp.zeros((a,), jnp.int32).at[order].set(dest).reshape(t, TOP_K)
    nused = (pad_end[-1] // MOE_TM).astype(jnp.int32)
    blk = jnp.arange(nblk, dtype=jnp.int32)
    be = jnp.minimum(jnp.searchsorted(pad_end, blk * MOE_TM, side='right'), N_EXPERTS - 1).astype(jnp.int32)
    be = jnp.where(blk < nused, be, be[jnp.maximum(nused - 1, 0)])
    first = jnp.concatenate([jnp.ones((1,), jnp.int32), (be[1:] != be[:-1]).astype(jnp.int32)])
    return row_token, row_w, pos, be, first, nused.reshape(1)


def _rope_tables():
    half = QK_ROPE // 2
    inv_freq = ROPE_THETA ** (-jnp.arange(half, dtype=F32) / half)
    pos = jnp.concatenate([jnp.tile(jnp.arange(SEQ), BATCH),
                           jnp.tile(PAST_LEN + jnp.arange(DEC_SEQ), DEC_BATCH)]).astype(F32)
    ang = pos[:, None] * inv_freq[None, :]
    cos, sin = jnp.cos(ang), jnp.sin(ang)
    cos2 = jnp.concatenate([cos, cos], axis=1)
    sin2 = jnp.concatenate([-sin, sin], axis=1)
    return cos2, sin2


def kernel(x_prompt, x_sample, cache_ckv, cache_krope, state_conv, state_ssm, norm1_w, w_in, q_norm_w, kv_norm_w, w_uq, w_ukv, conv_w, conv_b, dt_bias, a_log, d_skip, ssd_norm_w, w_mla_o, w_ssd_o, w_out, norm2_w, w_group, b_group, w_erouter, b_erouter, w_gate, w_up, w_down, final_norm_w):
    swap = np.concatenate([np.arange(QK_ROPE // 2, QK_ROPE), np.arange(QK_ROPE // 2)])
    x_all = jnp.concatenate([x_prompt.reshape(T_PROMPT, D_MODEL), x_sample.reshape(T_SAMPLE, D_MODEL)], axis=0)
    cos2, sin2 = _rope_tables()
    cos128, sin128 = jnp.tile(cos2, (1, 2)), jnp.tile(sin2, (1, 2))

    wi = w_in[0]
    w_kr = wi[:, OFF_KR:OFF_KR + QK_ROPE]
    w_small = jnp.concatenate([wi[:, OFF_CQ:OFF_KR], wi[:, OFF_DT:OFF_DT + SSD_HEADS], w_kr, w_kr[:, swap]],
                              axis=1).astype(BF16)
    w_big = jnp.concatenate([wi[:, OFF_Z:OFF_DT], wi[:, OFF_GMLA:]], axis=1).astype(BF16)
    wq = w_uq[0].reshape(Q_LORA, MLA_HEADS, QK_NOPE + QK_ROPE)
    wq_nope = wq[:, :, :QK_NOPE].reshape(Q_LORA, MLA_HEADS * QK_NOPE).astype(BF16)
    wq_rope = wq[:, :, QK_NOPE:]
    wq_r = wq_rope.reshape(Q_LORA, MLA_HEADS * QK_ROPE).astype(BF16)
    wq_rs = wq_rope[:, :, swap].reshape(Q_LORA, MLA_HEADS * QK_ROPE).astype(BF16)
    wkv3 = w_ukv[0].reshape(KV_LORA, MLA_HEADS, QK_NOPE + V_HEAD)
    w_uk_all = wkv3[:, :, :QK_NOPE].reshape(KV_LORA, MLA_HEADS * QK_NOPE).astype(BF16)
    w_uv_all_t = jnp.transpose(wkv3[:, :, QK_NOPE:], (1, 2, 0)).reshape(MLA_HEADS * V_HEAD, KV_LORA).astype(BF16)
    w_uk_t = jnp.transpose(wkv3[:, :, :QK_NOPE], (1, 2, 0)).astype(BF16)
    w_uv_h = jnp.transpose(wkv3[:, :, QK_NOPE:], (1, 0, 2)).astype(BF16)

    u = _rms_rows(x_all, norm1_w[0], BF16)
    small = _mm(u, w_small, tm=MM_TM, tn=SM_N // 2, out_dtype=F32, name="proj_small")
    big = _mm(u, w_big, tm=MM_TM, tn=512, out_dtype=F32, name="proj_big")

    cqn, ckv, ckv_b, krope, krope_b = _mla_prep(small, q_norm_w[0], kv_norm_w[0], cos2, sin2)
    qn = _mm(cqn, wq_nope, tm=MM_TM, tn=1024, out_dtype=BF16, epilogue=lambda acc: acc * QSCALE, name="q_nope")
    qr = _q_rope(cqn, wq_r, wq_rs, cos128, sin128, tm=MM_TM, tn=1024)
    kn = _mm(ckv_b, w_uk_all, tm=MM_TM, tn=1024, out_dtype=BF16, name="k_up")
    vt = _mm_t(w_uv_all_t, ckv_b, rows=T_PROMPT, tm=ATTN_TK, tn=1024, out_dtype=BF16, name="v_up_t")
    o_mla_p = _attn_prompt(qn, qr, kn, krope_b, vt, nb=BATCH, seq=SEQ, heads=MLA_HEADS, tq=ATTN_TQ)

    qn_s = qn[T_PROMPT:].reshape(T_SAMPLE, MLA_HEADS, QK_NOPE).transpose(1, 0, 2)
    q_lat = _bmm(qn_s, w_uk_t, BF16, "q_absorb")
    q_lat = q_lat.reshape(MLA_HEADS, DEC_BATCH, DEC_SEQ, KV_LORA).transpose(1, 0, 2, 3)
    q_lat = q_lat.reshape(DEC_BATCH, MLA_HEADS * DEC_SEQ, KV_LORA)
    qr_s = qr[T_PROMPT:].reshape(DEC_BATCH, DEC_SEQ, MLA_HEADS, QK_ROPE).transpose(0, 2, 1, 3)
    qr_s = qr_s.reshape(DEC_BATCH, MLA_HEADS * DEC_SEQ, QK_ROPE)
    o_lat = _attn_sample(q_lat, qr_s, cache_ckv[0], cache_krope[0], ckv, krope)
    o_lat = o_lat.reshape(DEC_BATCH, MLA_HEADS, DEC_SEQ, KV_LORA).transpose(1, 0, 2, 3)
    o_lat = o_lat.reshape(MLA_HEADS, T_SAMPLE, KV_LORA)
    o_mla_s = _bmm(o_lat, w_uv_h, BF16, "v_absorb")
    o_mla_s = o_mla_s.transpose(1, 0, 2).reshape(T_SAMPLE, MLA_HEADS * V_HEAD)
    o_mla = jnp.concatenate([o_mla_p, o_mla_s], axis=0)

    prev_p = jnp.zeros((BATCH, 8, SSD_CONV_DIM), F32)
    prev_s = jnp.concatenate([jnp.zeros((DEC_BATCH, 8 - (SSD_CONV - 1), SSD_CONV_DIM), F32), state_conv[0]], axis=1)
    xbc_p = _conv_silu(big, prev_p, conv_w[0], conv_b[0], nb=BATCH, seq=SEQ, tl=512, row_off=0)
    xbc_s = _conv_silu(big, prev_s, conv_w[0], conv_b[0], nb=DEC_BATCH, seq=DEC_SEQ, tl=DEC_SEQ, row_off=T_PROMPT)
    dt_raw = small[:, SM_DT:SM_DT + SSD_HEADS]

    def dt_layouts(d, lc):
        rows = d.shape[0]
        dg = d.reshape(rows, SSD_GROUPS, SSD_RANK).transpose(1, 0, 2)
        dgt = dg.reshape(SSD_GROUPS, rows // lc, lc, SSD_RANK).transpose(0, 1, 3, 2)
        return dg, dgt

    dt_p, dtt_p = dt_layouts(dt_raw[:T_PROMPT], CHUNK)
    dt_s, dtt_s = dt_layouts(dt_raw[T_PROMPT:], DEC_SEQ)
    h0_s = state_ssm[0].reshape(DEC_BATCH, SSD_HEADS * SSD_HEADDIM, SSD_STATE)
    y_p, ssm_p = _ssd(xbc_p, big, dt_p, dtt_p, None, dt_bias[0], a_log[0], d_skip[0], ssd_norm_w[0],
                      nb=BATCH, seq=SEQ, lc=CHUNK, z_row_off=0)
    y_s, ssm_s = _ssd(xbc_s, big, dt_s, dtt_s, h0_s, dt_bias[0], a_log[0], d_skip[0], ssd_norm_w[0],
                      nb=DEC_BATCH, seq=DEC_SEQ, lc=DEC_SEQ, z_row_off=T_PROMPT)
    o_ssd = jnp.concatenate([y_p, y_s], axis=0)

    gate = lambda acc, g: jax.nn.sigmoid(g) * acc
    m1 = _mm(o_mla, w_mla_o[0].astype(BF16), tm=640, tn=1024, out_dtype=F32,
             extras=[(big, BIG_GMLA // 1024)], epilogue=gate, name="mla_out")
    merged = _mm(o_ssd, w_ssd_o[0].astype(BF16), tm=640, tn=512, out_dtype=BF16,
                 extras=[(big, BIG_GSSD // 512), (m1, 0)],
                 epilogue=lambda acc, g, m: m + jax.nn.sigmoid(g) * acc, name="ssd_out")
    h = _mm(merged, w_out[0].astype(BF16), tm=640, tn=1024, out_dtype=F32,
            extras=[(x_all, 0)], epilogue=lambda acc, x: x + acc, name="out_proj")

    n_r = 128
    w_router = jnp.concatenate([w_group[0], w_erouter[0],
                                jnp.zeros((D_MODEL, n_r - N_GROUPS - N_EXPERTS), F32)], axis=1)
    b_router = jnp.concatenate([b_group[0], b_erouter[0], jnp.zeros((n_r - N_GROUPS - N_EXPERTS,), F32)])
    xn, logits = _norm_router(h, norm2_w[0], w_router, b_router.reshape(1, n_r))
    row_token, row_w, pos, be, first, nused = _route(logits)
    xs = jnp.take(xn, row_token, axis=0)
    hact = _moe_up(be, first, nused, xs, w_gate[0], w_up[0])
    yb = _moe_down(be, first, nused, hact, w_down[0], row_w.reshape(-1, 1))
    y_moe = jnp.take(yb, pos[:, 0], axis=0) + jnp.take(yb, pos[:, 1], axis=0)
    y_all = _final_norm(h, y_moe, final_norm_w)

    y_prompt = y_all[:T_PROMPT].reshape(BATCH, SEQ, D_MODEL)
    y_sample = y_all[T_PROMPT:].reshape(DEC_BATCH, DEC_SEQ, D_MODEL)
    ckv_p = ckv[:T_PROMPT].reshape(1, BATCH, SEQ, KV_LORA)
    ckv_s = ckv[T_PROMPT:].reshape(1, DEC_BATCH, DEC_SEQ, KV_LORA)
    kr_p = krope[:T_PROMPT].reshape(1, BATCH, SEQ, QK_ROPE)
    kr_s = krope[T_PROMPT:].reshape(1, DEC_BATCH, DEC_SEQ, QK_ROPE)
    tail = SSD_CONV - 1
    conv_p = jnp.stack([lax.slice(big, ((b + 1) * SEQ - tail, BIG_XBC), ((b + 1) * SEQ, BIG_XBC + SSD_CONV_DIM))
                        for b in range(BATCH)])[None]
    conv_s = jnp.stack([lax.slice(big, (T_PROMPT + (b + 1) * DEC_SEQ - tail, BIG_XBC),
                                  (T_PROMPT + (b + 1) * DEC_SEQ, BIG_XBC + SSD_CONV_DIM))
                        for b in range(DEC_BATCH)])[None]
    ssm_p = ssm_p.reshape(1, BATCH, SSD_HEADS, SSD_HEADDIM, SSD_STATE)
    ssm_s = ssm_s.reshape(1, DEC_BATCH, SSD_HEADS, SSD_HEADDIM, SSD_STATE)
    return (y_prompt, y_sample, ckv_p, kr_p, conv_p, ssm_p, ckv_s, kr_s, conv_s, ssm_s)
```

```python
import functools
import math

import numpy as np
import jax
import jax.numpy as jnp
from jax import lax
from jax.experimental import pallas as pl
from jax.experimental.pallas import tpu as pltpu

F32 = jnp.float32
BF16 = jnp.bfloat16
HI = lax.Precision.HIGHEST

D_MODEL = 4096
BATCH = 2
SEQ = 8192
DEC_BATCH = 8
DEC_SEQ = 32
PAST_LEN = 4096
CHUNK = 64
EPS = 1e-6
MLA_HEADS = 32
Q_LORA = 1024
KV_LORA = 512
QK_NOPE = 128
QK_ROPE = 64
V_HEAD = 128
ROPE_THETA = 10000.0
SCALE = (QK_NOPE + QK_ROPE) ** -0.5
QSCALE = SCALE * math.log2(math.e)
SSD_INNER = 2 * D_MODEL
SSD_HEADDIM = 64
SSD_HEADS = SSD_INNER // SSD_HEADDIM
SSD_STATE = 128
SSD_GROUPS = 8
SSD_RANK = SSD_HEADS // SSD_GROUPS
SSD_GCOLS = SSD_RANK * SSD_HEADDIM
SSD_CONV = 4
SSD_CONV_DIM = SSD_INNER + 2 * SSD_GROUPS * SSD_STATE
N_GROUPS = 8
EXPERTS_PER_GROUP = 8
N_EXPERTS = N_GROUPS * EXPERTS_PER_GROUP
TOP_K = 2
D_EXPERT = 1024

T_PROMPT = BATCH * SEQ
T_SAMPLE = DEC_BATCH * DEC_SEQ
T_ALL = T_PROMPT + T_SAMPLE

OFF_CQ = 0
OFF_CKV = OFF_CQ + Q_LORA
OFF_KR = OFF_CKV + KV_LORA
OFF_Z = OFF_KR + QK_ROPE
OFF_XBC = OFF_Z + SSD_INNER
OFF_DT = OFF_XBC + SSD_CONV_DIM
OFF_GMLA = OFF_DT + SSD_HEADS
OFF_GSSD = OFF_GMLA + D_MODEL
N_IN = OFF_GSSD + D_MODEL

BIG_Z = 0
BIG_XBC = SSD_INNER
BIG_GMLA = BIG_XBC + SSD_CONV_DIM
BIG_GSSD = BIG_GMLA + D_MODEL
BIG_N = BIG_GSSD + D_MODEL
SM_CQ = 0
SM_CKV = Q_LORA
SM_DT = SM_CKV + KV_LORA
SM_KR = SM_DT + SSD_HEADS
SM_KRS = SM_KR + QK_ROPE
SM_N = SM_KRS + QK_ROPE

V7X_VMEM_LIMIT = 56 * 1024 * 1024
ROW_TILE = 256
MM_TM = 1280
MOE_TM = 256
SSD_SUB = 4
MOE_TF = 512
MOE_TN = 2048
ATTN_TQ = 2048
ATTN_TK = 512


def _cparams(sem):
    return pltpu.CompilerParams(dimension_semantics=sem, vmem_limit_bytes=V7X_VMEM_LIMIT)


def _dot_nt(a, b):
    return lax.dot_general(a, b, (((1,), (1,)), ((), ())), preferred_element_type=F32)


def _rms(x, w):
    return x * lax.rsqrt(jnp.mean(x * x, axis=-1, keepdims=True) + EPS) * w


def _rms_rows_body(x_ref, w_ref, o_ref):
    o_ref[...] = _rms(x_ref[...], w_ref[...]).astype(o_ref.dtype)


def _rms_rows(x, w, out_dtype):
    m, d = x.shape
    return pl.pallas_call(
        _rms_rows_body,
        grid=(m // ROW_TILE,),
        in_specs=[pl.BlockSpec((ROW_TILE, d), lambda i: (i, 0)),
                  pl.BlockSpec((1, d), lambda i: (0, 0))],
        out_specs=pl.BlockSpec((ROW_TILE, d), lambda i: (i, 0)),
        out_shape=jax.ShapeDtypeStruct((m, d), out_dtype),
        compiler_params=_cparams(("parallel",)),
        name="rms_rows",
    )(x, w.reshape(1, d))


def _final_body(h_ref, ya_ref, yb_ref, w_ref, o_ref):
    y = ya_ref[...].astype(F32) + yb_ref[...].astype(F32)
    o_ref[...] = _rms(h_ref[...] + y, w_ref[...])


def _final_norm(h, y_a, y_b, w):
    m, d = h.shape
    row = pl.BlockSpec((ROW_TILE, d), lambda i: (i, 0))
    return pl.pallas_call(
        _final_body,
        grid=(m // ROW_TILE,),
        in_specs=[row, row, row, pl.BlockSpec((1, d), lambda i: (0, 0))],
        out_specs=row,
        out_shape=jax.ShapeDtypeStruct((m, d), F32),
        compiler_params=_cparams(("parallel",)),
        name="final_norm",
    )(h, y_a, y_b, w.reshape(1, d))


def _mm_body(epilogue, a_ref, w_ref, *rest):
    o_ref = rest[-1]
    acc = jnp.dot(a_ref[...], w_ref[...], preferred_element_type=F32)
    if epilogue is not None:
        acc = epilogue(acc, *[r[...] for r in rest[:-1]])
    o_ref[...] = acc.astype(o_ref.dtype)


def _mm(a, w, *, tm, tn, out_dtype, extras=(), epilogue=None, name="mm"):
    m, k = a.shape
    n = w.shape[1]
    in_specs = [pl.BlockSpec((tm, k), lambda i, j: (i, 0)),
                pl.BlockSpec((k, tn), lambda i, j: (0, j))]
    args = [a, w]
    for arr, off in extras:
        in_specs.append(pl.BlockSpec((tm, tn), functools.partial(lambda i, j, off: (i, j + off), off=off)))
        args.append(arr)
    return pl.pallas_call(
        functools.partial(_mm_body, epilogue),
        grid=(m // tm, n // tn),
        in_specs=in_specs,
        out_specs=pl.BlockSpec((tm, tn), lambda i, j: (i, j)),
        out_shape=jax.ShapeDtypeStruct((m, n), out_dtype),
        compiler_params=_cparams(("parallel", "parallel")),
        name=name,
    )(*args)


def _mla_prep_body(s_ref, qw_ref, kvw_ref, cos_ref, sin_ref, cqn_ref, ckv_ref, ckvb_ref, kr_ref, krb_ref):
    cqn_ref[...] = _rms(s_ref[:, SM_CQ:SM_CQ + Q_LORA], qw_ref[...]).astype(BF16)
    c = _rms(s_ref[:, SM_CKV:SM_CKV + KV_LORA], kvw_ref[...])
    ckv_ref[...] = c
    ckvb_ref[...] = c.astype(BF16)
    kr = (s_ref[:, SM_KR:SM_KR + QK_ROPE] * cos_ref[...]
          + s_ref[:, SM_KRS:SM_KRS + QK_ROPE] * sin_ref[...])
    kr_ref[...] = kr
    krb_ref[...] = kr.astype(BF16)


def _mla_prep(small, q_norm_w, kv_norm_w, cos2, sin2):
    m = small.shape[0]
    row = lambda n: pl.BlockSpec((ROW_TILE, n), lambda i: (i, 0))
    vec = lambda n: pl.BlockSpec((1, n), lambda i: (0, 0))
    return pl.pallas_call(
        _mla_prep_body,
        grid=(m // ROW_TILE,),
        in_specs=[row(SM_N), vec(Q_LORA), vec(KV_LORA), row(QK_ROPE), row(QK_ROPE)],
        out_specs=[row(Q_LORA), row(KV_LORA), row(KV_LORA), row(QK_ROPE), row(QK_ROPE)],
        out_shape=[jax.ShapeDtypeStruct((m, Q_LORA), BF16),
                   jax.ShapeDtypeStruct((m, KV_LORA), F32),
                   jax.ShapeDtypeStruct((m, KV_LORA), BF16),
                   jax.ShapeDtypeStruct((m, QK_ROPE), F32),
                   jax.ShapeDtypeStruct((m, QK_ROPE), BF16)],
        compiler_params=_cparams(("parallel",)),
        name="mla_prep",
    )(small, q_norm_w.reshape(1, -1), kv_norm_w.reshape(1, -1), cos2, sin2)


def _q_rope_body(a_ref, w_ref, ws_ref, cos_ref, sin_ref, o_ref):
    reps = o_ref.shape[1] // cos_ref.shape[1]
    a = a_ref[...]
    r = jnp.dot(a, w_ref[...], preferred_element_type=F32)
    rs = jnp.dot(a, ws_ref[...], preferred_element_type=F32)
    cos = jnp.tile(cos_ref[...], (1, reps))
    sin = jnp.tile(sin_ref[...], (1, reps))
    o_ref[...] = ((r * cos + rs * sin) * QSCALE).astype(o_ref.dtype)


def _q_rope(cqn, w_r, w_rs, cos128, sin128, *, tm, tn):
    m, k = cqn.shape
    n = w_r.shape[1]
    return pl.pallas_call(
        _q_rope_body,
        grid=(m // tm, n // tn),
        in_specs=[pl.BlockSpec((tm, k), lambda i, j: (i, 0)),
                  pl.BlockSpec((k, tn), lambda i, j: (0, j)),
                  pl.BlockSpec((k, tn), lambda i, j: (0, j)),
                  pl.BlockSpec((tm, 128), lambda i, j: (i, 0)),
                  pl.BlockSpec((tm, 128), lambda i, j: (i, 0))],
        out_specs=pl.BlockSpec((tm, tn), lambda i, j: (i, j)),
        out_shape=jax.ShapeDtypeStruct((m, n), BF16),
        compiler_params=_cparams(("parallel", "parallel")),
        name="q_rope",
    )(cqn, w_r, w_rs, cos128, sin128)


def _mm_t_body(w_ref, a_ref, o_ref):
    o_ref[0] = _dot_nt(w_ref[...], a_ref[...]).astype(o_ref.dtype)


def _mm_t(w_t, a, *, rows, tm, tn, out_dtype, name):
    n, k = w_t.shape
    return pl.pallas_call(
        _mm_t_body,
        grid=(rows // tm, n // tn),
        in_specs=[pl.BlockSpec((tn, k), lambda i, j: (j, 0)),
                  pl.BlockSpec((tm, k), lambda i, j: (i, 0))],
        out_specs=pl.BlockSpec((1, tn, tm), lambda i, j: (i, j, 0)),
        out_shape=jax.ShapeDtypeStruct((rows // tm, n, tm), out_dtype),
        compiler_params=_cparams(("parallel", "parallel")),
        name=name,
    )(w_t, a)


ATTN_KP = 256


ATTN_VP = 144


def _attn_prompt_body(qn_ref, qr_ref, kn_ref, kr_ref, vt_ref, o_ref, kf_sc, vf_sc, qf_sc, acc_sc):
    qi = pl.program_id(2)
    tq = qn_ref.shape[0]
    tk = vt_ref.shape[2]
    seq = kn_ref.shape[0]
    pad = ATTN_KP - QK_NOPE - QK_ROPE

    @pl.when(qi == 0)
    def _():
        def fill(i, c):
            r = pl.ds(pl.multiple_of(i * tk, tk), tk)
            kr = kr_ref[r, :]
            for hh in range(2):
                c0 = hh * ATTN_KP
                kf_sc[r, c0:c0 + QK_NOPE] = kn_ref[r, hh * QK_NOPE:(hh + 1) * QK_NOPE]
                kf_sc[r, c0 + QK_NOPE:c0 + QK_NOPE + QK_ROPE] = kr
                kf_sc[r, c0 + QK_NOPE + QK_ROPE:c0 + ATTN_KP] = jnp.zeros((tk, pad), BF16)
                vf_sc[i, hh * ATTN_VP:hh * ATTN_VP + V_HEAD, :] = vt_ref[i, hh * V_HEAD:(hh + 1) * V_HEAD, :]
                vf_sc[i, hh * ATTN_VP + V_HEAD:(hh + 1) * ATTN_VP, :] = jnp.ones((ATTN_VP - V_HEAD, tk), BF16)
            return c
        lax.fori_loop(0, seq // tk, fill, 0)

    for hh in range(2):
        c0 = hh * ATTN_KP
        qf_sc[:, c0:c0 + QK_NOPE] = qn_ref[:, hh * QK_NOPE:(hh + 1) * QK_NOPE]
        qf_sc[:, c0 + QK_NOPE:c0 + QK_NOPE + QK_ROPE] = qr_ref[:, hh * QK_ROPE:(hh + 1) * QK_ROPE]
        qf_sc[:, c0 + QK_NOPE + QK_ROPE:c0 + ATTN_KP] = jnp.zeros((tq, pad), BF16)
    acc_sc[...] = jnp.zeros(acc_sc.shape, F32)

    def block(kb, carry, diag):
        r = pl.ds(pl.multiple_of(kb * tk, tk), tk)
        q0 = 0 if diag is None else diag * tk
        nq_cols = tq - q0
        out = []
        for hh in range(2):
            m = carry[hh][:, q0:]
            st = _dot_nt(kf_sc[r, hh * ATTN_KP:(hh + 1) * ATTN_KP], qf_sc[q0:, hh * ATTN_KP:(hh + 1) * ATTN_KP])
            if diag is not None:
                kc = lax.broadcasted_iota(jnp.int32, (tk, nq_cols), 0) // CHUNK
                qc = lax.broadcasted_iota(jnp.int32, (tk, nq_cols), 1) // CHUNK
                st = jnp.where(kc <= qc, st, -jnp.inf)
            m_new = jnp.maximum(m, jnp.max(st, axis=0, keepdims=True))
            alpha = jnp.exp2(m - m_new)
            pt = jnp.exp2(st - m_new).astype(BF16)
            vt = vf_sc[kb, hh * ATTN_VP:(hh + 1) * ATTN_VP, :]
            acc_sc[hh, :, q0:] = alpha * acc_sc[hh, :, q0:] + jnp.dot(vt, pt, preferred_element_type=F32)
            out.append(m_new if q0 == 0 else jnp.concatenate([carry[hh][:, :q0], m_new], axis=1))
        return tuple(out)

    n_diag = tq // tk
    n_full = qi * n_diag
    init = (jnp.full((1, tq), -jnp.inf, F32),) * 2
    carry = lax.fori_loop(0, n_full, lambda kb, c: block(kb, c, None), init)
    for d in range(n_diag):
        carry = block(n_full + d, carry, d)
    for hh in range(2):
        o = acc_sc[hh, 0:V_HEAD, :] / acc_sc[hh, V_HEAD:V_HEAD + 1, :]
        o_ref[:, hh * V_HEAD:(hh + 1) * V_HEAD] = o.T.astype(o_ref.dtype)


def _attn_prompt(qn, qr, kn, krb, vt, *, nb, seq, heads, tq):
    tk = vt.shape[2]
    nq = seq // tq
    nk = seq // tk
    return pl.pallas_call(
        _attn_prompt_body,
        grid=(nb, heads // 2, nq),
        in_specs=[pl.BlockSpec((tq, 2 * QK_NOPE), lambda b, hp, qi: (b * nq + qi, hp)),
                  pl.BlockSpec((tq, 2 * QK_ROPE), lambda b, hp, qi: (b * nq + qi, hp)),
                  pl.BlockSpec((seq, 2 * QK_NOPE), lambda b, hp, qi: (b, hp)),
                  pl.BlockSpec((seq, QK_ROPE), lambda b, hp, qi: (b, 0)),
                  pl.BlockSpec((nk, 2 * V_HEAD, tk), lambda b, hp, qi: (b, hp, 0))],
        out_specs=pl.BlockSpec((tq, 2 * V_HEAD), lambda b, hp, qi: (b * nq + qi, hp)),
        out_shape=jax.ShapeDtypeStruct((nb * seq, heads * V_HEAD), BF16),
        scratch_shapes=[pltpu.VMEM((seq, 2 * ATTN_KP), BF16), pltpu.VMEM((nk, 2 * ATTN_VP, tk), BF16),
                        pltpu.VMEM((tq, 2 * ATTN_KP), BF16), pltpu.VMEM((2, ATTN_VP, tq), F32)],
        compiler_params=_cparams(("parallel", "parallel", "arbitrary")),
        name="attn_prompt",
    )(qn, qr, kn, krb, vt)


def _bmm_body(a_ref, w_ref, o_ref):
    o_ref[0] = jnp.dot(a_ref[0], w_ref[0], preferred_element_type=F32).astype(o_ref.dtype)


def _bmm(a, w, out_dtype, name):
    h, m, k = a.shape
    n = w.shape[2]
    return pl.pallas_call(
        _bmm_body,
        grid=(h,),
        in_specs=[pl.BlockSpec((1, m, k), lambda i: (i, 0, 0)),
                  pl.BlockSpec((1, k, n), lambda i: (i, 0, 0))],
        out_specs=pl.BlockSpec((1, m, n), lambda i: (i, 0, 0)),
        out_shape=jax.ShapeDtypeStruct((h, m, n), out_dtype),
        compiler_params=_cparams(("parallel",)),
        name=name,
    )(a, w)


def _attn_sample_body(ql_ref, qr_ref, cc_ref, ck_ref, nc_ref, nk_ref, o_ref, ccb_sc, ckb_sc):
    @pl.when(pl.program_id(1) == 0)
    def _():
        ccb_sc[...] = cc_ref[0].astype(BF16)
        ckb_sc[...] = ck_ref[0].astype(BF16)

    ql = ql_ref[0]
    qr = qr_ref[0]
    ncb = nc_ref[...].astype(BF16)
    nkb = nk_ref[...].astype(BF16)
    s1 = _dot_nt(ql, ccb_sc[...]) + _dot_nt(qr, ckb_sc[...])
    s2 = _dot_nt(ql, ncb) + _dot_nt(qr, nkb)
    m = jnp.maximum(jnp.max(s1, axis=1, keepdims=True), jnp.max(s2, axis=1, keepdims=True))
    p1 = jnp.exp2(s1 - m)
    p2 = jnp.exp2(s2 - m)
    l = jnp.sum(p1, axis=1, keepdims=True) + jnp.sum(p2, axis=1, keepdims=True)
    o = (jnp.dot(p1.astype(BF16), ccb_sc[...], preferred_element_type=F32)
         + jnp.dot(p2.astype(BF16), ncb, preferred_element_type=F32))
    o_ref[0] = (o / l).astype(o_ref.dtype)


def _attn_sample(q_lat, q_rope, cache_ckv, cache_krope, ckv, krope):
    rows = q_lat.shape[1]
    tr = 256
    new_blk = T_PROMPT // DEC_SEQ
    return pl.pallas_call(
        _attn_sample_body,
        grid=(DEC_BATCH, rows // tr),
        in_specs=[pl.BlockSpec((1, tr, KV_LORA), lambda b, r: (b, r, 0)),
                  pl.BlockSpec((1, tr, QK_ROPE), lambda b, r: (b, r, 0)),
                  pl.BlockSpec((1, PAST_LEN, KV_LORA), lambda b, r: (b, 0, 0)),
                  pl.BlockSpec((1, PAST_LEN, QK_ROPE), lambda b, r: (b, 0, 0)),
                  pl.BlockSpec((DEC_SEQ, KV_LORA), lambda b, r: (new_blk + b, 0)),
                  pl.BlockSpec((DEC_SEQ, QK_ROPE), lambda b, r: (new_blk + b, 0))],
        out_specs=pl.BlockSpec((1, tr, KV_LORA), lambda b, r: (b, r, 0)),
        out_shape=jax.ShapeDtypeStruct((DEC_BATCH, rows, KV_LORA), BF16),
        scratch_shapes=[pltpu.VMEM((PAST_LEN, KV_LORA), BF16), pltpu.VMEM((PAST_LEN, QK_ROPE), BF16)],
        compiler_params=_cparams(("parallel", "arbitrary")),
        name="attn_sample",
    )(q_lat, q_rope, cache_ckv, cache_krope, ckv, krope)


def _conv_body(x_ref, prev_ref, w_ref, b_ref, o_ref, ext_sc):
    tl = x_ref.shape[0]

    @pl.when(pl.program_id(2) == 0)
    def _():
        ext_sc[0:8, :] = prev_ref[0]

    @pl.when(pl.program_id(2) != 0)
    def _():
        ext_sc[0:8, :] = ext_sc[tl:tl + 8, :]

    ext_sc[8:8 + tl, :] = x_ref[...]
    acc = b_ref[...] + ext_sc[5:5 + tl, :] * w_ref[0:1, :]
    for k in range(1, SSD_CONV):
        acc = acc + ext_sc[5 + k:5 + k + tl, :] * w_ref[k:k + 1, :]
    o_ref[...] = (acc * jax.nn.sigmoid(acc)).astype(o_ref.dtype)


def _conv_silu(big, prev8, conv_w, conv_b, *, nb, seq, tl, row_off):
    tc = 1024
    nrt = seq // tl
    rb0 = row_off // tl
    cb0 = BIG_XBC // tc
    return pl.pallas_call(
        _conv_body,
        grid=(nb, SSD_CONV_DIM // tc, nrt),
        in_specs=[pl.BlockSpec((tl, tc), lambda b, j, r: (rb0 + b * nrt + r, cb0 + j)),
                  pl.BlockSpec((1, 8, tc), lambda b, j, r: (b, 0, j)),
                  pl.BlockSpec((SSD_CONV, tc), lambda b, j, r: (0, j)),
                  pl.BlockSpec((1, tc), lambda b, j, r: (0, j))],
        out_specs=pl.BlockSpec((tl, tc), lambda b, j, r: (b * nrt + r, j)),
        out_shape=jax.ShapeDtypeStruct((nb * seq, SSD_CONV_DIM), BF16),
        scratch_shapes=[pltpu.VMEM((tl + 8, tc), F32)],
        compiler_params=_cparams(("parallel", "parallel", "arbitrary")),
        name="conv_silu",
    )(big, prev8, conv_w, conv_b.reshape(1, -1))


def _softplus(x):
    return jnp.maximum(x, 0.0) + jnp.log(1.0 + jnp.exp(-jnp.abs(x)))


def _dot_sel(sel, a, *, sel_left):
    hi = a.astype(BF16)
    r1 = a - hi.astype(F32)
    mid = r1.astype(BF16)
    lo = (r1 - mid.astype(F32)).astype(BF16)
    out = None
    for term in (hi, mid, lo):
        d = (jnp.dot(sel, term, preferred_element_type=F32) if sel_left
             else jnp.dot(term, sel, preferred_element_type=F32))
        out = d if out is None else out + d
    return out


def _ssd_body(has_h0, x_ref, b_ref, c_ref, z_ref, dt_ref, bias_ref, al_ref, dsk_ref, nw_ref, *rest):
    if has_h0:
        h0_ref, y_ref, hout_ref, ht_sc = rest
    else:
        y_ref, hout_ref, ht_sc = rest
    R, P = SSD_RANK, SSD_HEADDIM
    lc = P
    W = 2 * P
    c_idx = pl.program_id(2)

    @pl.when(c_idx == 0)
    def _():
        if has_h0:
            ht_sc[...] = h0_ref[0].T
        else:
            ht_sc[...] = jnp.zeros(ht_sc.shape, F32)

    ri = lax.broadcasted_iota(jnp.int32, (lc, lc), 0)
    ci = lax.broadcasted_iota(jnp.int32, (lc, lc), 1)
    tril_b = jnp.where(ri >= ci, 1.0, 0.0).astype(BF16)
    er = lax.broadcasted_iota(jnp.int32, (R, R * P), 0)
    ec = lax.broadcasted_iota(jnp.int32, (R, R * P), 1) // P
    expand = jnp.where(er == ec, 1.0, 0.0).astype(BF16)
    li = lax.broadcasted_iota(jnp.int32, (lc, R * P), 0)
    si = lax.broadcasted_iota(jnp.int32, (lc, R * P), 1) % P
    first_head = lax.broadcasted_iota(jnp.int32, (lc, W), 1) < P
    neg_a = -jnp.exp(al_ref[0])

    ht = ht_sc[...]
    for k in range(x_ref.shape[0] // lc):
        rows = slice(k * lc, (k + 1) * lc)
        dt = _softplus(dt_ref[0, rows, :] + bias_ref[0])
        a_cs = _dot_sel(tril_b, dt * neg_a, sel_left=True)
        a_exp = _dot_sel(expand, a_cs, sel_left=False)
        dt_exp = _dot_sel(expand, dt, sel_left=False)
        a_last = a_exp[lc - 1:lc, :]
        a_key = jnp.sum(jnp.where(li == si, a_exp, 0.0), axis=0, keepdims=True)
        decay = jnp.exp(jnp.where(li >= si, a_exp - a_key, -jnp.inf))

        x = x_ref[rows, :].astype(F32)
        xdt = x * dt_exp
        bm = b_ref[rows, :].astype(BF16)
        cm = c_ref[rows, :].astype(BF16)
        cb2 = _dot_nt(cm, jnp.concatenate([bm, bm], axis=0))
        m_all = (jnp.tile(cb2, (1, R // 2)) * decay).astype(BF16)
        y_off = jnp.dot(cm, ht.astype(BF16), preferred_element_type=F32) * jnp.exp(a_exp)
        xdt_b = xdt.astype(BF16)
        y_diag = []
        for j in range(R // 2):
            xp = xdt_b[:, j * W:(j + 1) * W]
            zero = jnp.zeros_like(xp)
            stack = jnp.concatenate([jnp.where(first_head, xp, zero), jnp.where(first_head, zero, xp)], axis=0)
            y_diag.append(jnp.dot(m_all[:, j * W:(j + 1) * W], stack, preferred_element_type=F32))
        wgt = (xdt * jnp.exp(a_last - a_exp)).astype(BF16)
        states_t = lax.dot_general(bm, wgt, (((0,), (0,)), ((), ())), preferred_element_type=F32)
        ht = ht * jnp.exp(a_last) + states_t

        y = jnp.concatenate(y_diag, axis=1) + y_off + x * dsk_ref[...]
        z = z_ref[rows, :]
        y = y * (z * jax.nn.sigmoid(z))
        y = y * lax.rsqrt(jnp.mean(y * y, axis=-1, keepdims=True) + EPS) * nw_ref[...]
        y_ref[rows, :] = y.astype(y_ref.dtype)
    ht_sc[...] = ht

    @pl.when(c_idx == pl.num_programs(2) - 1)
    def _():
        hout_ref[0] = ht.T


def _ssd(xbc, zsrc, dt_g, h0, dt_bias, a_log, d_skip, ssd_norm_w, *, nb, seq, sub):
    lc = sub * SSD_HEADDIM
    nc = seq // lc
    G, R, GC, N = SSD_GROUPS, SSD_RANK, SSD_GCOLS, SSD_STATE
    row = lambda b, g, c: b * nc + c
    in_specs = [
        pl.BlockSpec((lc, GC), lambda b, g, c: (row(b, g, c), g)),
        pl.BlockSpec((lc, N), lambda b, g, c: (row(b, g, c), SSD_INNER // N + g)),
        pl.BlockSpec((lc, N), lambda b, g, c: (row(b, g, c), SSD_INNER // N + G + g)),
        pl.BlockSpec((lc, GC), lambda b, g, c: (row(b, g, c), g)),
        pl.BlockSpec((1, lc, R), lambda b, g, c: (g, row(b, g, c), 0)),
        pl.BlockSpec((1, 1, R), lambda b, g, c: (g, 0, 0)),
        pl.BlockSpec((1, 1, R), lambda b, g, c: (g, 0, 0)),
        pl.BlockSpec((1, GC), lambda b, g, c: (0, g)),
        pl.BlockSpec((1, GC), lambda b, g, c: (0, g)),
    ]
    args = [xbc, xbc, xbc, zsrc, dt_g, dt_bias.reshape(G, 1, R), a_log.reshape(G, 1, R),
            jnp.repeat(d_skip, SSD_HEADDIM).reshape(1, SSD_INNER), ssd_norm_w.reshape(1, SSD_INNER)]
    if h0 is not None:
        in_specs.append(pl.BlockSpec((1, GC, N), lambda b, g, c: (b, g, 0)))
        args.append(h0)
    return pl.pallas_call(
        functools.partial(_ssd_body, h0 is not None),
        grid=(nb, G, nc),
        in_specs=in_specs,
        out_specs=[pl.BlockSpec((lc, GC), lambda b, g, c: (row(b, g, c), g)),
                   pl.BlockSpec((1, GC, N), lambda b, g, c: (b, g, 0))],
        out_shape=[jax.ShapeDtypeStruct((nb * seq, SSD_INNER), BF16),
                   jax.ShapeDtypeStruct((nb, SSD_HEADS * SSD_HEADDIM, N), F32)],
        scratch_shapes=[pltpu.VMEM((N, GC), F32)],
        compiler_params=_cparams(("parallel", "parallel", "arbitrary")),
        name="ssd",
    )(*args)


def _norm_router_body(h_ref, w_ref, wr_ref, br_ref, xn_ref, lg_ref):
    xn = _rms(h_ref[...], w_ref[...])
    xn_ref[...] = xn.astype(BF16)
    lg_ref[...] = jnp.dot(xn, wr_ref[...], precision=HI, preferred_element_type=F32) + br_ref[...]


def _norm_router(h, norm_w, w_router, b_router):
    m, d = h.shape
    n = w_router.shape[1]
    return pl.pallas_call(
        _norm_router_body,
        grid=(m // ROW_TILE,),
        in_specs=[pl.BlockSpec((ROW_TILE, d), lambda i: (i, 0)),
                  pl.BlockSpec((1, d), lambda i: (0, 0)),
                  pl.BlockSpec((d, n), lambda i: (0, 0)),
                  pl.BlockSpec((1, n), lambda i: (0, 0))],
        out_specs=[pl.BlockSpec((ROW_TILE, d), lambda i: (i, 0)),
                   pl.BlockSpec((ROW_TILE, n), lambda i: (i, 0))],
        out_shape=[jax.ShapeDtypeStruct((m, d), BF16), jax.ShapeDtypeStruct((m, n), F32)],
        compiler_params=_cparams(("parallel",)),
        name="norm_router",
    )(h, norm_w.reshape(1, d), w_router, b_router)


def _moe_up_body(be_ref, first_ref, nused_ref, x_ref, wg_ref, wu_ref, o_ref, wgb_sc, wub_sc):
    blk = pl.program_id(1)

    @pl.when(first_ref[blk] == 1)
    def _():
        wgb_sc[...] = wg_ref[0].astype(BF16)
        wub_sc[...] = wu_ref[0].astype(BF16)

    @pl.when(blk < nused_ref[0])
    def _():
        x = x_ref[...]
        g = jnp.dot(x, wgb_sc[...], preferred_element_type=F32)
        u = jnp.dot(x, wub_sc[...], preferred_element_type=F32)
        o_ref[...] = (g * jax.nn.sigmoid(g) * u).astype(o_ref.dtype)

    @pl.when(blk >= nused_ref[0])
    def _():
        o_ref[...] = jnp.zeros(o_ref.shape, o_ref.dtype)


def _moe_up(be, first, nused, xs, w_gate, w_up):
    p, d = xs.shape
    nblk = p // MOE_TM
    grid_spec = pltpu.PrefetchScalarGridSpec(
        num_scalar_prefetch=3,
        grid=(D_EXPERT // MOE_TF, nblk),
        in_specs=[pl.BlockSpec((MOE_TM, d), lambda f, i, be, fi, nu: (i, 0)),
                  pl.BlockSpec((1, d, MOE_TF), lambda f, i, be, fi, nu: (be[i], 0, f)),
                  pl.BlockSpec((1, d, MOE_TF), lambda f, i, be, fi, nu: (be[i], 0, f))],
        out_specs=pl.BlockSpec((MOE_TM, MOE_TF), lambda f, i, be, fi, nu: (i, f)),
        scratch_shapes=[pltpu.VMEM((d, MOE_TF), BF16), pltpu.VMEM((d, MOE_TF), BF16)],
    )
    return pl.pallas_call(
        _moe_up_body,
        grid_spec=grid_spec,
        out_shape=jax.ShapeDtypeStruct((p, D_EXPERT), BF16),
        compiler_params=_cparams(("arbitrary", "arbitrary")),
        name="moe_up",
    )(be, first, nused, xs, w_gate, w_up)


def _moe_down_body(be_ref, first_ref, nused_ref, h_ref, wd_ref, rw_ref, o_ref, wdb_sc):
    blk = pl.program_id(1)

    @pl.when(first_ref[blk] == 1)
    def _():
        wdb_sc[...] = wd_ref[0].astype(BF16)

    @pl.when(blk < nused_ref[0])
    def _():
        o_ref[...] = (jnp.dot(h_ref[...], wdb_sc[...], preferred_element_type=F32) * rw_ref[...]).astype(o_ref.dtype)

    @pl.when(blk >= nused_ref[0])
    def _():
        o_ref[...] = jnp.zeros(o_ref.shape, o_ref.dtype)


def _moe_down(be, first, nused, hact, w_down, row_w):
    p, f = hact.shape
    d = w_down.shape[2]
    nblk = p // MOE_TM
    grid_spec = pltpu.PrefetchScalarGridSpec(
        num_scalar_prefetch=3,
        grid=(d // MOE_TN, nblk),
        in_specs=[pl.BlockSpec((MOE_TM, f), lambda n, i, be, fi, nu: (i, 0)),
                  pl.BlockSpec((1, f, MOE_TN), lambda n, i, be, fi, nu: (be[i], 0, n)),
                  pl.BlockSpec((MOE_TM, 1), lambda n, i, be, fi, nu: (i, 0))],
        out_specs=pl.BlockSpec((MOE_TM, MOE_TN), lambda n, i, be, fi, nu: (i, n)),
        scratch_shapes=[pltpu.VMEM((f, MOE_TN), BF16)],
    )
    return pl.pallas_call(
        _moe_down_body,
        grid_spec=grid_spec,
        out_shape=jax.ShapeDtypeStruct((p, d), BF16),
        compiler_params=_cparams(("arbitrary", "arbitrary")),
        name="moe_down",
    )(be, first, nused, hact, w_down, row_w)


def _route(logits):
    t = logits.shape[0]
    g_logits = logits[:, :N_GROUPS]
    g_sel = jnp.argmax(g_logits, axis=-1)
    g_w = jnp.max(jax.nn.softmax(g_logits, axis=-1), axis=-1)
    e_logits = logits[:, N_GROUPS:N_GROUPS + N_EXPERTS].reshape(t, N_GROUPS, EXPERTS_PER_GROUP)
    e_in = jnp.take_along_axis(e_logits, g_sel[:, None, None], axis=1)[:, 0]
    e_val, e_idx = lax.top_k(e_in, TOP_K)
    e_w = jax.nn.softmax(e_val, axis=-1) * g_w[:, None]
    expert_id = (g_sel[:, None] * EXPERTS_PER_GROUP + e_idx).reshape(-1).astype(jnp.int32)
    a = t * TOP_K
    token_id = jnp.repeat(jnp.arange(t, dtype=jnp.int32), TOP_K)
    e_sorted, order = lax.sort((expert_id, jnp.arange(a, dtype=jnp.int32)), num_keys=1, is_stable=True)
    experts = jnp.arange(N_EXPERTS, dtype=jnp.int32)
    start = jnp.searchsorted(e_sorted, experts, side='left').astype(jnp.int32)
    counts = jnp.searchsorted(e_sorted, experts, side='right').astype(jnp.int32) - start
    padded = (counts + MOE_TM - 1) // MOE_TM * MOE_TM
    pad_end = jnp.cumsum(padded)
    pad_start = pad_end - padded
    dest = pad_start[e_sorted] + jnp.arange(a, dtype=jnp.int32) - start[e_sorted]
    pos = lax.sort((order, dest), num_keys=1)[1].reshape(t, TOP_K)
    nblk = a // MOE_TM + N_EXPERTS
    nused = (pad_end[-1] // MOE_TM).astype(jnp.int32)
    blk = jnp.arange(nblk, dtype=jnp.int32)
    be = jnp.minimum(jnp.searchsorted(pad_end, blk * MOE_TM, side='right'), N_EXPERTS - 1).astype(jnp.int32)
    be = jnp.where(blk < nused, be, be[jnp.maximum(nused - 1, 0)])
    first = jnp.concatenate([jnp.ones((1,), jnp.int32), (be[1:] != be[:-1]).astype(jnp.int32)])
    row_e = jnp.repeat(be, MOE_TM)
    rank = jnp.arange(nblk * MOE_TM, dtype=jnp.int32) - pad_start[row_e]
    valid = rank < counts[row_e]
    src = jnp.clip(start[row_e] + rank, 0, a - 1)
    row_token = jnp.where(valid, token_id[order][src], 0)
    row_w = jnp.where(valid, e_w.reshape(-1)[order][src], 0.0)
    return row_token, row_w, pos, be, first, nused.reshape(1)


def _rope_tables():
    half = QK_ROPE // 2
    inv_freq = ROPE_THETA ** (-jnp.arange(half, dtype=F32) / half)
    pos = jnp.concatenate([jnp.tile(jnp.arange(SEQ), BATCH),
                           jnp.tile(PAST_LEN + jnp.arange(DEC_SEQ), DEC_BATCH)]).astype(F32)
    ang = pos[:, None] * inv_freq[None, :]
    cos, sin = jnp.cos(ang), jnp.sin(ang)
    cos2 = jnp.concatenate([cos, cos], axis=1)
    sin2 = jnp.concatenate([-sin, sin], axis=1)
    return cos2, sin2


def kernel(x_prompt, x_sample, cache_ckv, cache_krope, state_conv, state_ssm, norm1_w, w_in, q_norm_w, kv_norm_w, w_uq, w_ukv, conv_w, conv_b, dt_bias, a_log, d_skip, ssd_norm_w, w_mla_o, w_ssd_o, w_out, norm2_w, w_group, b_group, w_erouter, b_erouter, w_gate, w_up, w_down, final_norm_w):
    swap = np.concatenate([np.arange(QK_ROPE // 2, QK_ROPE), np.arange(QK_ROPE // 2)])
    x_all = jnp.concatenate([x_prompt.reshape(T_PROMPT, D_MODEL), x_sample.reshape(T_SAMPLE, D_MODEL)], axis=0)
    cos2, sin2 = _rope_tables()
    cos128, sin128 = jnp.tile(cos2, (1, 2)), jnp.tile(sin2, (1, 2))

    wi = w_in[0]
    w_kr = wi[:, OFF_KR:OFF_KR + QK_ROPE]
    w_small = jnp.concatenate([wi[:, OFF_CQ:OFF_KR], wi[:, OFF_DT:OFF_DT + SSD_HEADS], w_kr, w_kr[:, swap]],
                              axis=1).astype(BF16)
    w_big = jnp.concatenate([wi[:, OFF_Z:OFF_DT], wi[:, OFF_GMLA:]], axis=1).astype(BF16)
    wq = w_uq[0].reshape(Q_LORA, MLA_HEADS, QK_NOPE + QK_ROPE)
    wq_nope = wq[:, :, :QK_NOPE].reshape(Q_LORA, MLA_HEADS * QK_NOPE).astype(BF16)
    wq_rope = wq[:, :, QK_NOPE:]
    wq_r = wq_rope.reshape(Q_LORA, MLA_HEADS * QK_ROPE).astype(BF16)
    wq_rs = wq_rope[:, :, swap].reshape(Q_LORA, MLA_HEADS * QK_ROPE).astype(BF16)
    wkv3 = w_ukv[0].reshape(KV_LORA, MLA_HEADS, QK_NOPE + V_HEAD)
    w_uk_all = wkv3[:, :, :QK_NOPE].reshape(KV_LORA, MLA_HEADS * QK_NOPE).astype(BF16)
    w_uv_all_t = jnp.transpose(wkv3[:, :, QK_NOPE:], (1, 2, 0)).reshape(MLA_HEADS * V_HEAD, KV_LORA).astype(BF16)
    w_uk_t = jnp.transpose(wkv3[:, :, :QK_NOPE], (1, 2, 0)).astype(BF16)
    w_uv_h = jnp.transpose(wkv3[:, :, QK_NOPE:], (1, 0, 2)).astype(BF16)

    u = _rms_rows(x_all, norm1_w[0], BF16)
    small = _mm(u, w_small, tm=MM_TM, tn=SM_N // 2, out_dtype=F32, name="proj_small")
    big = _mm(u, w_big, tm=MM_TM, tn=512, out_dtype=F32, name="proj_big")

    cqn, ckv, ckv_b, krope, krope_b = _mla_prep(small, q_norm_w[0], kv_norm_w[0], cos2, sin2)
    qn = _mm(cqn, wq_nope, tm=MM_TM, tn=1024, out_dtype=BF16, epilogue=lambda acc: acc * QSCALE, name="q_nope")
    qr = _q_rope(cqn, wq_r, wq_rs, cos128, sin128, tm=MM_TM, tn=1024)
    kn = _mm(ckv_b, w_uk_all, tm=MM_TM, tn=1024, out_dtype=BF16, name="k_up")
    vt = _mm_t(w_uv_all_t, ckv_b, rows=T_PROMPT, tm=ATTN_TK, tn=1024, out_dtype=BF16, name="v_up_t")
    o_mla_p = _attn_prompt(qn, qr, kn, krope_b, vt, nb=BATCH, seq=SEQ, heads=MLA_HEADS, tq=ATTN_TQ)

    qn_s = qn[T_PROMPT:].reshape(T_SAMPLE, MLA_HEADS, QK_NOPE).transpose(1, 0, 2)
    q_lat = _bmm(qn_s, w_uk_t, BF16, "q_absorb")
    q_lat = q_lat.reshape(MLA_HEADS, DEC_BATCH, DEC_SEQ, KV_LORA).transpose(1, 0, 2, 3)
    q_lat = q_lat.reshape(DEC_BATCH, MLA_HEADS * DEC_SEQ, KV_LORA)
    qr_s = qr[T_PROMPT:].reshape(DEC_BATCH, DEC_SEQ, MLA_HEADS, QK_ROPE).transpose(0, 2, 1, 3)
    qr_s = qr_s.reshape(DEC_BATCH, MLA_HEADS * DEC_SEQ, QK_ROPE)
    o_lat = _attn_sample(q_lat, qr_s, cache_ckv[0], cache_krope[0], ckv, krope)
    o_lat = o_lat.reshape(DEC_BATCH, MLA_HEADS, DEC_SEQ, KV_LORA).transpose(1, 0, 2, 3)
    o_lat = o_lat.reshape(MLA_HEADS, T_SAMPLE, KV_LORA)
    o_mla_s = _bmm(o_lat, w_uv_h, BF16, "v_absorb")
    o_mla_s = o_mla_s.transpose(1, 0, 2).reshape(T_SAMPLE, MLA_HEADS * V_HEAD)
    o_mla = jnp.concatenate([o_mla_p, o_mla_s], axis=0)

    prev_p = jnp.zeros((BATCH, 8, SSD_CONV_DIM), F32)
    prev_s = jnp.concatenate([jnp.zeros((DEC_BATCH, 8 - (SSD_CONV - 1), SSD_CONV_DIM), F32), state_conv[0]], axis=1)
    xbc_p = _conv_silu(big, prev_p, conv_w[0], conv_b[0], nb=BATCH, seq=SEQ, tl=512, row_off=0)
    xbc_s = _conv_silu(big, prev_s, conv_w[0], conv_b[0], nb=DEC_BATCH, seq=DEC_SEQ, tl=DEC_SEQ, row_off=T_PROMPT)
    dt_raw = small[:, SM_DT:SM_DT + SSD_HEADS]
    by_group = lambda d: d.reshape(d.shape[0], SSD_GROUPS, SSD_RANK).transpose(1, 0, 2)

    def pad_seq(a, fill):
        a = a.reshape(DEC_BATCH, DEC_SEQ, a.shape[-1])
        a = jnp.pad(a, ((0, 0), (0, CHUNK - DEC_SEQ), (0, 0)), constant_values=fill)
        return a.reshape(DEC_BATCH * CHUNK, a.shape[-1])

    h0_s = state_ssm[0].reshape(DEC_BATCH, SSD_HEADS * SSD_HEADDIM, SSD_STATE)
    y_p, ssm_p = _ssd(xbc_p, big, by_group(dt_raw[:T_PROMPT]), None, dt_bias[0], a_log[0], d_skip[0],
                      ssd_norm_w[0], nb=BATCH, seq=SEQ, sub=SSD_SUB)
    y_s, ssm_s = _ssd(pad_seq(xbc_s, 0.0), pad_seq(big[T_PROMPT:, BIG_Z:BIG_Z + SSD_INNER], 0.0),
                      by_group(pad_seq(dt_raw[T_PROMPT:], -jnp.inf)), h0_s, dt_bias[0], a_log[0], d_skip[0],
                      ssd_norm_w[0], nb=DEC_BATCH, seq=CHUNK, sub=1)
    y_s = y_s.reshape(DEC_BATCH, CHUNK, SSD_INNER)[:, :DEC_SEQ].reshape(T_SAMPLE, SSD_INNER)
    o_ssd = jnp.concatenate([y_p, y_s], axis=0)

    gate = lambda acc, g: jax.nn.sigmoid(g) * acc
    m1 = _mm(o_mla, w_mla_o[0].astype(BF16), tm=640, tn=1024, out_dtype=F32,
             extras=[(big, BIG_GMLA // 1024)], epilogue=gate, name="mla_out")
    merged = _mm(o_ssd, w_ssd_o[0].astype(BF16), tm=640, tn=512, out_dtype=BF16,
                 extras=[(big, BIG_GSSD // 512), (m1, 0)],
                 epilogue=lambda acc, g, m: m + jax.nn.sigmoid(g) * acc, name="ssd_out")
    h = _mm(merged, w_out[0].astype(BF16), tm=640, tn=1024, out_dtype=F32,
            extras=[(x_all, 0)], epilogue=lambda acc, x: x + acc, name="out_proj")

    n_r = 128
    w_router = jnp.concatenate([w_group[0], w_erouter[0],
                                jnp.zeros((D_MODEL, n_r - N_GROUPS - N_EXPERTS), F32)], axis=1)
    b_router = jnp.concatenate([b_group[0], b_erouter[0], jnp.zeros((n_r - N_GROUPS - N_EXPERTS,), F32)])
    xn, logits = _norm_router(h, norm2_w[0], w_router, b_router.reshape(1, n_r))
    row_token, row_w, pos, be, first, nused = _route(logits)
    xs = jnp.take(xn, row_token, axis=0)
    hact = _moe_up(be, first, nused, xs, w_gate[0], w_up[0])
    yb = _moe_down(be, first, nused, hact, w_down[0], row_w.reshape(-1, 1))
    y_all = _final_norm(h, jnp.take(yb, pos[:, 0], axis=0), jnp.take(yb, pos[:, 1], axis=0), final_norm_w)

    y_prompt = y_all[:T_PROMPT].reshape(BATCH, SEQ, D_MODEL)
    y_sample = y_all[T_PROMPT:].reshape(DEC_BATCH, DEC_SEQ, D_MODEL)
    ckv_p = ckv[:T_PROMPT].reshape(1, BATCH, SEQ, KV_LORA)
    ckv_s = ckv[T_PROMPT:].reshape(1, DEC_BATCH, DEC_SEQ, KV_LORA)
    kr_p = krope[:T_PROMPT].reshape(1, BATCH, SEQ, QK_ROPE)
    kr_s = krope[T_PROMPT:].reshape(1, DEC_BATCH, DEC_SEQ, QK_ROPE)
    tail = SSD_CONV - 1
    conv_p = jnp.stack([lax.slice(big, ((b + 1) * SEQ - tail, BIG_XBC), ((b + 1) * SEQ, BIG_XBC + SSD_CONV_DIM))
                        for b in range(BATCH)])[None]
    conv_s = jnp.stack([lax.slice(big, (T_PROMPT + (b + 1) * DEC_SEQ - tail, BIG_XBC),
                                  (T_PROMPT + (b + 1) * DEC_SEQ, BIG_XBC + SSD_CONV_DIM))
                        for b in range(DEC_BATCH)])[None]
    ssm_p = ssm_p.reshape(1, BATCH, SSD_HEADS, SSD_HEADDIM, SSD_STATE)
    ssm_s = ssm_s.reshape(1, DEC_BATCH, SSD_HEADS, SSD_HEADDIM, SSD_STATE)
    return (y_prompt, y_sample, ckv_p, kr_p, conv_p, ssm_p, ckv_s, kr_s, conv_s, ssm_s)
```

```python
import functools
import math

import numpy as np
import jax
import jax.numpy as jnp
from jax import lax
from jax.experimental import pallas as pl
from jax.experimental.pallas import tpu as pltpu

F32 = jnp.float32
BF16 = jnp.bfloat16
HI = lax.Precision.HIGHEST

D_MODEL = 4096
BATCH = 2
SEQ = 8192
DEC_BATCH = 8
DEC_SEQ = 32
PAST_LEN = 4096
CHUNK = 64
EPS = 1e-6
MLA_HEADS = 32
Q_LORA = 1024
KV_LORA = 512
QK_NOPE = 128
QK_ROPE = 64
V_HEAD = 128
ROPE_THETA = 10000.0
SCALE = (QK_NOPE + QK_ROPE) ** -0.5
QSCALE = SCALE * math.log2(math.e)
SSD_INNER = 2 * D_MODEL
SSD_HEADDIM = 64
SSD_HEADS = SSD_INNER // SSD_HEADDIM
SSD_STATE = 128
SSD_GROUPS = 8
SSD_RANK = SSD_HEADS // SSD_GROUPS
SSD_GCOLS = SSD_RANK * SSD_HEADDIM
SSD_CONV = 4
SSD_CONV_DIM = SSD_INNER + 2 * SSD_GROUPS * SSD_STATE
N_GROUPS = 8
EXPERTS_PER_GROUP = 8
N_EXPERTS = N_GROUPS * EXPERTS_PER_GROUP
TOP_K = 2
D_EXPERT = 1024

T_PROMPT = BATCH * SEQ
T_SAMPLE = DEC_BATCH * DEC_SEQ
T_ALL = T_PROMPT + T_SAMPLE

OFF_CQ = 0
OFF_CKV = OFF_CQ + Q_LORA
OFF_KR = OFF_CKV + KV_LORA
OFF_Z = OFF_KR + QK_ROPE
OFF_XBC = OFF_Z + SSD_INNER
OFF_DT = OFF_XBC + SSD_CONV_DIM
OFF_GMLA = OFF_DT + SSD_HEADS
OFF_GSSD = OFF_GMLA + D_MODEL
N_IN = OFF_GSSD + D_MODEL

BIG_Z = 0
BIG_XBC = SSD_INNER
BIG_GMLA = BIG_XBC + SSD_CONV_DIM
BIG_GSSD = BIG_GMLA + D_MODEL
BIG_N = BIG_GSSD + D_MODEL
SM_CQ = 0
SM_CKV = Q_LORA
SM_DT = SM_CKV + KV_LORA
SM_KR = SM_DT + SSD_HEADS
SM_KRS = SM_KR + QK_ROPE
SM_N = SM_KRS + QK_ROPE

V7X_VMEM_LIMIT = 56 * 1024 * 1024
ROW_TILE = 256
MM_TM = 1280
MOE_TM = 256
SSD_SUB = 4
MOE_SEGMENTS = 4
MOE_TF = 512
MOE_TN = 2048
ATTN_TQ = 2048
ATTN_TK = 512


def _cparams(sem):
    return pltpu.CompilerParams(dimension_semantics=sem, vmem_limit_bytes=V7X_VMEM_LIMIT)


def _dot_nt(a, b):
    return lax.dot_general(a, b, (((1,), (1,)), ((), ())), preferred_element_type=F32)


def _rms(x, w):
    return x * lax.rsqrt(jnp.mean(x * x, axis=-1, keepdims=True) + EPS) * w


def _sigmoid(x):
    return 0.5 * jnp.tanh(0.5 * x) + 0.5


def _stacked_x_specs():
    n_p = T_PROMPT // ROW_TILE
    return [pl.BlockSpec((ROW_TILE, D_MODEL), lambda i: (jnp.minimum(i, n_p - 1), 0)),
            pl.BlockSpec((ROW_TILE, D_MODEL), lambda i: (0, 0))]


def _stacked_x(xp_ref, xs_ref):
    is_prompt = pl.program_id(0) < T_PROMPT // ROW_TILE
    return jnp.where(is_prompt, xp_ref[...], xs_ref[...])


def _rms_rows_body(xp_ref, xs_ref, w_ref, o_ref):
    o_ref[...] = _rms(_stacked_x(xp_ref, xs_ref), w_ref[...]).astype(o_ref.dtype)


def _rms_rows(x_p, x_s, w, out_dtype):
    d = D_MODEL
    return pl.pallas_call(
        _rms_rows_body,
        grid=(T_ALL // ROW_TILE,),
        in_specs=_stacked_x_specs() + [pl.BlockSpec((1, d), lambda i: (0, 0))],
        out_specs=pl.BlockSpec((ROW_TILE, d), lambda i: (i, 0)),
        out_shape=jax.ShapeDtypeStruct((T_ALL, d), out_dtype),
        compiler_params=_cparams(("parallel",)),
        name="rms_rows",
    )(x_p, x_s, w.reshape(1, d))


def _final_body(h_ref, ya_ref, yb_ref, w_ref, o_ref):
    y = ya_ref[...].astype(F32) + yb_ref[...].astype(F32)
    o_ref[...] = _rms(h_ref[...] + y, w_ref[...])


def _final_norm(h, y_a, y_b, w):
    m, d = h.shape
    row = pl.BlockSpec((ROW_TILE, d), lambda i: (i, 0))
    return pl.pallas_call(
        _final_body,
        grid=(m // ROW_TILE,),
        in_specs=[row, row, row, pl.BlockSpec((1, d), lambda i: (0, 0))],
        out_specs=row,
        out_shape=jax.ShapeDtypeStruct((m, d), F32),
        compiler_params=_cparams(("parallel",)),
        name="final_norm",
    )(h, y_a, y_b, w.reshape(1, d))


def _mm_body(epilogue, a_ref, w_ref, *rest):
    o_ref = rest[-1]
    acc = jnp.dot(a_ref[...], w_ref[...], preferred_element_type=F32)
    if epilogue is not None:
        acc = epilogue(acc, *[r[...] for r in rest[:-1]])
    o_ref[...] = acc.astype(o_ref.dtype)


def _mm(a, w, *, tm, tn, out_dtype, extras=(), epilogue=None, name="mm"):
    m, k = a.shape
    n = w.shape[1]
    in_specs = [pl.BlockSpec((tm, k), lambda i, j: (i, 0)),
                pl.BlockSpec((k, tn), lambda i, j: (0, j))]
    args = [a, w]
    for arr, off in extras:
        in_specs.append(pl.BlockSpec((tm, tn), functools.partial(lambda i, j, off: (i, j + off), off=off)))
        args.append(arr)
    return pl.pallas_call(
        functools.partial(_mm_body, epilogue),
        grid=(m // tm, n // tn),
        in_specs=in_specs,
        out_specs=pl.BlockSpec((tm, tn), lambda i, j: (i, j)),
        out_shape=jax.ShapeDtypeStruct((m, n), out_dtype),
        compiler_params=_cparams(("parallel", "parallel")),
        name=name,
    )(*args)


def _mla_prep_body(s_ref, qw_ref, kvw_ref, cos_ref, sin_ref, cqn_ref, ckv_ref, ckvb_ref, kr_ref, krb_ref):
    cqn_ref[...] = _rms(s_ref[:, SM_CQ:SM_CQ + Q_LORA], qw_ref[...]).astype(BF16)
    c = _rms(s_ref[:, SM_CKV:SM_CKV + KV_LORA], kvw_ref[...])
    ckv_ref[...] = c
    ckvb_ref[...] = c.astype(BF16)
    kr = (s_ref[:, SM_KR:SM_KR + QK_ROPE] * cos_ref[...]
          + s_ref[:, SM_KRS:SM_KRS + QK_ROPE] * sin_ref[...])
    kr_ref[...] = kr
    krb_ref[...] = kr.astype(BF16)


def _mla_prep(small, q_norm_w, kv_norm_w, cos2, sin2):
    m = small.shape[0]
    row = lambda n: pl.BlockSpec((ROW_TILE, n), lambda i: (i, 0))
    vec = lambda n: pl.BlockSpec((1, n), lambda i: (0, 0))
    return pl.pallas_call(
        _mla_prep_body,
        grid=(m // ROW_TILE,),
        in_specs=[row(SM_N), vec(Q_LORA), vec(KV_LORA), row(QK_ROPE), row(QK_ROPE)],
        out_specs=[row(Q_LORA), row(KV_LORA), row(KV_LORA), row(QK_ROPE), row(QK_ROPE)],
        out_shape=[jax.ShapeDtypeStruct((m, Q_LORA), BF16),
                   jax.ShapeDtypeStruct((m, KV_LORA), F32),
                   jax.ShapeDtypeStruct((m, KV_LORA), BF16),
                   jax.ShapeDtypeStruct((m, QK_ROPE), F32),
                   jax.ShapeDtypeStruct((m, QK_ROPE), BF16)],
        compiler_params=_cparams(("parallel",)),
        name="mla_prep",
    )(small, q_norm_w.reshape(1, -1), kv_norm_w.reshape(1, -1), cos2, sin2)


def _q_rope_body(a_ref, w_ref, ws_ref, cos_ref, sin_ref, o_ref):
    reps = o_ref.shape[1] // cos_ref.shape[1]
    a = a_ref[...]
    r = jnp.dot(a, w_ref[...], preferred_element_type=F32)
    rs = jnp.dot(a, ws_ref[...], preferred_element_type=F32)
    cos = jnp.tile(cos_ref[...], (1, reps))
    sin = jnp.tile(sin_ref[...], (1, reps))
    o_ref[...] = ((r * cos + rs * sin) * QSCALE).astype(o_ref.dtype)


def _q_rope(cqn, w_r, w_rs, cos128, sin128, *, tm, tn):
    m, k = cqn.shape
    n = w_r.shape[1]
    return pl.pallas_call(
        _q_rope_body,
        grid=(m // tm, n // tn),
        in_specs=[pl.BlockSpec((tm, k), lambda i, j: (i, 0)),
                  pl.BlockSpec((k, tn), lambda i, j: (0, j)),
                  pl.BlockSpec((k, tn), lambda i, j: (0, j)),
                  pl.BlockSpec((tm, 128), lambda i, j: (i, 0)),
                  pl.BlockSpec((tm, 128), lambda i, j: (i, 0))],
        out_specs=pl.BlockSpec((tm, tn), lambda i, j: (i, j)),
        out_shape=jax.ShapeDtypeStruct((m, n), BF16),
        compiler_params=_cparams(("parallel", "parallel")),
        name="q_rope",
    )(cqn, w_r, w_rs, cos128, sin128)


def _mm_t_body(w_ref, a_ref, o_ref):
    o_ref[0] = _dot_nt(w_ref[...], a_ref[...]).astype(o_ref.dtype)


def _mm_t(w_t, a, *, rows, tm, tn, out_dtype, name):
    n, k = w_t.shape
    return pl.pallas_call(
        _mm_t_body,
        grid=(rows // tm, n // tn),
        in_specs=[pl.BlockSpec((tn, k), lambda i, j: (j, 0)),
                  pl.BlockSpec((tm, k), lambda i, j: (i, 0))],
        out_specs=pl.BlockSpec((1, tn, tm), lambda i, j: (i, j, 0)),
        out_shape=jax.ShapeDtypeStruct((rows // tm, n, tm), out_dtype),
        compiler_params=_cparams(("parallel", "parallel")),
        name=name,
    )(w_t, a)


ATTN_KP = 256


ATTN_VP = 144


def _attn_prompt_body(qn_ref, qr_ref, kn_ref, kr_ref, vt_ref, o_ref, kf_sc, vf_sc, qf_sc, acc_sc):
    qi = pl.program_id(2)
    tq = qn_ref.shape[0]
    tk = vt_ref.shape[2]
    seq = kn_ref.shape[0]
    pad = ATTN_KP - QK_NOPE - QK_ROPE

    @pl.when(qi == 0)
    def _():
        def fill(i, c):
            r = pl.ds(pl.multiple_of(i * tk, tk), tk)
            kr = kr_ref[r, :]
            for hh in range(2):
                c0 = hh * ATTN_KP
                kf_sc[r, c0:c0 + QK_NOPE] = kn_ref[r, hh * QK_NOPE:(hh + 1) * QK_NOPE]
                kf_sc[r, c0 + QK_NOPE:c0 + QK_NOPE + QK_ROPE] = kr
                kf_sc[r, c0 + QK_NOPE + QK_ROPE:c0 + ATTN_KP] = jnp.zeros((tk, pad), BF16)
                vf_sc[i, hh * ATTN_VP:hh * ATTN_VP + V_HEAD, :] = vt_ref[i, hh * V_HEAD:(hh + 1) * V_HEAD, :]
                vf_sc[i, hh * ATTN_VP + V_HEAD:(hh + 1) * ATTN_VP, :] = jnp.ones((ATTN_VP - V_HEAD, tk), BF16)
            return c
        lax.fori_loop(0, seq // tk, fill, 0)

    for hh in range(2):
        c0 = hh * ATTN_KP
        qf_sc[:, c0:c0 + QK_NOPE] = qn_ref[:, hh * QK_NOPE:(hh + 1) * QK_NOPE]
        qf_sc[:, c0 + QK_NOPE:c0 + QK_NOPE + QK_ROPE] = qr_ref[:, hh * QK_ROPE:(hh + 1) * QK_ROPE]
        qf_sc[:, c0 + QK_NOPE + QK_ROPE:c0 + ATTN_KP] = jnp.zeros((tq, pad), BF16)
    acc_sc[...] = jnp.zeros(acc_sc.shape, F32)

    def block(kb, carry, diag):
        r = pl.ds(pl.multiple_of(kb * tk, tk), tk)
        q0 = 0 if diag is None else diag * tk
        nq_cols = tq - q0
        out = []
        for hh in range(2):
            m = carry[hh][:, q0:]
            st = _dot_nt(kf_sc[r, hh * ATTN_KP:(hh + 1) * ATTN_KP], qf_sc[q0:, hh * ATTN_KP:(hh + 1) * ATTN_KP])
            if diag is not None:
                kc = lax.broadcasted_iota(jnp.int32, (tk, nq_cols), 0) // CHUNK
                qc = lax.broadcasted_iota(jnp.int32, (tk, nq_cols), 1) // CHUNK
                st = jnp.where(kc <= qc, st, -jnp.inf)
            m_new = jnp.maximum(m, jnp.max(st, axis=0, keepdims=True))
            alpha = jnp.exp2(m - m_new)
            pt = jnp.exp2(st - m_new).astype(BF16)
            vt = vf_sc[kb, hh * ATTN_VP:(hh + 1) * ATTN_VP, :]
            acc_sc[hh, :, q0:] = alpha * acc_sc[hh, :, q0:] + jnp.dot(vt, pt, preferred_element_type=F32)
            out.append(m_new if q0 == 0 else jnp.concatenate([carry[hh][:, :q0], m_new], axis=1))
        return tuple(out)

    n_diag = tq // tk
    n_full = qi * n_diag
    init = (jnp.full((1, tq), -jnp.inf, F32),) * 2
    carry = lax.fori_loop(0, n_full, lambda kb, c: block(kb, c, None), init)
    for d in range(n_diag):
        carry = block(n_full + d, carry, d)
    for hh in range(2):
        o = acc_sc[hh, 0:V_HEAD, :] / acc_sc[hh, V_HEAD:V_HEAD + 1, :]
        o_ref[:, hh * V_HEAD:(hh + 1) * V_HEAD] = o.T.astype(o_ref.dtype)


def _attn_prompt(qn, qr, kn, krb, vt, *, nb, seq, heads, tq):
    tk = vt.shape[2]
    nq = seq // tq
    nk = seq // tk
    return pl.pallas_call(
        _attn_prompt_body,
        grid=(nb, heads // 2, nq),
        in_specs=[pl.BlockSpec((tq, 2 * QK_NOPE), lambda b, hp, qi: (b * nq + qi, hp)),
                  pl.BlockSpec((tq, 2 * QK_ROPE), lambda b, hp, qi: (b * nq + qi, hp)),
                  pl.BlockSpec((seq, 2 * QK_NOPE), lambda b, hp, qi: (b, hp)),
                  pl.BlockSpec((seq, QK_ROPE), lambda b, hp, qi: (b, 0)),
                  pl.BlockSpec((nk, 2 * V_HEAD, tk), lambda b, hp, qi: (b, hp, 0))],
        out_specs=pl.BlockSpec((tq, 2 * V_HEAD), lambda b, hp, qi: (b * nq + qi, hp)),
        out_shape=jax.ShapeDtypeStruct((nb * seq, heads * V_HEAD), BF16),
        scratch_shapes=[pltpu.VMEM((seq, 2 * ATTN_KP), BF16), pltpu.VMEM((nk, 2 * ATTN_VP, tk), BF16),
                        pltpu.VMEM((tq, 2 * ATTN_KP), BF16), pltpu.VMEM((2, ATTN_VP, tq), F32)],
        compiler_params=_cparams(("parallel", "parallel", "arbitrary")),
        name="attn_prompt",
    )(qn, qr, kn, krb, vt)


def _bmm_body(a_ref, w_ref, o_ref):
    o_ref[0] = jnp.dot(a_ref[0], w_ref[0], preferred_element_type=F32).astype(o_ref.dtype)


def _bmm(a, w, out_dtype, name):
    h, m, k = a.shape
    n = w.shape[2]
    return pl.pallas_call(
        _bmm_body,
        grid=(h,),
        in_specs=[pl.BlockSpec((1, m, k), lambda i: (i, 0, 0)),
                  pl.BlockSpec((1, k, n), lambda i: (i, 0, 0))],
        out_specs=pl.BlockSpec((1, m, n), lambda i: (i, 0, 0)),
        out_shape=jax.ShapeDtypeStruct((h, m, n), out_dtype),
        compiler_params=_cparams(("parallel",)),
        name=name,
    )(a, w)


def _attn_sample_body(ql_ref, qr_ref, cc_ref, ck_ref, nc_ref, nk_ref, o_ref, ccb_sc, ckb_sc):
    @pl.when(pl.program_id(1) == 0)
    def _():
        ccb_sc[...] = cc_ref[0].astype(BF16)
        ckb_sc[...] = ck_ref[0].astype(BF16)

    ql = ql_ref[0]
    qr = qr_ref[0]
    ncb = nc_ref[...].astype(BF16)
    nkb = nk_ref[...].astype(BF16)
    s1 = _dot_nt(ql, ccb_sc[...]) + _dot_nt(qr, ckb_sc[...])
    s2 = _dot_nt(ql, ncb) + _dot_nt(qr, nkb)
    m = jnp.maximum(jnp.max(s1, axis=1, keepdims=True), jnp.max(s2, axis=1, keepdims=True))
    p1 = jnp.exp2(s1 - m)
    p2 = jnp.exp2(s2 - m)
    l = jnp.sum(p1, axis=1, keepdims=True) + jnp.sum(p2, axis=1, keepdims=True)
    o = (jnp.dot(p1.astype(BF16), ccb_sc[...], preferred_element_type=F32)
         + jnp.dot(p2.astype(BF16), ncb, preferred_element_type=F32))
    o_ref[0] = (o / l).astype(o_ref.dtype)


def _attn_sample(q_lat, q_rope, cache_ckv, cache_krope, ckv, krope):
    rows = q_lat.shape[1]
    tr = 256
    new_blk = T_PROMPT // DEC_SEQ
    return pl.pallas_call(
        _attn_sample_body,
        grid=(DEC_BATCH, rows // tr),
        in_specs=[pl.BlockSpec((1, tr, KV_LORA), lambda b, r: (b, r, 0)),
                  pl.BlockSpec((1, tr, QK_ROPE), lambda b, r: (b, r, 0)),
                  pl.BlockSpec((1, PAST_LEN, KV_LORA), lambda b, r: (b, 0, 0)),
                  pl.BlockSpec((1, PAST_LEN, QK_ROPE), lambda b, r: (b, 0, 0)),
                  pl.BlockSpec((DEC_SEQ, KV_LORA), lambda b, r: (new_blk + b, 0)),
                  pl.BlockSpec((DEC_SEQ, QK_ROPE), lambda b, r: (new_blk + b, 0))],
        out_specs=pl.BlockSpec((1, tr, KV_LORA), lambda b, r: (b, r, 0)),
        out_shape=jax.ShapeDtypeStruct((DEC_BATCH, rows, KV_LORA), BF16),
        scratch_shapes=[pltpu.VMEM((PAST_LEN, KV_LORA), BF16), pltpu.VMEM((PAST_LEN, QK_ROPE), BF16)],
        compiler_params=_cparams(("parallel", "arbitrary")),
        name="attn_sample",
    )(q_lat, q_rope, cache_ckv, cache_krope, ckv, krope)


def _conv_body(x_ref, prev_ref, w_ref, b_ref, o_ref, ext_sc):
    tl = x_ref.shape[0]

    @pl.when(pl.program_id(2) == 0)
    def _():
        ext_sc[0:8, :] = prev_ref[0]

    @pl.when(pl.program_id(2) != 0)
    def _():
        ext_sc[0:8, :] = ext_sc[tl:tl + 8, :]

    ext_sc[8:8 + tl, :] = x_ref[...]
    acc = b_ref[...] + ext_sc[5:5 + tl, :] * w_ref[0:1, :]
    for k in range(1, SSD_CONV):
        acc = acc + ext_sc[5 + k:5 + k + tl, :] * w_ref[k:k + 1, :]
    o_ref[...] = (acc * _sigmoid(acc)).astype(o_ref.dtype)


def _conv_silu(big, prev8, conv_w, conv_b, *, nb, seq, tl, row_off):
    tc = 1024
    nrt = seq // tl
    rb0 = row_off // tl
    cb0 = BIG_XBC // tc
    return pl.pallas_call(
        _conv_body,
        grid=(nb, SSD_CONV_DIM // tc, nrt),
        in_specs=[pl.BlockSpec((tl, tc), lambda b, j, r: (rb0 + b * nrt + r, cb0 + j)),
                  pl.BlockSpec((1, 8, tc), lambda b, j, r: (b, 0, j)),
                  pl.BlockSpec((SSD_CONV, tc), lambda b, j, r: (0, j)),
                  pl.BlockSpec((1, tc), lambda b, j, r: (0, j))],
        out_specs=pl.BlockSpec((tl, tc), lambda b, j, r: (b * nrt + r, j)),
        out_shape=jax.ShapeDtypeStruct((nb * seq, SSD_CONV_DIM), BF16),
        scratch_shapes=[pltpu.VMEM((tl + 8, tc), F32)],
        compiler_params=_cparams(("parallel", "parallel", "arbitrary")),
        name="conv_silu",
    )(big, prev8, conv_w, conv_b.reshape(1, -1))


def _softplus(x):
    return jnp.maximum(x, 0.0) + jnp.log(1.0 + jnp.exp(-jnp.abs(x)))


def _dot_sel(sel, a, *, sel_left):
    hi = a.astype(BF16)
    r1 = a - hi.astype(F32)
    mid = r1.astype(BF16)
    lo = (r1 - mid.astype(F32)).astype(BF16)
    out = None
    for term in (hi, mid, lo):
        d = (jnp.dot(sel, term, preferred_element_type=F32) if sel_left
             else jnp.dot(term, sel, preferred_element_type=F32))
        out = d if out is None else out + d
    return out


def _ssd_body(has_h0, x_ref, b_ref, c_ref, z_ref, dt_ref, bias_ref, al_ref, dsk_ref, nw_ref, *rest):
    if has_h0:
        h0_ref, y_ref, hout_ref, ht_sc = rest
    else:
        y_ref, hout_ref, ht_sc = rest
    R, P = SSD_RANK, SSD_HEADDIM
    lc = P
    W = 2 * P
    c_idx = pl.program_id(2)

    @pl.when(c_idx == 0)
    def _():
        if has_h0:
            ht_sc[...] = h0_ref[0].T
        else:
            ht_sc[...] = jnp.zeros(ht_sc.shape, F32)

    ri = lax.broadcasted_iota(jnp.int32, (lc, lc), 0)
    ci = lax.broadcasted_iota(jnp.int32, (lc, lc), 1)
    tril_b = jnp.where(ri >= ci, 1.0, 0.0).astype(BF16)
    er = lax.broadcasted_iota(jnp.int32, (R, R * P), 0)
    ec = lax.broadcasted_iota(jnp.int32, (R, R * P), 1) // P
    expand = jnp.where(er == ec, 1.0, 0.0).astype(BF16)
    li = lax.broadcasted_iota(jnp.int32, (lc, R * P), 0)
    si = lax.broadcasted_iota(jnp.int32, (lc, R * P), 1) % P
    first_head = lax.broadcasted_iota(jnp.int32, (lc, W), 1) < P
    neg_a = -jnp.exp(al_ref[0])

    ht = ht_sc[...]
    for k in range(x_ref.shape[0] // lc):
        rows = slice(k * lc, (k + 1) * lc)
        dt = _softplus(dt_ref[0, rows, :] + bias_ref[0])
        a_cs = _dot_sel(tril_b, dt * neg_a, sel_left=True)
        a_exp = _dot_sel(expand, a_cs, sel_left=False)
        dt_exp = _dot_sel(expand, dt, sel_left=False)
        a_last = a_exp[lc - 1:lc, :]
        a_key = jnp.sum(jnp.where(li == si, a_exp, 0.0), axis=0, keepdims=True)
        decay = jnp.exp(jnp.where(li >= si, a_exp - a_key, -jnp.inf))

        x = x_ref[rows, :].astype(F32)
        xdt = x * dt_exp
        bm = b_ref[rows, :].astype(BF16)
        cm = c_ref[rows, :].astype(BF16)
        cb2 = _dot_nt(cm, jnp.concatenate([bm, bm], axis=0))
        m_all = (jnp.tile(cb2, (1, R // 2)) * decay).astype(BF16)
        y_off = jnp.dot(cm, ht.astype(BF16), preferred_element_type=F32) * jnp.exp(a_exp)
        xdt_b = xdt.astype(BF16)
        y_diag = []
        for j in range(R // 2):
            xp = xdt_b[:, j * W:(j + 1) * W]
            zero = jnp.zeros_like(xp)
            stack = jnp.concatenate([jnp.where(first_head, xp, zero), jnp.where(first_head, zero, xp)], axis=0)
            y_diag.append(jnp.dot(m_all[:, j * W:(j + 1) * W], stack, preferred_element_type=F32))
        wgt = (xdt * jnp.exp(a_last - a_exp)).astype(BF16)
        states_t = lax.dot_general(bm, wgt, (((0,), (0,)), ((), ())), preferred_element_type=F32)
        ht = ht * jnp.exp(a_last) + states_t

        y = jnp.concatenate(y_diag, axis=1) + y_off + x * dsk_ref[...]
        z = z_ref[rows, :]
        y = y * (z * _sigmoid(z))
        y = y * lax.rsqrt(jnp.mean(y * y, axis=-1, keepdims=True) + EPS) * nw_ref[...]
        y_ref[rows, :] = y.astype(y_ref.dtype)
    ht_sc[...] = ht

    @pl.when(c_idx == pl.num_programs(2) - 1)
    def _():
        hout_ref[0] = ht.T


def _ssd(xbc, zsrc, dt_g, h0, dt_bias, a_log, d_skip, ssd_norm_w, *, nb, seq, sub):
    lc = sub * SSD_HEADDIM
    nc = seq // lc
    G, R, GC, N = SSD_GROUPS, SSD_RANK, SSD_GCOLS, SSD_STATE
    row = lambda b, g, c: b * nc + c
    in_specs = [
        pl.BlockSpec((lc, GC), lambda b, g, c: (row(b, g, c), g)),
        pl.BlockSpec((lc, N), lambda b, g, c: (row(b, g, c), SSD_INNER // N + g)),
        pl.BlockSpec((lc, N), lambda b, g, c: (row(b, g, c), SSD_INNER // N + G + g)),
        pl.BlockSpec((lc, GC), lambda b, g, c: (row(b, g, c), g)),
        pl.BlockSpec((1, lc, R), lambda b, g, c: (g, row(b, g, c), 0)),
        pl.BlockSpec((1, 1, R), lambda b, g, c: (g, 0, 0)),
        pl.BlockSpec((1, 1, R), lambda b, g, c: (g, 0, 0)),
        pl.BlockSpec((1, GC), lambda b, g, c: (0, g)),
        pl.BlockSpec((1, GC), lambda b, g, c: (0, g)),
    ]
    args = [xbc, xbc, xbc, zsrc, dt_g, dt_bias.reshape(G, 1, R), a_log.reshape(G, 1, R),
            jnp.repeat(d_skip, SSD_HEADDIM).reshape(1, SSD_INNER), ssd_norm_w.reshape(1, SSD_INNER)]
    if h0 is not None:
        in_specs.append(pl.BlockSpec((1, GC, N), lambda b, g, c: (b, g, 0)))
        args.append(h0)
    return pl.pallas_call(
        functools.partial(_ssd_body, h0 is not None),
        grid=(nb, G, nc),
        in_specs=in_specs,
        out_specs=[pl.BlockSpec((lc, GC), lambda b, g, c: (row(b, g, c), g)),
                   pl.BlockSpec((1, GC, N), lambda b, g, c: (b, g, 0))],
        out_shape=[jax.ShapeDtypeStruct((nb * seq, SSD_INNER), BF16),
                   jax.ShapeDtypeStruct((nb, SSD_HEADS * SSD_HEADDIM, N), F32)],
        scratch_shapes=[pltpu.VMEM((N, GC), F32)],
        compiler_params=_cparams(("parallel", "parallel", "arbitrary")),
        name="ssd",
    )(*args)


def _norm_router_body(xp_ref, xs_ref, mix_ref, w_ref, wr_ref, br_ref, h_ref, xn_ref, lg_ref):
    h = _stacked_x(xp_ref, xs_ref) + mix_ref[...]
    h_ref[...] = h
    xn = _rms(h, w_ref[...])
    xn_ref[...] = xn.astype(BF16)
    lg_ref[...] = jnp.dot(xn, wr_ref[...], precision=HI, preferred_element_type=F32) + br_ref[...]


def _norm_router(x_p, x_s, mix, norm_w, w_router, b_router):
    m, d = mix.shape
    n = w_router.shape[1]
    row = pl.BlockSpec((ROW_TILE, d), lambda i: (i, 0))
    return pl.pallas_call(
        _norm_router_body,
        grid=(m // ROW_TILE,),
        in_specs=_stacked_x_specs() + [row,
                                       pl.BlockSpec((1, d), lambda i: (0, 0)),
                                       pl.BlockSpec((d, n), lambda i: (0, 0)),
                                       pl.BlockSpec((1, n), lambda i: (0, 0))],
        out_specs=[row, row, pl.BlockSpec((ROW_TILE, n), lambda i: (i, 0))],
        out_shape=[jax.ShapeDtypeStruct((m, d), F32), jax.ShapeDtypeStruct((m, d), BF16),
                   jax.ShapeDtypeStruct((m, n), F32)],
        compiler_params=_cparams(("parallel",)),
        name="norm_router",
    )(x_p, x_s, mix, norm_w.reshape(1, d), w_router, b_router)


def _moe_up_body(be_ref, first_ref, nused_ref, x_ref, wg_ref, wu_ref, o_ref, wgb_sc, wub_sc):
    blk = pl.program_id(1)

    @pl.when(first_ref[blk] == 1)
    def _():
        wgb_sc[...] = wg_ref[0].astype(BF16)
        wub_sc[...] = wu_ref[0].astype(BF16)

    @pl.when(blk < nused_ref[0])
    def _():
        x = x_ref[...]
        g = jnp.dot(x, wgb_sc[...], preferred_element_type=F32)
        u = jnp.dot(x, wub_sc[...], preferred_element_type=F32)
        o_ref[...] = (g * _sigmoid(g) * u).astype(o_ref.dtype)

    @pl.when(blk >= nused_ref[0])
    def _():
        o_ref[...] = jnp.zeros(o_ref.shape, o_ref.dtype)


def _moe_up(be, first, nused, xs, w_gate, w_up):
    p, d = xs.shape
    nblk = p // MOE_TM
    grid_spec = pltpu.PrefetchScalarGridSpec(
        num_scalar_prefetch=3,
        grid=(D_EXPERT // MOE_TF, nblk),
        in_specs=[pl.BlockSpec((MOE_TM, d), lambda f, i, be, fi, nu: (i, 0)),
                  pl.BlockSpec((1, d, MOE_TF), lambda f, i, be, fi, nu: (be[i], 0, f)),
                  pl.BlockSpec((1, d, MOE_TF), lambda f, i, be, fi, nu: (be[i], 0, f))],
        out_specs=pl.BlockSpec((MOE_TM, MOE_TF), lambda f, i, be, fi, nu: (i, f)),
        scratch_shapes=[pltpu.VMEM((d, MOE_TF), BF16), pltpu.VMEM((d, MOE_TF), BF16)],
    )
    return pl.pallas_call(
        _moe_up_body,
        grid_spec=grid_spec,
        out_shape=jax.ShapeDtypeStruct((p, D_EXPERT), BF16),
        compiler_params=_cparams(("arbitrary", "arbitrary")),
        name="moe_up",
    )(be, first, nused, xs, w_gate, w_up)


def _moe_down_body(be_ref, first_ref, nused_ref, h_ref, wd_ref, rw_ref, o_ref, wdb_sc):
    blk = pl.program_id(1)

    @pl.when(first_ref[blk] == 1)
    def _():
        wdb_sc[...] = wd_ref[0].astype(BF16)

    @pl.when(blk < nused_ref[0])
    def _():
        o_ref[...] = (jnp.dot(h_ref[...], wdb_sc[...], preferred_element_type=F32) * rw_ref[...]).astype(o_ref.dtype)

    @pl.when(blk >= nused_ref[0])
    def _():
        o_ref[...] = jnp.zeros(o_ref.shape, o_ref.dtype)


def _moe_down(be, first, nused, hact, w_down, row_w):
    p, f = hact.shape
    d = w_down.shape[2]
    nblk = p // MOE_TM
    grid_spec = pltpu.PrefetchScalarGridSpec(
        num_scalar_prefetch=3,
        grid=(d // MOE_TN, nblk),
        in_specs=[pl.BlockSpec((MOE_TM, f), lambda n, i, be, fi, nu: (i, 0)),
                  pl.BlockSpec((1, f, MOE_TN), lambda n, i, be, fi, nu: (be[i], 0, n)),
                  pl.BlockSpec((MOE_TM, 1), lambda n, i, be, fi, nu: (i, 0))],
        out_specs=pl.BlockSpec((MOE_TM, MOE_TN), lambda n, i, be, fi, nu: (i, n)),
        scratch_shapes=[pltpu.VMEM((f, MOE_TN), BF16)],
    )
    return pl.pallas_call(
        _moe_down_body,
        grid_spec=grid_spec,
        out_shape=jax.ShapeDtypeStruct((p, d), BF16),
        compiler_params=_cparams(("arbitrary", "arbitrary")),
        name="moe_down",
    )(be, first, nused, hact, w_down, row_w)


def _route(logits):
    t = logits.shape[0]
    g_logits = logits[:, :N_GROUPS]
    g_sel = jnp.argmax(g_logits, axis=-1)
    g_w = jnp.max(jax.nn.softmax(g_logits, axis=-1), axis=-1)
    e_logits = logits[:, N_GROUPS:N_GROUPS + N_EXPERTS].reshape(t, N_GROUPS, EXPERTS_PER_GROUP)
    e_in = jnp.take_along_axis(e_logits, g_sel[:, None, None], axis=1)[:, 0]
    e_val, e_idx = lax.top_k(e_in, TOP_K)
    e_w = jax.nn.softmax(e_val, axis=-1) * g_w[:, None]
    expert_id = (g_sel[:, None] * EXPERTS_PER_GROUP + e_idx).reshape(-1).astype(jnp.int32)
    a = t * TOP_K
    e_sorted, order = lax.sort((expert_id, jnp.arange(a, dtype=jnp.int32)), num_keys=1, is_stable=True)
    experts = jnp.arange(N_EXPERTS, dtype=jnp.int32)
    start = jnp.searchsorted(e_sorted, experts, side='left').astype(jnp.int32)
    counts = jnp.searchsorted(e_sorted, experts, side='right').astype(jnp.int32) - start
    padded = (counts + MOE_TM - 1) // MOE_TM * MOE_TM
    pad_end = jnp.cumsum(padded)
    pad_start = pad_end - padded
    shift = pad_start - start
    jumps = shift - jnp.concatenate([jnp.zeros((1,), jnp.int32), shift[:-1]])
    marks = jnp.zeros((a + 1,), jnp.int32).at[start].add(jumps)
    dest = jnp.arange(a, dtype=jnp.int32) + jnp.cumsum(marks)[:a]
    pos = lax.sort((order, dest), num_keys=1)[1].reshape(t, TOP_K)
    nblk = a // MOE_TM + N_EXPERTS
    nused = (pad_end[-1] // MOE_TM).astype(jnp.int32)
    blk = jnp.arange(nblk, dtype=jnp.int32)
    be = jnp.minimum(jnp.searchsorted(pad_end, blk * MOE_TM, side='right'), N_EXPERTS - 1).astype(jnp.int32)
    be = jnp.where(blk < nused, be, be[jnp.maximum(nused - 1, 0)])
    first = jnp.concatenate([jnp.ones((1,), jnp.int32), (be[1:] != be[:-1]).astype(jnp.int32)])
    per_row = lambda v: jnp.broadcast_to(v[be][:, None], (nblk, MOE_TM)).reshape(-1)
    rank = jnp.arange(nblk * MOE_TM, dtype=jnp.int32) - per_row(pad_start)
    valid = rank < per_row(counts)
    src = jnp.clip(per_row(start) + rank, 0, a - 1)
    row_token = jnp.where(valid, order[src] // TOP_K, 0)
    row_w = jnp.where(valid, e_w.reshape(-1)[order[src]], 0.0)
    return row_token, row_w, pos, be, first, nused.reshape(1)


def _rope_tables():
    half = QK_ROPE // 2
    inv_freq = ROPE_THETA ** (-jnp.arange(half, dtype=F32) / half)
    pos = jnp.concatenate([jnp.tile(jnp.arange(SEQ), BATCH),
                           jnp.tile(PAST_LEN + jnp.arange(DEC_SEQ), DEC_BATCH)]).astype(F32)
    ang = pos[:, None] * inv_freq[None, :]
    cos, sin = jnp.cos(ang), jnp.sin(ang)
    cos2 = jnp.concatenate([cos, cos], axis=1)
    sin2 = jnp.concatenate([-sin, sin], axis=1)
    return cos2, sin2


def kernel(x_prompt, x_sample, cache_ckv, cache_krope, state_conv, state_ssm, norm1_w, w_in, q_norm_w, kv_norm_w, w_uq, w_ukv, conv_w, conv_b, dt_bias, a_log, d_skip, ssd_norm_w, w_mla_o, w_ssd_o, w_out, norm2_w, w_group, b_group, w_erouter, b_erouter, w_gate, w_up, w_down, final_norm_w):
    swap = np.concatenate([np.arange(QK_ROPE // 2, QK_ROPE), np.arange(QK_ROPE // 2)])
    x_p2 = x_prompt.reshape(T_PROMPT, D_MODEL)
    x_s2 = x_sample.reshape(T_SAMPLE, D_MODEL)
    cos2, sin2 = _rope_tables()
    cos128, sin128 = jnp.tile(cos2, (1, 2)), jnp.tile(sin2, (1, 2))

    wi = w_in[0]
    w_kr = wi[:, OFF_KR:OFF_KR + QK_ROPE]
    w_small = jnp.concatenate([wi[:, OFF_CQ:OFF_KR], wi[:, OFF_DT:OFF_DT + SSD_HEADS], w_kr, w_kr[:, swap]],
                              axis=1).astype(BF16)
    w_big = jnp.concatenate([wi[:, OFF_Z:OFF_DT], wi[:, OFF_GMLA:]], axis=1).astype(BF16)
    wq = w_uq[0].reshape(Q_LORA, MLA_HEADS, QK_NOPE + QK_ROPE)
    wq_nope = wq[:, :, :QK_NOPE].reshape(Q_LORA, MLA_HEADS * QK_NOPE).astype(BF16)
    wq_rope = wq[:, :, QK_NOPE:]
    wq_r = wq_rope.reshape(Q_LORA, MLA_HEADS * QK_ROPE).astype(BF16)
    wq_rs = wq_rope[:, :, swap].reshape(Q_LORA, MLA_HEADS * QK_ROPE).astype(BF16)
    wkv3 = w_ukv[0].reshape(KV_LORA, MLA_HEADS, QK_NOPE + V_HEAD)
    w_uk_all = wkv3[:, :, :QK_NOPE].reshape(KV_LORA, MLA_HEADS * QK_NOPE).astype(BF16)
    w_uv_all_t = jnp.transpose(wkv3[:, :, QK_NOPE:], (1, 2, 0)).reshape(MLA_HEADS * V_HEAD, KV_LORA).astype(BF16)
    w_uk_t = jnp.transpose(wkv3[:, :, :QK_NOPE], (1, 2, 0)).astype(BF16)
    w_uv_h = jnp.transpose(wkv3[:, :, QK_NOPE:], (1, 0, 2)).astype(BF16)

    u = _rms_rows(x_p2, x_s2, norm1_w[0], BF16)
    small = _mm(u, w_small, tm=MM_TM, tn=SM_N // 2, out_dtype=F32, name="proj_small")
    big = _mm(u, w_big, tm=MM_TM, tn=512, out_dtype=F32, name="proj_big")

    cqn, ckv, ckv_b, krope, krope_b = _mla_prep(small, q_norm_w[0], kv_norm_w[0], cos2, sin2)
    qn = _mm(cqn, wq_nope, tm=MM_TM, tn=1024, out_dtype=BF16, epilogue=lambda acc: acc * QSCALE, name="q_nope")
    qr = _q_rope(cqn, wq_r, wq_rs, cos128, sin128, tm=MM_TM, tn=1024)
    kn = _mm(ckv_b, w_uk_all, tm=MM_TM, tn=1024, out_dtype=BF16, name="k_up")
    vt = _mm_t(w_uv_all_t, ckv_b, rows=T_PROMPT, tm=ATTN_TK, tn=1024, out_dtype=BF16, name="v_up_t")
    o_mla_p = _attn_prompt(qn, qr, kn, krope_b, vt, nb=BATCH, seq=SEQ, heads=MLA_HEADS, tq=ATTN_TQ)

    qn_s = qn[T_PROMPT:].reshape(T_SAMPLE, MLA_HEADS, QK_NOPE).transpose(1, 0, 2)
    q_lat = _bmm(qn_s, w_uk_t, BF16, "q_absorb")
    q_lat = q_lat.reshape(MLA_HEADS, DEC_BATCH, DEC_SEQ, KV_LORA).transpose(1, 0, 2, 3)
    q_lat = q_lat.reshape(DEC_BATCH, MLA_HEADS * DEC_SEQ, KV_LORA)
    qr_s = qr[T_PROMPT:].reshape(DEC_BATCH, DEC_SEQ, MLA_HEADS, QK_ROPE).transpose(0, 2, 1, 3)
    qr_s = qr_s.reshape(DEC_BATCH, MLA_HEADS * DEC_SEQ, QK_ROPE)
    o_lat = _attn_sample(q_lat, qr_s, cache_ckv[0], cache_krope[0], ckv, krope)
    o_lat = o_lat.reshape(DEC_BATCH, MLA_HEADS, DEC_SEQ, KV_LORA).transpose(1, 0, 2, 3)
    o_lat = o_lat.reshape(MLA_HEADS, T_SAMPLE, KV_LORA)
    o_mla_s = _bmm(o_lat, w_uv_h, BF16, "v_absorb")
    o_mla_s = o_mla_s.transpose(1, 0, 2).reshape(T_SAMPLE, MLA_HEADS * V_HEAD)
    o_mla = jnp.concatenate([o_mla_p, o_mla_s], axis=0)

    prev_p = jnp.zeros((BATCH, 8, SSD_CONV_DIM), F32)
    prev_s = jnp.concatenate([jnp.zeros((DEC_BATCH, 8 - (SSD_CONV - 1), SSD_CONV_DIM), F32), state_conv[0]], axis=1)
    xbc_p = _conv_silu(big, prev_p, conv_w[0], conv_b[0], nb=BATCH, seq=SEQ, tl=512, row_off=0)
    xbc_s = _conv_silu(big, prev_s, conv_w[0], conv_b[0], nb=DEC_BATCH, seq=DEC_SEQ, tl=DEC_SEQ, row_off=T_PROMPT)
    dt_raw = small[:, SM_DT:SM_DT + SSD_HEADS]
    by_group = lambda d: d.reshape(d.shape[0], SSD_GROUPS, SSD_RANK).transpose(1, 0, 2)

    def pad_seq(a, fill):
        a = a.reshape(DEC_BATCH, DEC_SEQ, a.shape[-1])
        a = jnp.pad(a, ((0, 0), (0, CHUNK - DEC_SEQ), (0, 0)), constant_values=fill)
        return a.reshape(DEC_BATCH * CHUNK, a.shape[-1])

    h0_s = state_ssm[0].reshape(DEC_BATCH, SSD_HEADS * SSD_HEADDIM, SSD_STATE)
    y_p, ssm_p = _ssd(xbc_p, big, by_group(dt_raw[:T_PROMPT]), None, dt_bias[0], a_log[0], d_skip[0],
                      ssd_norm_w[0], nb=BATCH, seq=SEQ, sub=SSD_SUB)
    y_s, ssm_s = _ssd(pad_seq(xbc_s, 0.0), pad_seq(big[T_PROMPT:, BIG_Z:BIG_Z + SSD_INNER], 0.0),
                      by_group(pad_seq(dt_raw[T_PROMPT:], -jnp.inf)), h0_s, dt_bias[0], a_log[0], d_skip[0],
                      ssd_norm_w[0], nb=DEC_BATCH, seq=CHUNK, sub=1)
    y_s = y_s.reshape(DEC_BATCH, CHUNK, SSD_INNER)[:, :DEC_SEQ].reshape(T_SAMPLE, SSD_INNER)
    o_ssd = jnp.concatenate([y_p, y_s], axis=0)

    gate = lambda acc, g: _sigmoid(g) * acc
    m1 = _mm(o_mla, w_mla_o[0].astype(BF16), tm=640, tn=1024, out_dtype=F32,
             extras=[(big, BIG_GMLA // 1024)], epilogue=gate, name="mla_out")
    merged = _mm(o_ssd, w_ssd_o[0].astype(BF16), tm=640, tn=512, out_dtype=BF16,
                 extras=[(big, BIG_GSSD // 512), (m1, 0)],
                 epilogue=lambda acc, g, m: m + _sigmoid(g) * acc, name="ssd_out")
    mix = _mm(merged, w_out[0].astype(BF16), tm=640, tn=1024, out_dtype=F32, name="out_proj")

    n_r = 128
    w_router = jnp.concatenate([w_group[0], w_erouter[0],
                                jnp.zeros((D_MODEL, n_r - N_GROUPS - N_EXPERTS), F32)], axis=1)
    b_router = jnp.concatenate([b_group[0], b_erouter[0], jnp.zeros((n_r - N_GROUPS - N_EXPERTS,), F32)])
    h, xn, logits = _norm_router(x_p2, x_s2, mix, norm2_w[0], w_router, b_router.reshape(1, n_r))
    row_token, row_w, pos, be, first, nused = _route(logits)
    nblk = be.shape[0]
    hact = []
    for s in range(MOE_SEGMENTS):
        b0, b1 = s * nblk // MOE_SEGMENTS, (s + 1) * nblk // MOE_SEGMENTS
        xs = jnp.take(xn, row_token[b0 * MOE_TM:b1 * MOE_TM], axis=0)
        first_s = jnp.concatenate([jnp.ones((1,), jnp.int32), first[b0 + 1:b1]])
        hact.append(_moe_up(be[b0:b1], first_s, jnp.clip(nused - b0, 0, b1 - b0), xs, w_gate[0], w_up[0]))
    hact = jnp.concatenate(hact, axis=0)
    yb = _moe_down(be, first, nused, hact, w_down[0], row_w.reshape(-1, 1))
    y_all = _final_norm(h, jnp.take(yb, pos[:, 0], axis=0), jnp.take(yb, pos[:, 1], axis=0), final_norm_w)

    y_prompt = y_all[:T_PROMPT].reshape(BATCH, SEQ, D_MODEL)
    y_sample = y_all[T_PROMPT:].reshape(DEC_BATCH, DEC_SEQ, D_MODEL)
    ckv_p = ckv[:T_PROMPT].reshape(1, BATCH, SEQ, KV_LORA)
    ckv_s = ckv[T_PROMPT:].reshape(1, DEC_BATCH, DEC_SEQ, KV_LORA)
    kr_p = krope[:T_PROMPT].reshape(1, BATCH, SEQ, QK_ROPE)
    kr_s = krope[T_PROMPT:].reshape(1, DEC_BATCH, DEC_SEQ, QK_ROPE)
    tail = SSD_CONV - 1
    conv_p = jnp.stack([lax.slice(big, ((b + 1) * SEQ - tail, BIG_XBC), ((b + 1) * SEQ, BIG_XBC + SSD_CONV_DIM))
                        for b in range(BATCH)])[None]
    conv_s = jnp.stack([lax.slice(big, (T_PROMPT + (b + 1) * DEC_SEQ - tail, BIG_XBC),
                                  (T_PROMPT + (b + 1) * DEC_SEQ, BIG_XBC + SSD_CONV_DIM))
                        for b in range(DEC_BATCH)])[None]
    ssm_p = ssm_p.reshape(1, BATCH, SSD_HEADS, SSD_HEADDIM, SSD_STATE)
    ssm_s = ssm_s.reshape(1, DEC_BATCH, SSD_HEADS, SSD_HEADDIM, SSD_STATE)
    return (y_prompt, y_sample, ckv_p, kr_p, conv_p, ssm_p, ckv_s, kr_s, conv_s, ssm_s)
```

```python
import functools
import math

import numpy as np
import jax
import jax.numpy as jnp
from jax import lax
from jax.experimental import pallas as pl
from jax.experimental.pallas import tpu as pltpu

F32 = jnp.float32
BF16 = jnp.bfloat16
HI = lax.Precision.HIGHEST

D_MODEL = 4096
BATCH = 2
SEQ = 8192
DEC_BATCH = 8
DEC_SEQ = 32
PAST_LEN = 4096
CHUNK = 64
EPS = 1e-6
MLA_HEADS = 32
Q_LORA = 1024
KV_LORA = 512
QK_NOPE = 128
QK_ROPE = 64
V_HEAD = 128
ROPE_THETA = 10000.0
SCALE = (QK_NOPE + QK_ROPE) ** -0.5
QSCALE = SCALE * math.log2(math.e)
SSD_INNER = 2 * D_MODEL
SSD_HEADDIM = 64
SSD_HEADS = SSD_INNER // SSD_HEADDIM
SSD_STATE = 128
SSD_GROUPS = 8
SSD_RANK = SSD_HEADS // SSD_GROUPS
SSD_GCOLS = SSD_RANK * SSD_HEADDIM
SSD_CONV = 4
SSD_CONV_DIM = SSD_INNER + 2 * SSD_GROUPS * SSD_STATE
N_GROUPS = 8
EXPERTS_PER_GROUP = 8
N_EXPERTS = N_GROUPS * EXPERTS_PER_GROUP
TOP_K = 2
D_EXPERT = 1024

T_PROMPT = BATCH * SEQ
T_SAMPLE = DEC_BATCH * DEC_SEQ
T_ALL = T_PROMPT + T_SAMPLE

OFF_CQ = 0
OFF_CKV = OFF_CQ + Q_LORA
OFF_KR = OFF_CKV + KV_LORA
OFF_Z = OFF_KR + QK_ROPE
OFF_XBC = OFF_Z + SSD_INNER
OFF_DT = OFF_XBC + SSD_CONV_DIM
OFF_GMLA = OFF_DT + SSD_HEADS
OFF_GSSD = OFF_GMLA + D_MODEL
N_IN = OFF_GSSD + D_MODEL

BIG_Z = 0
BIG_XBC = SSD_INNER
BIG_GMLA = BIG_XBC + SSD_CONV_DIM
BIG_GSSD = BIG_GMLA + D_MODEL
BIG_N = BIG_GSSD + D_MODEL
SM_CQ = 0
SM_CKV = Q_LORA
SM_DT = SM_CKV + KV_LORA
SM_KR = SM_DT + SSD_HEADS
SM_KRS = SM_KR + QK_ROPE
SM_N = SM_KRS + QK_ROPE

V7X_VMEM_LIMIT = 56 * 1024 * 1024
ROW_TILE = 256
MM_TM = 1280
MOE_TM = 256
SSD_SUB = 4
MOE_SEGMENTS = 4
MOE_TF = 512
MOE_TN = 2048
ATTN_TQ = 2048
ATTN_TK = 512


def _cparams(sem):
    return pltpu.CompilerParams(dimension_semantics=sem, vmem_limit_bytes=V7X_VMEM_LIMIT)


def _dot_nt(a, b):
    return lax.dot_general(a, b, (((1,), (1,)), ((), ())), preferred_element_type=F32)


def _rms(x, w):
    return x * lax.rsqrt(jnp.mean(x * x, axis=-1, keepdims=True) + EPS) * w


def _sigmoid(x):
    return 0.5 * jnp.tanh(0.5 * x) + 0.5


def _stacked_x_specs():
    n_p = T_PROMPT // ROW_TILE
    return [pl.BlockSpec((ROW_TILE, D_MODEL), lambda i: (jnp.minimum(i, n_p - 1), 0)),
            pl.BlockSpec((ROW_TILE, D_MODEL), lambda i: (0, 0))]


def _stacked_x(xp_ref, xs_ref):
    is_prompt = pl.program_id(0) < T_PROMPT // ROW_TILE
    return jnp.where(is_prompt, xp_ref[...], xs_ref[...])


def _rms_rows_body(xp_ref, xs_ref, w_ref, o_ref):
    o_ref[...] = _rms(_stacked_x(xp_ref, xs_ref), w_ref[...]).astype(o_ref.dtype)


def _rms_rows(x_p, x_s, w, out_dtype):
    d = D_MODEL
    return pl.pallas_call(
        _rms_rows_body,
        grid=(T_ALL // ROW_TILE,),
        in_specs=_stacked_x_specs() + [pl.BlockSpec((1, d), lambda i: (0, 0))],
        out_specs=pl.BlockSpec((ROW_TILE, d), lambda i: (i, 0)),
        out_shape=jax.ShapeDtypeStruct((T_ALL, d), out_dtype),
        compiler_params=_cparams(("parallel",)),
        name="rms_rows",
    )(x_p, x_s, w.reshape(1, d))


def _final_body(h_ref, ya_ref, yb_ref, w_ref, op_ref, os_ref):
    y = ya_ref[...].astype(F32) + yb_ref[...].astype(F32)
    out = _rms(h_ref[...] + y, w_ref[...])
    is_prompt = pl.program_id(0) < T_PROMPT // ROW_TILE

    @pl.when(is_prompt)
    def _():
        op_ref[...] = out

    @pl.when(jnp.logical_not(is_prompt))
    def _():
        os_ref[...] = out


def _final_norm(h, y_a, y_b, w):
    m, d = h.shape
    n_p = T_PROMPT // ROW_TILE
    row = pl.BlockSpec((ROW_TILE, d), lambda i: (i, 0))
    return pl.pallas_call(
        _final_body,
        grid=(m // ROW_TILE,),
        in_specs=[row, row, row, pl.BlockSpec((1, d), lambda i: (0, 0))],
        out_specs=[pl.BlockSpec((ROW_TILE, d), lambda i: (jnp.minimum(i, n_p - 1), 0)),
                   pl.BlockSpec((ROW_TILE, d), lambda i: (0, 0))],
        out_shape=[jax.ShapeDtypeStruct((T_PROMPT, d), F32), jax.ShapeDtypeStruct((T_SAMPLE, d), F32)],
        compiler_params=_cparams(("arbitrary",)),
        name="final_norm",
    )(h, y_a, y_b, w.reshape(1, d))


def _mm_body(epilogue, a_ref, w_ref, *rest):
    o_ref = rest[-1]
    acc = jnp.dot(a_ref[...], w_ref[...], preferred_element_type=F32)
    if epilogue is not None:
        acc = epilogue(acc, *[r[...] for r in rest[:-1]])
    o_ref[...] = acc.astype(o_ref.dtype)


def _mm(a, w, *, tm, tn, out_dtype, extras=(), epilogue=None, name="mm"):
    m, k = a.shape
    n = w.shape[1]
    in_specs = [pl.BlockSpec((tm, k), lambda i, j: (i, 0)),
                pl.BlockSpec((k, tn), lambda i, j: (0, j))]
    args = [a, w]
    for arr, off in extras:
        in_specs.append(pl.BlockSpec((tm, tn), functools.partial(lambda i, j, off: (i, j + off), off=off)))
        args.append(arr)
    return pl.pallas_call(
        functools.partial(_mm_body, epilogue),
        grid=(m // tm, n // tn),
        in_specs=in_specs,
        out_specs=pl.BlockSpec((tm, tn), lambda i, j: (i, j)),
        out_shape=jax.ShapeDtypeStruct((m, n), out_dtype),
        compiler_params=_cparams(("parallel", "parallel")),
        name=name,
    )(*args)


def _mla_prep_body(s_ref, qw_ref, kvw_ref, cos_ref, sin_ref, cqn_ref, ckv_ref, ckvb_ref, kr_ref, krb_ref):
    cqn_ref[...] = _rms(s_ref[:, SM_CQ:SM_CQ + Q_LORA], qw_ref[...]).astype(BF16)
    c = _rms(s_ref[:, SM_CKV:SM_CKV + KV_LORA], kvw_ref[...])
    ckv_ref[...] = c
    ckvb_ref[...] = c.astype(BF16)
    kr = (s_ref[:, SM_KR:SM_KR + QK_ROPE] * cos_ref[...]
          + s_ref[:, SM_KRS:SM_KRS + QK_ROPE] * sin_ref[...])
    kr_ref[...] = kr
    krb_ref[...] = kr.astype(BF16)


def _mla_prep(small, q_norm_w, kv_norm_w, cos2, sin2):
    m = small.shape[0]
    row = lambda n: pl.BlockSpec((ROW_TILE, n), lambda i: (i, 0))
    vec = lambda n: pl.BlockSpec((1, n), lambda i: (0, 0))
    return pl.pallas_call(
        _mla_prep_body,
        grid=(m // ROW_TILE,),
        in_specs=[row(SM_N), vec(Q_LORA), vec(KV_LORA), row(QK_ROPE), row(QK_ROPE)],
        out_specs=[row(Q_LORA), row(KV_LORA), row(KV_LORA), row(QK_ROPE), row(QK_ROPE)],
        out_shape=[jax.ShapeDtypeStruct((m, Q_LORA), BF16),
                   jax.ShapeDtypeStruct((m, KV_LORA), F32),
                   jax.ShapeDtypeStruct((m, KV_LORA), BF16),
                   jax.ShapeDtypeStruct((m, QK_ROPE), F32),
                   jax.ShapeDtypeStruct((m, QK_ROPE), BF16)],
        compiler_params=_cparams(("parallel",)),
        name="mla_prep",
    )(small, q_norm_w.reshape(1, -1), kv_norm_w.reshape(1, -1), cos2, sin2)


def _q_rope_body(a_ref, w_ref, ws_ref, cos_ref, sin_ref, o_ref):
    reps = o_ref.shape[1] // cos_ref.shape[1]
    a = a_ref[...]
    r = jnp.dot(a, w_ref[...], preferred_element_type=F32)
    rs = jnp.dot(a, ws_ref[...], preferred_element_type=F32)
    cos = jnp.tile(cos_ref[...], (1, reps))
    sin = jnp.tile(sin_ref[...], (1, reps))
    o_ref[...] = ((r * cos + rs * sin) * QSCALE).astype(o_ref.dtype)


def _q_rope(cqn, w_r, w_rs, cos128, sin128, *, tm, tn):
    m, k = cqn.shape
    n = w_r.shape[1]
    return pl.pallas_call(
        _q_rope_body,
        grid=(m // tm, n // tn),
        in_specs=[pl.BlockSpec((tm, k), lambda i, j: (i, 0)),
                  pl.BlockSpec((k, tn), lambda i, j: (0, j)),
                  pl.BlockSpec((k, tn), lambda i, j: (0, j)),
                  pl.BlockSpec((tm, 128), lambda i, j: (i, 0)),
                  pl.BlockSpec((tm, 128), lambda i, j: (i, 0))],
        out_specs=pl.BlockSpec((tm, tn), lambda i, j: (i, j)),
        out_shape=jax.ShapeDtypeStruct((m, n), BF16),
        compiler_params=_cparams(("parallel", "parallel")),
        name="q_rope",
    )(cqn, w_r, w_rs, cos128, sin128)


def _mm_t_body(w_ref, a_ref, o_ref):
    o_ref[0] = _dot_nt(w_ref[...], a_ref[...]).astype(o_ref.dtype)


def _mm_t(w_t, a, *, rows, tm, tn, out_dtype, name):
    n, k = w_t.shape
    return pl.pallas_call(
        _mm_t_body,
        grid=(rows // tm, n // tn),
        in_specs=[pl.BlockSpec((tn, k), lambda i, j: (j, 0)),
                  pl.BlockSpec((tm, k), lambda i, j: (i, 0))],
        out_specs=pl.BlockSpec((1, tn, tm), lambda i, j: (i, j, 0)),
        out_shape=jax.ShapeDtypeStruct((rows // tm, n, tm), out_dtype),
        compiler_params=_cparams(("parallel", "parallel")),
        name=name,
    )(w_t, a)


ATTN_KP = 256


ATTN_VP = 144


def _attn_prompt_body(qn_ref, qr_ref, kn_ref, kr_ref, vt_ref, base_ref, o_ref, kf_sc, vf_sc, qf_sc, acc_sc):
    qi = pl.program_id(2)
    tq = qn_ref.shape[0]
    tk = vt_ref.shape[2]
    seq = kn_ref.shape[0]
    pad = ATTN_KP - QK_NOPE - QK_ROPE

    @pl.when(qi == 0)
    def _():
        def fill(i, c):
            r = pl.ds(pl.multiple_of(i * tk, tk), tk)
            kr = kr_ref[r, :]
            for hh in range(2):
                c0 = hh * ATTN_KP
                kf_sc[r, c0:c0 + QK_NOPE] = kn_ref[r, hh * QK_NOPE:(hh + 1) * QK_NOPE]
                kf_sc[r, c0 + QK_NOPE:c0 + QK_NOPE + QK_ROPE] = kr
                kf_sc[r, c0 + QK_NOPE + QK_ROPE:c0 + ATTN_KP] = jnp.zeros((tk, pad), BF16)
                vf_sc[i, hh * ATTN_VP:hh * ATTN_VP + V_HEAD, :] = vt_ref[i, hh * V_HEAD:(hh + 1) * V_HEAD, :]
                vf_sc[i, hh * ATTN_VP + V_HEAD:(hh + 1) * ATTN_VP, :] = jnp.ones((ATTN_VP - V_HEAD, tk), BF16)
            return c
        lax.fori_loop(0, seq // tk, fill, 0)

    for hh in range(2):
        c0 = hh * ATTN_KP
        qf_sc[:, c0:c0 + QK_NOPE] = qn_ref[:, hh * QK_NOPE:(hh + 1) * QK_NOPE]
        qf_sc[:, c0 + QK_NOPE:c0 + QK_NOPE + QK_ROPE] = qr_ref[:, hh * QK_ROPE:(hh + 1) * QK_ROPE]
        qf_sc[:, c0 + QK_NOPE + QK_ROPE:c0 + ATTN_KP] = jnp.zeros((tq, pad), BF16)
    acc_sc[...] = jnp.zeros(acc_sc.shape, F32)

    def block(kb, carry, diag):
        r = pl.ds(pl.multiple_of(kb * tk, tk), tk)
        q0 = 0 if diag is None else diag * tk
        nq_cols = tq - q0
        out = []
        for hh in range(2):
            m = carry[hh][:, q0:]
            st = _dot_nt(kf_sc[r, hh * ATTN_KP:(hh + 1) * ATTN_KP], qf_sc[q0:, hh * ATTN_KP:(hh + 1) * ATTN_KP])
            if diag is not None:
                kc = lax.broadcasted_iota(jnp.int32, (tk, nq_cols), 0) // CHUNK
                qc = lax.broadcasted_iota(jnp.int32, (tk, nq_cols), 1) // CHUNK
                st = jnp.where(kc <= qc, st, -jnp.inf)
            m_new = jnp.maximum(m, jnp.max(st, axis=0, keepdims=True))
            alpha = jnp.exp2(m - m_new)
            pt = jnp.exp2(st - m_new).astype(BF16)
            vt = vf_sc[kb, hh * ATTN_VP:(hh + 1) * ATTN_VP, :]
            acc_sc[hh, :, q0:] = alpha * acc_sc[hh, :, q0:] + jnp.dot(vt, pt, preferred_element_type=F32)
            out.append(m_new if q0 == 0 else jnp.concatenate([carry[hh][:, :q0], m_new], axis=1))
        return tuple(out)

    n_diag = tq // tk
    n_full = qi * n_diag
    init = (jnp.full((1, tq), -jnp.inf, F32),) * 2
    carry = lax.fori_loop(0, n_full, lambda kb, c: block(kb, c, None), init)
    for d in range(n_diag):
        carry = block(n_full + d, carry, d)
    for hh in range(2):
        o = acc_sc[hh, 0:V_HEAD, :] / acc_sc[hh, V_HEAD:V_HEAD + 1, :]
        o_ref[:, hh * V_HEAD:(hh + 1) * V_HEAD] = o.T.astype(o_ref.dtype)


def _attn_prompt(qn, qr, kn, krb, vt, base, *, nb, seq, heads, tq):
    tk = vt.shape[2]
    nq = seq // tq
    nk = seq // tk
    return pl.pallas_call(
        _attn_prompt_body,
        grid=(nb, heads // 2, nq),
        in_specs=[pl.BlockSpec((tq, 2 * QK_NOPE), lambda b, hp, qi: (b * nq + qi, hp)),
                  pl.BlockSpec((tq, 2 * QK_ROPE), lambda b, hp, qi: (b * nq + qi, hp)),
                  pl.BlockSpec((seq, 2 * QK_NOPE), lambda b, hp, qi: (b, hp)),
                  pl.BlockSpec((seq, QK_ROPE), lambda b, hp, qi: (b, 0)),
                  pl.BlockSpec((nk, 2 * V_HEAD, tk), lambda b, hp, qi: (b, hp, 0)),
                  pl.BlockSpec(memory_space=pl.ANY)],
        out_specs=pl.BlockSpec((tq, 2 * V_HEAD), lambda b, hp, qi: (b * nq + qi, hp)),
        out_shape=jax.ShapeDtypeStruct(base.shape, base.dtype),
        input_output_aliases={5: 0},
        scratch_shapes=[pltpu.VMEM((seq, 2 * ATTN_KP), BF16), pltpu.VMEM((nk, 2 * ATTN_VP, tk), BF16),
                        pltpu.VMEM((tq, 2 * ATTN_KP), BF16), pltpu.VMEM((2, ATTN_VP, tq), F32)],
        compiler_params=_cparams(("parallel", "parallel", "arbitrary")),
        name="attn_prompt",
    )(qn, qr, kn, krb, vt, base)


def _bmm_body(a_ref, w_ref, o_ref):
    o_ref[0] = jnp.dot(a_ref[0], w_ref[0], preferred_element_type=F32).astype(o_ref.dtype)


def _bmm(a, w, out_dtype, name):
    h, m, k = a.shape
    n = w.shape[2]
    return pl.pallas_call(
        _bmm_body,
        grid=(h,),
        in_specs=[pl.BlockSpec((1, m, k), lambda i: (i, 0, 0)),
                  pl.BlockSpec((1, k, n), lambda i: (i, 0, 0))],
        out_specs=pl.BlockSpec((1, m, n), lambda i: (i, 0, 0)),
        out_shape=jax.ShapeDtypeStruct((h, m, n), out_dtype),
        compiler_params=_cparams(("parallel",)),
        name=name,
    )(a, w)


def _attn_sample_body(ql_ref, qr_ref, cc_ref, ck_ref, nc_ref, nk_ref, o_ref, ccb_sc, ckb_sc):
    @pl.when(pl.program_id(1) == 0)
    def _():
        ccb_sc[...] = cc_ref[0].astype(BF16)
        ckb_sc[...] = ck_ref[0].astype(BF16)

    ql = ql_ref[0]
    qr = qr_ref[0]
    ncb = nc_ref[...].astype(BF16)
    nkb = nk_ref[...].astype(BF16)
    s1 = _dot_nt(ql, ccb_sc[...]) + _dot_nt(qr, ckb_sc[...])
    s2 = _dot_nt(ql, ncb) + _dot_nt(qr, nkb)
    m = jnp.maximum(jnp.max(s1, axis=1, keepdims=True), jnp.max(s2, axis=1, keepdims=True))
    p1 = jnp.exp2(s1 - m)
    p2 = jnp.exp2(s2 - m)
    l = jnp.sum(p1, axis=1, keepdims=True) + jnp.sum(p2, axis=1, keepdims=True)
    o = (jnp.dot(p1.astype(BF16), ccb_sc[...], preferred_element_type=F32)
         + jnp.dot(p2.astype(BF16), ncb, preferred_element_type=F32))
    o_ref[0] = (o / l).astype(o_ref.dtype)


def _attn_sample(q_lat, q_rope, cache_ckv, cache_krope, ckv, krope):
    rows = q_lat.shape[1]
    tr = 256
    new_blk = T_PROMPT // DEC_SEQ
    return pl.pallas_call(
        _attn_sample_body,
        grid=(DEC_BATCH, rows // tr),
        in_specs=[pl.BlockSpec((1, tr, KV_LORA), lambda b, r: (b, r, 0)),
                  pl.BlockSpec((1, tr, QK_ROPE), lambda b, r: (b, r, 0)),
                  pl.BlockSpec((1, PAST_LEN, KV_LORA), lambda b, r: (b, 0, 0)),
                  pl.BlockSpec((1, PAST_LEN, QK_ROPE), lambda b, r: (b, 0, 0)),
                  pl.BlockSpec((DEC_SEQ, KV_LORA), lambda b, r: (new_blk + b, 0)),
                  pl.BlockSpec((DEC_SEQ, QK_ROPE), lambda b, r: (new_blk + b, 0))],
        out_specs=pl.BlockSpec((1, tr, KV_LORA), lambda b, r: (b, r, 0)),
        out_shape=jax.ShapeDtypeStruct((DEC_BATCH, rows, KV_LORA), BF16),
        scratch_shapes=[pltpu.VMEM((PAST_LEN, KV_LORA), BF16), pltpu.VMEM((PAST_LEN, QK_ROPE), BF16)],
        compiler_params=_cparams(("parallel", "arbitrary")),
        name="attn_sample",
    )(q_lat, q_rope, cache_ckv, cache_krope, ckv, krope)


def _conv_body(x_ref, prev_ref, w_ref, b_ref, o_ref, ext_sc):
    tl = x_ref.shape[0]

    @pl.when(pl.program_id(2) == 0)
    def _():
        ext_sc[0:8, :] = prev_ref[0]

    @pl.when(pl.program_id(2) != 0)
    def _():
        ext_sc[0:8, :] = ext_sc[tl:tl + 8, :]

    ext_sc[8:8 + tl, :] = x_ref[...]
    acc = b_ref[...] + ext_sc[5:5 + tl, :] * w_ref[0:1, :]
    for k in range(1, SSD_CONV):
        acc = acc + ext_sc[5 + k:5 + k + tl, :] * w_ref[k:k + 1, :]
    o_ref[...] = (acc * _sigmoid(acc)).astype(o_ref.dtype)


def _conv_silu(big, prev8, conv_w, conv_b, *, nb, seq, tl, row_off):
    tc = 1024
    nrt = seq // tl
    rb0 = row_off // tl
    cb0 = BIG_XBC // tc
    return pl.pallas_call(
        _conv_body,
        grid=(nb, SSD_CONV_DIM // tc, nrt),
        in_specs=[pl.BlockSpec((tl, tc), lambda b, j, r: (rb0 + b * nrt + r, cb0 + j)),
                  pl.BlockSpec((1, 8, tc), lambda b, j, r: (b, 0, j)),
                  pl.BlockSpec((SSD_CONV, tc), lambda b, j, r: (0, j)),
                  pl.BlockSpec((1, tc), lambda b, j, r: (0, j))],
        out_specs=pl.BlockSpec((tl, tc), lambda b, j, r: (b * nrt + r, j)),
        out_shape=jax.ShapeDtypeStruct((nb * seq, SSD_CONV_DIM), BF16),
        scratch_shapes=[pltpu.VMEM((tl + 8, tc), F32)],
        compiler_params=_cparams(("parallel", "parallel", "arbitrary")),
        name="conv_silu",
    )(big, prev8, conv_w, conv_b.reshape(1, -1))


def _softplus(x):
    return jnp.maximum(x, 0.0) + jnp.log(1.0 + jnp.exp(-jnp.abs(x)))


def _dot_sel(sel, a, *, sel_left):
    hi = a.astype(BF16)
    r1 = a - hi.astype(F32)
    mid = r1.astype(BF16)
    lo = (r1 - mid.astype(F32)).astype(BF16)
    out = None
    for term in (hi, mid, lo):
        d = (jnp.dot(sel, term, preferred_element_type=F32) if sel_left
             else jnp.dot(term, sel, preferred_element_type=F32))
        out = d if out is None else out + d
    return out


def _ssd_body(has_h0, x_ref, b_ref, c_ref, z_ref, dt_ref, bias_ref, al_ref, dsk_ref, nw_ref, *rest):
    h0_ref = rest[0] if has_h0 else None
    y_ref, hout_ref, ht_sc = rest[-3:]
    R, P = SSD_RANK, SSD_HEADDIM
    lc = P
    W = 2 * P
    c_idx = pl.program_id(2)

    @pl.when(c_idx == 0)
    def _():
        if has_h0:
            ht_sc[...] = h0_ref[0].T
        else:
            ht_sc[...] = jnp.zeros(ht_sc.shape, F32)

    ri = lax.broadcasted_iota(jnp.int32, (lc, lc), 0)
    ci = lax.broadcasted_iota(jnp.int32, (lc, lc), 1)
    tril_b = jnp.where(ri >= ci, 1.0, 0.0).astype(BF16)
    er = lax.broadcasted_iota(jnp.int32, (R, R * P), 0)
    ec = lax.broadcasted_iota(jnp.int32, (R, R * P), 1) // P
    expand = jnp.where(er == ec, 1.0, 0.0).astype(BF16)
    li = lax.broadcasted_iota(jnp.int32, (lc, R * P), 0)
    si = lax.broadcasted_iota(jnp.int32, (lc, R * P), 1) % P
    first_head = lax.broadcasted_iota(jnp.int32, (lc, W), 1) < P
    neg_a = -jnp.exp(al_ref[0])

    ht = ht_sc[...]
    for k in range(x_ref.shape[0] // lc):
        rows = slice(k * lc, (k + 1) * lc)
        dt = _softplus(dt_ref[0, rows, :] + bias_ref[0])
        a_cs = _dot_sel(tril_b, dt * neg_a, sel_left=True)
        a_exp = _dot_sel(expand, a_cs, sel_left=False)
        dt_exp = _dot_sel(expand, dt, sel_left=False)
        a_last = a_exp[lc - 1:lc, :]
        a_key = jnp.sum(jnp.where(li == si, a_exp, 0.0), axis=0, keepdims=True)
        decay = jnp.exp(jnp.where(li >= si, a_exp - a_key, -jnp.inf))

        x = x_ref[rows, :].astype(F32)
        xdt = x * dt_exp
        bm = b_ref[rows, :].astype(BF16)
        cm = c_ref[rows, :].astype(BF16)
        cb2 = _dot_nt(cm, jnp.concatenate([bm, bm], axis=0))
        m_all = (jnp.tile(cb2, (1, R // 2)) * decay).astype(BF16)
        y_off = jnp.dot(cm, ht.astype(BF16), preferred_element_type=F32) * jnp.exp(a_exp)
        xdt_b = xdt.astype(BF16)
        y_diag = []
        for j in range(R // 2):
            xp = xdt_b[:, j * W:(j + 1) * W]
            zero = jnp.zeros_like(xp)
            stack = jnp.concatenate([jnp.where(first_head, xp, zero), jnp.where(first_head, zero, xp)], axis=0)
            y_diag.append(jnp.dot(m_all[:, j * W:(j + 1) * W], stack, preferred_element_type=F32))
        wgt = (xdt * jnp.exp(a_last - a_exp)).astype(BF16)
        states_t = lax.dot_general(bm, wgt, (((0,), (0,)), ((), ())), preferred_element_type=F32)
        ht = ht * jnp.exp(a_last) + states_t

        y = jnp.concatenate(y_diag, axis=1) + y_off + x * dsk_ref[...]
        z = z_ref[rows, :]
        y = y * (z * _sigmoid(z))
        y = y * lax.rsqrt(jnp.mean(y * y, axis=-1, keepdims=True) + EPS) * nw_ref[...]
        y_ref[rows, :] = y.astype(y_ref.dtype)
    ht_sc[...] = ht

    @pl.when(c_idx == pl.num_programs(2) - 1)
    def _():
        hout_ref[0] = ht.T


def _ssd(xbc, zsrc, dt_g, h0, dt_bias, a_log, d_skip, ssd_norm_w, *, nb, seq, sub, base=None):
    lc = sub * SSD_HEADDIM
    nc = seq // lc
    G, R, GC, N = SSD_GROUPS, SSD_RANK, SSD_GCOLS, SSD_STATE
    row = lambda b, g, c: b * nc + c
    in_specs = [
        pl.BlockSpec((lc, GC), lambda b, g, c: (row(b, g, c), g)),
        pl.BlockSpec((lc, N), lambda b, g, c: (row(b, g, c), SSD_INNER // N + g)),
        pl.BlockSpec((lc, N), lambda b, g, c: (row(b, g, c), SSD_INNER // N + G + g)),
        pl.BlockSpec((lc, GC), lambda b, g, c: (row(b, g, c), g)),
        pl.BlockSpec((1, lc, R), lambda b, g, c: (g, row(b, g, c), 0)),
        pl.BlockSpec((1, 1, R), lambda b, g, c: (g, 0, 0)),
        pl.BlockSpec((1, 1, R), lambda b, g, c: (g, 0, 0)),
        pl.BlockSpec((1, GC), lambda b, g, c: (0, g)),
        pl.BlockSpec((1, GC), lambda b, g, c: (0, g)),
    ]
    args = [xbc, xbc, xbc, zsrc, dt_g, dt_bias.reshape(G, 1, R), a_log.reshape(G, 1, R),
            jnp.repeat(d_skip, SSD_HEADDIM).reshape(1, SSD_INNER), ssd_norm_w.reshape(1, SSD_INNER)]
    if h0 is not None:
        in_specs.append(pl.BlockSpec((1, GC, N), lambda b, g, c: (b, g, 0)))
        args.append(h0)
    aliases = {}
    y_shape = jax.ShapeDtypeStruct((nb * seq, SSD_INNER), BF16)
    if base is not None:
        aliases = {len(args): 0}
        in_specs.append(pl.BlockSpec(memory_space=pl.ANY))
        args.append(base)
        y_shape = jax.ShapeDtypeStruct(base.shape, base.dtype)
    return pl.pallas_call(
        functools.partial(_ssd_body, h0 is not None),
        grid=(nb, G, nc),
        in_specs=in_specs,
        out_specs=[pl.BlockSpec((lc, GC), lambda b, g, c: (row(b, g, c), g)),
                   pl.BlockSpec((1, GC, N), lambda b, g, c: (b, g, 0))],
        out_shape=[y_shape, jax.ShapeDtypeStruct((nb, SSD_HEADS * SSD_HEADDIM, N), F32)],
        input_output_aliases=aliases,
        scratch_shapes=[pltpu.VMEM((N, GC), F32)],
        compiler_params=_cparams(("parallel", "parallel", "arbitrary")),
        name="ssd",
    )(*args)


def _norm_router_body(xp_ref, xs_ref, mix_ref, w_ref, wr_ref, br_ref, h_ref, xn_ref, lg_ref):
    h = _stacked_x(xp_ref, xs_ref) + mix_ref[...]
    h_ref[...] = h
    xn = _rms(h, w_ref[...])
    xn_ref[...] = xn.astype(BF16)
    lg_ref[...] = jnp.dot(xn, wr_ref[...], precision=HI, preferred_element_type=F32) + br_ref[...]


def _norm_router(x_p, x_s, mix, norm_w, w_router, b_router):
    m, d = mix.shape
    n = w_router.shape[1]
    row = pl.BlockSpec((ROW_TILE, d), lambda i: (i, 0))
    return pl.pallas_call(
        _norm_router_body,
        grid=(m // ROW_TILE,),
        in_specs=_stacked_x_specs() + [row,
                                       pl.BlockSpec((1, d), lambda i: (0, 0)),
                                       pl.BlockSpec((d, n), lambda i: (0, 0)),
                                       pl.BlockSpec((1, n), lambda i: (0, 0))],
        out_specs=[row, row, pl.BlockSpec((ROW_TILE, n), lambda i: (i, 0))],
        out_shape=[jax.ShapeDtypeStruct((m, d), F32), jax.ShapeDtypeStruct((m, d), BF16),
                   jax.ShapeDtypeStruct((m, n), F32)],
        compiler_params=_cparams(("parallel",)),
        name="norm_router",
    )(x_p, x_s, mix, norm_w.reshape(1, d), w_router, b_router)


def _moe_up_body(be_ref, first_ref, nused_ref, x_ref, wg_ref, wu_ref, base_ref, o_ref, wgb_sc, wub_sc):
    del base_ref
    blk = pl.program_id(1)

    @pl.when(first_ref[blk] == 1)
    def _():
        wgb_sc[...] = wg_ref[0].astype(BF16)
        wub_sc[...] = wu_ref[0].astype(BF16)

    @pl.when(blk < nused_ref[0])
    def _():
        x = x_ref[...]
        g = jnp.dot(x, wgb_sc[...], preferred_element_type=F32)
        u = jnp.dot(x, wub_sc[...], preferred_element_type=F32)
        o_ref[...] = (g * _sigmoid(g) * u).astype(o_ref.dtype)

    @pl.when(blk >= nused_ref[0])
    def _():
        o_ref[...] = jnp.zeros(o_ref.shape, o_ref.dtype)


def _moe_up(be, first, nused, xs, w_gate, w_up, base, blk0):
    p, d = xs.shape
    nblk = p // MOE_TM
    grid_spec = pltpu.PrefetchScalarGridSpec(
        num_scalar_prefetch=3,
        grid=(D_EXPERT // MOE_TF, nblk),
        in_specs=[pl.BlockSpec((MOE_TM, d), lambda f, i, be, fi, nu: (i, 0)),
                  pl.BlockSpec((1, d, MOE_TF), lambda f, i, be, fi, nu: (be[i], 0, f)),
                  pl.BlockSpec((1, d, MOE_TF), lambda f, i, be, fi, nu: (be[i], 0, f)),
                  pl.BlockSpec(memory_space=pl.ANY)],
        out_specs=pl.BlockSpec((MOE_TM, MOE_TF), lambda f, i, be, fi, nu: (blk0 + i, f)),
        scratch_shapes=[pltpu.VMEM((d, MOE_TF), BF16), pltpu.VMEM((d, MOE_TF), BF16)],
    )
    return pl.pallas_call(
        _moe_up_body,
        grid_spec=grid_spec,
        out_shape=jax.ShapeDtypeStruct(base.shape, base.dtype),
        input_output_aliases={6: 0},
        compiler_params=_cparams(("arbitrary", "arbitrary")),
        name="moe_up",
    )(be, first, nused, xs, w_gate, w_up, base)


def _moe_down_body(be_ref, first_ref, nused_ref, h_ref, wd_ref, rw_ref, o_ref, wdb_sc):
    blk = pl.program_id(1)

    @pl.when(first_ref[blk] == 1)
    def _():
        wdb_sc[...] = wd_ref[0].astype(BF16)

    @pl.when(blk < nused_ref[0])
    def _():
        o_ref[...] = (jnp.dot(h_ref[...], wdb_sc[...], preferred_element_type=F32) * rw_ref[...]).astype(o_ref.dtype)

    @pl.when(blk >= nused_ref[0])
    def _():
        o_ref[...] = jnp.zeros(o_ref.shape, o_ref.dtype)


def _moe_down(be, first, nused, hact, w_down, row_w):
    p, f = hact.shape
    d = w_down.shape[2]
    nblk = p // MOE_TM
    grid_spec = pltpu.PrefetchScalarGridSpec(
        num_scalar_prefetch=3,
        grid=(d // MOE_TN, nblk),
        in_specs=[pl.BlockSpec((MOE_TM, f), lambda n, i, be, fi, nu: (i, 0)),
                  pl.BlockSpec((1, f, MOE_TN), lambda n, i, be, fi, nu: (be[i], 0, n)),
                  pl.BlockSpec((MOE_TM, 1), lambda n, i, be, fi, nu: (i, 0))],
        out_specs=pl.BlockSpec((MOE_TM, MOE_TN), lambda n, i, be, fi, nu: (i, n)),
        scratch_shapes=[pltpu.VMEM((f, MOE_TN), BF16)],
    )
    return pl.pallas_call(
        _moe_down_body,
        grid_spec=grid_spec,
        out_shape=jax.ShapeDtypeStruct((p, d), BF16),
        compiler_params=_cparams(("arbitrary", "arbitrary")),
        name="moe_down",
    )(be, first, nused, hact, w_down, row_w)


def _route(logits):
    t = logits.shape[0]
    g_logits = logits[:, :N_GROUPS]
    g_sel = jnp.argmax(g_logits, axis=-1)
    g_w = jnp.max(jax.nn.softmax(g_logits, axis=-1), axis=-1)
    e_logits = logits[:, N_GROUPS:N_GROUPS + N_EXPERTS].reshape(t, N_GROUPS, EXPERTS_PER_GROUP)
    e_in = jnp.take_along_axis(e_logits, g_sel[:, None, None], axis=1)[:, 0]
    e_val, e_idx = lax.top_k(e_in, TOP_K)
    e_w = jax.nn.softmax(e_val, axis=-1) * g_w[:, None]
    expert_id = (g_sel[:, None] * EXPERTS_PER_GROUP + e_idx).reshape(-1).astype(jnp.int32)
    a = t * TOP_K
    e_sorted, order = lax.sort((expert_id, jnp.arange(a, dtype=jnp.int32)), num_keys=1, is_stable=True)
    experts = jnp.arange(N_EXPERTS, dtype=jnp.int32)
    start = jnp.searchsorted(e_sorted, experts, side='left').astype(jnp.int32)
    counts = jnp.searchsorted(e_sorted, experts, side='right').astype(jnp.int32) - start
    padded = (counts + MOE_TM - 1) // MOE_TM * MOE_TM
    pad_end = jnp.cumsum(padded)
    pad_start = pad_end - padded
    shift = pad_start - start
    jumps = shift - jnp.concatenate([jnp.zeros((1,), jnp.int32), shift[:-1]])
    marks = jnp.zeros((a + 1,), jnp.int32).at[start].add(jumps)
    dest = jnp.arange(a, dtype=jnp.int32) + jnp.cumsum(marks)[:a]
    pos = lax.sort((order, dest), num_keys=1)[1].reshape(t, TOP_K)
    nblk = a // MOE_TM + N_EXPERTS
    nused = (pad_end[-1] // MOE_TM).astype(jnp.int32)
    blk = jnp.arange(nblk, dtype=jnp.int32)
    be = jnp.minimum(jnp.searchsorted(pad_end, blk * MOE_TM, side='right'), N_EXPERTS - 1).astype(jnp.int32)
    be = jnp.where(blk < nused, be, be[jnp.maximum(nused - 1, 0)])
    first = jnp.concatenate([jnp.ones((1,), jnp.int32), (be[1:] != be[:-1]).astype(jnp.int32)])
    per_row = lambda v: jnp.broadcast_to(v[be][:, None], (nblk, MOE_TM)).reshape(-1)
    rank = jnp.arange(nblk * MOE_TM, dtype=jnp.int32) - per_row(pad_start)
    valid = rank < per_row(counts)
    src = jnp.clip(per_row(start) + rank, 0, a - 1)
    row_token = jnp.where(valid, order[src] // TOP_K, 0)
    row_w = jnp.where(valid, e_w.reshape(-1)[order[src]], 0.0)
    return row_token, row_w, pos, be, first, nused.reshape(1)


def _rope_tables():
    half = QK_ROPE // 2
    inv_freq = ROPE_THETA ** (-jnp.arange(half, dtype=F32) / half)
    pos = jnp.concatenate([jnp.tile(jnp.arange(SEQ), BATCH),
                           jnp.tile(PAST_LEN + jnp.arange(DEC_SEQ), DEC_BATCH)]).astype(F32)
    ang = pos[:, None] * inv_freq[None, :]
    cos, sin = jnp.cos(ang), jnp.sin(ang)
    cos2 = jnp.concatenate([cos, cos], axis=1)
    sin2 = jnp.concatenate([-sin, sin], axis=1)
    return cos2, sin2


def kernel(x_prompt, x_sample, cache_ckv, cache_krope, state_conv, state_ssm, norm1_w, w_in, q_norm_w, kv_norm_w, w_uq, w_ukv, conv_w, conv_b, dt_bias, a_log, d_skip, ssd_norm_w, w_mla_o, w_ssd_o, w_out, norm2_w, w_group, b_group, w_erouter, b_erouter, w_gate, w_up, w_down, final_norm_w):
    swap = np.concatenate([np.arange(QK_ROPE // 2, QK_ROPE), np.arange(QK_ROPE // 2)])
    x_p2 = x_prompt.reshape(T_PROMPT, D_MODEL)
    x_s2 = x_sample.reshape(T_SAMPLE, D_MODEL)
    cos2, sin2 = _rope_tables()
    cos128, sin128 = jnp.tile(cos2, (1, 2)), jnp.tile(sin2, (1, 2))

    wi = w_in[0]
    w_kr = wi[:, OFF_KR:OFF_KR + QK_ROPE]
    w_small = jnp.concatenate([wi[:, OFF_CQ:OFF_KR], wi[:, OFF_DT:OFF_DT + SSD_HEADS], w_kr, w_kr[:, swap]],
                              axis=1).astype(BF16)
    w_big = jnp.concatenate([wi[:, OFF_Z:OFF_DT], wi[:, OFF_GMLA:]], axis=1).astype(BF16)
    wq = w_uq[0].reshape(Q_LORA, MLA_HEADS, QK_NOPE + QK_ROPE)
    wq_nope = wq[:, :, :QK_NOPE].reshape(Q_LORA, MLA_HEADS * QK_NOPE).astype(BF16)
    wq_rope = wq[:, :, QK_NOPE:]
    wq_r = wq_rope.reshape(Q_LORA, MLA_HEADS * QK_ROPE).astype(BF16)
    wq_rs = wq_rope[:, :, swap].reshape(Q_LORA, MLA_HEADS * QK_ROPE).astype(BF16)
    wkv3 = w_ukv[0].reshape(KV_LORA, MLA_HEADS, QK_NOPE + V_HEAD)
    w_uk_all = wkv3[:, :, :QK_NOPE].reshape(KV_LORA, MLA_HEADS * QK_NOPE).astype(BF16)
    w_uv_all_t = jnp.transpose(wkv3[:, :, QK_NOPE:], (1, 2, 0)).reshape(MLA_HEADS * V_HEAD, KV_LORA).astype(BF16)
    w_uk_t = jnp.transpose(wkv3[:, :, :QK_NOPE], (1, 2, 0)).astype(BF16)
    w_uv_h = jnp.transpose(wkv3[:, :, QK_NOPE:], (1, 0, 2)).astype(BF16)

    u = _rms_rows(x_p2, x_s2, norm1_w[0], BF16)
    small = _mm(u, w_small, tm=MM_TM, tn=SM_N // 2, out_dtype=F32, name="proj_small")
    big = _mm(u, w_big, tm=MM_TM, tn=512, out_dtype=F32, name="proj_big")

    cqn, ckv, ckv_b, krope, krope_b = _mla_prep(small, q_norm_w[0], kv_norm_w[0], cos2, sin2)
    qn = _mm(cqn, wq_nope, tm=MM_TM, tn=1024, out_dtype=BF16, epilogue=lambda acc: acc * QSCALE, name="q_nope")
    qr = _q_rope(cqn, wq_r, wq_rs, cos128, sin128, tm=MM_TM, tn=1024)
    kn = _mm(ckv_b, w_uk_all, tm=MM_TM, tn=1024, out_dtype=BF16, name="k_up")
    vt = _mm_t(w_uv_all_t, ckv_b, rows=T_PROMPT, tm=ATTN_TK, tn=1024, out_dtype=BF16, name="v_up_t")

    qn_s = qn[T_PROMPT:].reshape(T_SAMPLE, MLA_HEADS, QK_NOPE).transpose(1, 0, 2)
    q_lat = _bmm(qn_s, w_uk_t, BF16, "q_absorb")
    q_lat = q_lat.reshape(MLA_HEADS, DEC_BATCH, DEC_SEQ, KV_LORA).transpose(1, 0, 2, 3)
    q_lat = q_lat.reshape(DEC_BATCH, MLA_HEADS * DEC_SEQ, KV_LORA)
    qr_s = qr[T_PROMPT:].reshape(DEC_BATCH, DEC_SEQ, MLA_HEADS, QK_ROPE).transpose(0, 2, 1, 3)
    qr_s = qr_s.reshape(DEC_BATCH, MLA_HEADS * DEC_SEQ, QK_ROPE)
    o_lat = _attn_sample(q_lat, qr_s, cache_ckv[0], cache_krope[0], ckv, krope)
    o_lat = o_lat.reshape(DEC_BATCH, MLA_HEADS, DEC_SEQ, KV_LORA).transpose(1, 0, 2, 3)
    o_lat = o_lat.reshape(MLA_HEADS, T_SAMPLE, KV_LORA)
    o_mla_s = _bmm(o_lat, w_uv_h, BF16, "v_absorb")
    o_mla_s = o_mla_s.transpose(1, 0, 2).reshape(T_SAMPLE, MLA_HEADS * V_HEAD)
    o_mla = _attn_prompt(qn, qr, kn, krope_b, vt, jnp.pad(o_mla_s, ((T_PROMPT, 0), (0, 0))),
                         nb=BATCH, seq=SEQ, heads=MLA_HEADS, tq=ATTN_TQ)

    prev_p = jnp.zeros((BATCH, 8, SSD_CONV_DIM), F32)
    prev_s = jnp.concatenate([jnp.zeros((DEC_BATCH, 8 - (SSD_CONV - 1), SSD_CONV_DIM), F32), state_conv[0]], axis=1)
    xbc_p = _conv_silu(big, prev_p, conv_w[0], conv_b[0], nb=BATCH, seq=SEQ, tl=512, row_off=0)
    xbc_s = _conv_silu(big, prev_s, conv_w[0], conv_b[0], nb=DEC_BATCH, seq=DEC_SEQ, tl=DEC_SEQ, row_off=T_PROMPT)
    dt_raw = small[:, SM_DT:SM_DT + SSD_HEADS]
    by_group = lambda d: d.reshape(d.shape[0], SSD_GROUPS, SSD_RANK).transpose(1, 0, 2)

    def pad_seq(a, fill):
        a = a.reshape(DEC_BATCH, DEC_SEQ, a.shape[-1])
        a = jnp.pad(a, ((0, 0), (0, CHUNK - DEC_SEQ), (0, 0)), constant_values=fill)
        return a.reshape(DEC_BATCH * CHUNK, a.shape[-1])

    h0_s = state_ssm[0].reshape(DEC_BATCH, SSD_HEADS * SSD_HEADDIM, SSD_STATE)
    y_s, ssm_s = _ssd(pad_seq(xbc_s, 0.0), pad_seq(big[T_PROMPT:, BIG_Z:BIG_Z + SSD_INNER], 0.0),
                      by_group(pad_seq(dt_raw[T_PROMPT:], -jnp.inf)), h0_s, dt_bias[0], a_log[0], d_skip[0],
                      ssd_norm_w[0], nb=DEC_BATCH, seq=CHUNK, sub=1)
    y_s = y_s.reshape(DEC_BATCH, CHUNK, SSD_INNER)[:, :DEC_SEQ].reshape(T_SAMPLE, SSD_INNER)
    o_ssd, ssm_p = _ssd(xbc_p, big, by_group(dt_raw[:T_PROMPT]), None, dt_bias[0], a_log[0], d_skip[0],
                        ssd_norm_w[0], nb=BATCH, seq=SEQ, sub=SSD_SUB, base=jnp.pad(y_s, ((T_PROMPT, 0), (0, 0))))

    gate = lambda acc, g: _sigmoid(g) * acc
    m1 = _mm(o_mla, w_mla_o[0].astype(BF16), tm=640, tn=1024, out_dtype=F32,
             extras=[(big, BIG_GMLA // 1024)], epilogue=gate, name="mla_out")
    merged = _mm(o_ssd, w_ssd_o[0].astype(BF16), tm=640, tn=512, out_dtype=BF16,
                 extras=[(big, BIG_GSSD // 512), (m1, 0)],
                 epilogue=lambda acc, g, m: m + _sigmoid(g) * acc, name="ssd_out")
    mix = _mm(merged, w_out[0].astype(BF16), tm=640, tn=1024, out_dtype=F32, name="out_proj")

    n_r = 128
    w_router = jnp.concatenate([w_group[0], w_erouter[0],
                                jnp.zeros((D_MODEL, n_r - N_GROUPS - N_EXPERTS), F32)], axis=1)
    b_router = jnp.concatenate([b_group[0], b_erouter[0], jnp.zeros((n_r - N_GROUPS - N_EXPERTS,), F32)])
    h, xn, logits = _norm_router(x_p2, x_s2, mix, norm2_w[0], w_router, b_router.reshape(1, n_r))
    row_token, row_w, pos, be, first, nused = _route(logits)
    nblk = be.shape[0]
    hact = jnp.zeros((nblk * MOE_TM, D_EXPERT), BF16)
    for s in range(MOE_SEGMENTS):
        b0, b1 = s * nblk // MOE_SEGMENTS, (s + 1) * nblk // MOE_SEGMENTS
        xs = jnp.take(xn, row_token[b0 * MOE_TM:b1 * MOE_TM], axis=0, mode="clip")
        first_s = jnp.concatenate([jnp.ones((1,), jnp.int32), first[b0 + 1:b1]])
        hact = _moe_up(be[b0:b1], first_s, jnp.clip(nused - b0, 0, b1 - b0), xs, w_gate[0], w_up[0], hact, b0)
    yb = _moe_down(be, first, nused, hact, w_down[0], row_w.reshape(-1, 1))
    y_p2, y_s2 = _final_norm(h, jnp.take(yb, pos[:, 0], axis=0, mode="clip"),
                             jnp.take(yb, pos[:, 1], axis=0, mode="clip"), final_norm_w)

    y_prompt = y_p2.reshape(BATCH, SEQ, D_MODEL)
    y_sample = y_s2.reshape(DEC_BATCH, DEC_SEQ, D_MODEL)
    ckv_p = ckv[:T_PROMPT].reshape(1, BATCH, SEQ, KV_LORA)
    ckv_s = ckv[T_PROMPT:].reshape(1, DEC_BATCH, DEC_SEQ, KV_LORA)
    kr_p = krope[:T_PROMPT].reshape(1, BATCH, SEQ, QK_ROPE)
    kr_s = krope[T_PROMPT:].reshape(1, DEC_BATCH, DEC_SEQ, QK_ROPE)
    tail = SSD_CONV - 1
    conv_p = jnp.stack([lax.slice(big, ((b + 1) * SEQ - tail, BIG_XBC), ((b + 1) * SEQ, BIG_XBC + SSD_CONV_DIM))
                        for b in range(BATCH)])[None]
    conv_s = jnp.stack([lax.slice(big, (T_PROMPT + (b + 1) * DEC_SEQ - tail, BIG_XBC),
                                  (T_PROMPT + (b + 1) * DEC_SEQ, BIG_XBC + SSD_CONV_DIM))
                        for b in range(DEC_BATCH)])[None]
    ssm_p = ssm_p.reshape(1, BATCH, SSD_HEADS, SSD_HEADDIM, SSD_STATE)
    ssm_s = ssm_s.reshape(1, DEC_BATCH, SSD_HEADS, SSD_HEADDIM, SSD_STATE)
    return (y_prompt, y_sample, ckv_p, kr_p, conv_p, ssm_p, ckv_s, kr_s, conv_s, ssm_s)
```

```python
import functools
import math

import numpy as np
import jax
import jax.numpy as jnp
from jax import lax
from jax.experimental import pallas as pl
from jax.experimental.pallas import tpu as pltpu

F32 = jnp.float32
BF16 = jnp.bfloat16
HI = lax.Precision.HIGHEST

D_MODEL = 4096
BATCH = 2
SEQ = 8192
DEC_BATCH = 8
DEC_SEQ = 32
PAST_LEN = 4096
CHUNK = 64
EPS = 1e-6
MLA_HEADS = 32
Q_LORA = 1024
KV_LORA = 512
QK_NOPE = 128
QK_ROPE = 64
V_HEAD = 128
ROPE_THETA = 10000.0
SCALE = (QK_NOPE + QK_ROPE) ** -0.5
QSCALE = SCALE * math.log2(math.e)
SSD_INNER = 2 * D_MODEL
SSD_HEADDIM = 64
SSD_HEADS = SSD_INNER // SSD_HEADDIM
SSD_STATE = 128
SSD_GROUPS = 8
SSD_RANK = SSD_HEADS // SSD_GROUPS
SSD_GCOLS = SSD_RANK * SSD_HEADDIM
SSD_CONV = 4
SSD_CONV_DIM = SSD_INNER + 2 * SSD_GROUPS * SSD_STATE
N_GROUPS = 8
EXPERTS_PER_GROUP = 8
N_EXPERTS = N_GROUPS * EXPERTS_PER_GROUP
TOP_K = 2
D_EXPERT = 1024

T_PROMPT = BATCH * SEQ
T_SAMPLE = DEC_BATCH * DEC_SEQ
T_ALL = T_PROMPT + T_SAMPLE

OFF_CQ = 0
OFF_CKV = OFF_CQ + Q_LORA
OFF_KR = OFF_CKV + KV_LORA
OFF_Z = OFF_KR + QK_ROPE
OFF_XBC = OFF_Z + SSD_INNER
OFF_DT = OFF_XBC + SSD_CONV_DIM
OFF_GMLA = OFF_DT + SSD_HEADS
OFF_GSSD = OFF_GMLA + D_MODEL
N_IN = OFF_GSSD + D_MODEL

BIG_Z = 0
BIG_XBC = SSD_INNER
BIG_N = BIG_XBC + SSD_CONV_DIM
GATE_MLA = 0
GATE_SSD = D_MODEL
SM_CQ = 0
SM_CKV = Q_LORA
SM_DT = SM_CKV + KV_LORA
SM_KR = SM_DT + SSD_HEADS
SM_KRS = SM_KR + QK_ROPE
SM_N = SM_KRS + QK_ROPE

V7X_VMEM_LIMIT = 56 * 1024 * 1024
ROW_TILE = 256
MM_TM = 1280
MOE_TM = 256
SSD_SUB = 8
MOE_TF = 512
MOE_TN = 2048
ATTN_TQ = 2048
ATTN_TK = 512


def _cparams(sem):
    return pltpu.CompilerParams(dimension_semantics=sem, vmem_limit_bytes=V7X_VMEM_LIMIT)


def _dot_nt(a, b):
    return lax.dot_general(a, b, (((1,), (1,)), ((), ())), preferred_element_type=F32)


def _rms(x, w):
    return x * lax.rsqrt(jnp.mean(x * x, axis=-1, keepdims=True) + EPS) * w


def _sigmoid(x):
    return 0.5 * jnp.tanh(0.5 * x) + 0.5


def _stacked_x_specs():
    n_p = T_PROMPT // ROW_TILE
    return [pl.BlockSpec((ROW_TILE, D_MODEL), lambda i: (jnp.minimum(i, n_p - 1), 0)),
            pl.BlockSpec((ROW_TILE, D_MODEL), lambda i: (0, 0))]


def _stacked_x(xp_ref, xs_ref):
    is_prompt = pl.program_id(0) < T_PROMPT // ROW_TILE
    return jnp.where(is_prompt, xp_ref[...], xs_ref[...])


def _rms_rows_body(xp_ref, xs_ref, w_ref, o_ref):
    o_ref[...] = _rms(_stacked_x(xp_ref, xs_ref), w_ref[...]).astype(o_ref.dtype)


def _rms_rows(x_p, x_s, w, out_dtype):
    d = D_MODEL
    return pl.pallas_call(
        _rms_rows_body,
        grid=(T_ALL // ROW_TILE,),
        in_specs=_stacked_x_specs() + [pl.BlockSpec((1, d), lambda i: (0, 0))],
        out_specs=pl.BlockSpec((ROW_TILE, d), lambda i: (i, 0)),
        out_shape=jax.ShapeDtypeStruct((T_ALL, d), out_dtype),
        compiler_params=_cparams(("parallel",)),
        name="rms_rows",
    )(x_p, x_s, w.reshape(1, d))


def _final_body(h_ref, ya_ref, yb_ref, w_ref, op_ref, os_ref):
    y = ya_ref[...].astype(F32) + yb_ref[...].astype(F32)
    out = _rms(h_ref[...] + y, w_ref[...])
    is_prompt = pl.program_id(0) < T_PROMPT // ROW_TILE

    @pl.when(is_prompt)
    def _():
        op_ref[...] = out

    @pl.when(jnp.logical_not(is_prompt))
    def _():
        os_ref[...] = out


def _final_norm(h, y_a, y_b, w):
    m, d = h.shape
    n_p = T_PROMPT // ROW_TILE
    row = pl.BlockSpec((ROW_TILE, d), lambda i: (i, 0))
    return pl.pallas_call(
        _final_body,
        grid=(m // ROW_TILE,),
        in_specs=[row, row, row, pl.BlockSpec((1, d), lambda i: (0, 0))],
        out_specs=[pl.BlockSpec((ROW_TILE, d), lambda i: (jnp.minimum(i, n_p - 1), 0)),
                   pl.BlockSpec((ROW_TILE, d), lambda i: (0, 0))],
        out_shape=[jax.ShapeDtypeStruct((T_PROMPT, d), F32), jax.ShapeDtypeStruct((T_SAMPLE, d), F32)],
        compiler_params=_cparams(("arbitrary",)),
        name="final_norm",
    )(h, y_a, y_b, w.reshape(1, d))


def _mm_body(epilogue, a_ref, w_ref, *rest):
    o_ref = rest[-1]
    acc = jnp.dot(a_ref[...], w_ref[...], preferred_element_type=F32)
    if epilogue is not None:
        acc = epilogue(acc, *[r[...] for r in rest[:-1]])
    o_ref[...] = acc.astype(o_ref.dtype)


def _mm(a, w, *, tm, tn, out_dtype, extras=(), epilogue=None, name="mm"):
    m, k = a.shape
    n = w.shape[1]
    in_specs = [pl.BlockSpec((tm, k), lambda i, j: (i, 0)),
                pl.BlockSpec((k, tn), lambda i, j: (0, j))]
    args = [a, w]
    for arr, off in extras:
        in_specs.append(pl.BlockSpec((tm, tn), functools.partial(lambda i, j, off: (i, j + off), off=off)))
        args.append(arr)
    return pl.pallas_call(
        functools.partial(_mm_body, epilogue),
        grid=(m // tm, n // tn),
        in_specs=in_specs,
        out_specs=pl.BlockSpec((tm, tn), lambda i, j: (i, j)),
        out_shape=jax.ShapeDtypeStruct((m, n), out_dtype),
        compiler_params=_cparams(("parallel", "parallel")),
        name=name,
    )(*args)


def _mla_prep_body(s_ref, qw_ref, kvw_ref, cos_ref, sin_ref, cqn_ref, ckv_ref, ckvb_ref, kr_ref, krb_ref):
    cqn_ref[...] = _rms(s_ref[:, SM_CQ:SM_CQ + Q_LORA], qw_ref[...]).astype(BF16)
    c = _rms(s_ref[:, SM_CKV:SM_CKV + KV_LORA], kvw_ref[...])
    ckv_ref[...] = c
    ckvb_ref[...] = c.astype(BF16)
    kr = (s_ref[:, SM_KR:SM_KR + QK_ROPE] * cos_ref[...]
          + s_ref[:, SM_KRS:SM_KRS + QK_ROPE] * sin_ref[...])
    kr_ref[...] = kr
    krb_ref[...] = kr.astype(BF16)


def _mla_prep(small, q_norm_w, kv_norm_w, cos2, sin2):
    m = small.shape[0]
    row = lambda n: pl.BlockSpec((ROW_TILE, n), lambda i: (i, 0))
    vec = lambda n: pl.BlockSpec((1, n), lambda i: (0, 0))
    return pl.pallas_call(
        _mla_prep_body,
        grid=(m // ROW_TILE,),
        in_specs=[row(SM_N), vec(Q_LORA), vec(KV_LORA), row(QK_ROPE), row(QK_ROPE)],
        out_specs=[row(Q_LORA), row(KV_LORA), row(KV_LORA), row(QK_ROPE), row(QK_ROPE)],
        out_shape=[jax.ShapeDtypeStruct((m, Q_LORA), BF16),
                   jax.ShapeDtypeStruct((m, KV_LORA), F32),
                   jax.ShapeDtypeStruct((m, KV_LORA), BF16),
                   jax.ShapeDtypeStruct((m, QK_ROPE), F32),
                   jax.ShapeDtypeStruct((m, QK_ROPE), BF16)],
        compiler_params=_cparams(("parallel",)),
        name="mla_prep",
    )(small, q_norm_w.reshape(1, -1), kv_norm_w.reshape(1, -1), cos2, sin2)


def _q_rope_body(a_ref, w_ref, ws_ref, cos_ref, sin_ref, o_ref):
    reps = o_ref.shape[1] // cos_ref.shape[1]
    a = a_ref[...]
    r = jnp.dot(a, w_ref[...], preferred_element_type=F32)
    rs = jnp.dot(a, ws_ref[...], preferred_element_type=F32)
    cos = jnp.tile(cos_ref[...], (1, reps))
    sin = jnp.tile(sin_ref[...], (1, reps))
    o_ref[...] = ((r * cos + rs * sin) * QSCALE).astype(o_ref.dtype)


def _q_rope(cqn, w_r, w_rs, cos128, sin128, *, tm, tn):
    m, k = cqn.shape
    n = w_r.shape[1]
    return pl.pallas_call(
        _q_rope_body,
        grid=(m // tm, n // tn),
        in_specs=[pl.BlockSpec((tm, k), lambda i, j: (i, 0)),
                  pl.BlockSpec((k, tn), lambda i, j: (0, j)),
                  pl.BlockSpec((k, tn), lambda i, j: (0, j)),
                  pl.BlockSpec((tm, 128), lambda i, j: (i, 0)),
                  pl.BlockSpec((tm, 128), lambda i, j: (i, 0))],
        out_specs=pl.BlockSpec((tm, tn), lambda i, j: (i, j)),
        out_shape=jax.ShapeDtypeStruct((m, n), BF16),
        compiler_params=_cparams(("parallel", "parallel")),
        name="q_rope",
    )(cqn, w_r, w_rs, cos128, sin128)


def _mm_t_body(w_ref, a_ref, o_ref):
    o_ref[0] = _dot_nt(w_ref[...], a_ref[...]).astype(o_ref.dtype)


def _mm_t(w_t, a, *, rows, tm, tn, out_dtype, name):
    n, k = w_t.shape
    return pl.pallas_call(
        _mm_t_body,
        grid=(rows // tm, n // tn),
        in_specs=[pl.BlockSpec((tn, k), lambda i, j: (j, 0)),
                  pl.BlockSpec((tm, k), lambda i, j: (i, 0))],
        out_specs=pl.BlockSpec((1, tn, tm), lambda i, j: (i, j, 0)),
        out_shape=jax.ShapeDtypeStruct((rows // tm, n, tm), out_dtype),
        compiler_params=_cparams(("parallel", "parallel")),
        name=name,
    )(w_t, a)


ATTN_KP = 256


ATTN_VP = 144


def _attn_prompt_body(qn_ref, qr_ref, kn_ref, kr_ref, vt_ref, base_ref, o_ref, kf_sc, vf_sc, qf_sc, acc_sc):
    qi = pl.program_id(2)
    tq = qn_ref.shape[0]
    tk = vt_ref.shape[2]
    seq = kn_ref.shape[0]
    pad = ATTN_KP - QK_NOPE - QK_ROPE

    @pl.when(qi == 0)
    def _():
        def fill(i, c):
            r = pl.ds(pl.multiple_of(i * tk, tk), tk)
            kr = kr_ref[r, :]
            for hh in range(2):
                c0 = hh * ATTN_KP
                kf_sc[r, c0:c0 + QK_NOPE] = kn_ref[r, hh * QK_NOPE:(hh + 1) * QK_NOPE]
                kf_sc[r, c0 + QK_NOPE:c0 + QK_NOPE + QK_ROPE] = kr
                kf_sc[r, c0 + QK_NOPE + QK_ROPE:c0 + ATTN_KP] = jnp.zeros((tk, pad), BF16)
                vf_sc[i, hh * ATTN_VP:hh * ATTN_VP + V_HEAD, :] = vt_ref[i, hh * V_HEAD:(hh + 1) * V_HEAD, :]
                vf_sc[i, hh * ATTN_VP + V_HEAD:(hh + 1) * ATTN_VP, :] = jnp.ones((ATTN_VP - V_HEAD, tk), BF16)
            return c
        lax.fori_loop(0, seq // tk, fill, 0)

    for hh in range(2):
        c0 = hh * ATTN_KP
        qf_sc[:, c0:c0 + QK_NOPE] = qn_ref[:, hh * QK_NOPE:(hh + 1) * QK_NOPE]
        qf_sc[:, c0 + QK_NOPE:c0 + QK_NOPE + QK_ROPE] = qr_ref[:, hh * QK_ROPE:(hh + 1) * QK_ROPE]
        qf_sc[:, c0 + QK_NOPE + QK_ROPE:c0 + ATTN_KP] = jnp.zeros((tq, pad), BF16)
    acc_sc[...] = jnp.zeros(acc_sc.shape, F32)

    def block(kb, carry, diag):
        r = pl.ds(pl.multiple_of(kb * tk, tk), tk)
        q0 = 0 if diag is None else diag * tk
        nq_cols = tq - q0
        out = []
        for hh in range(2):
            m = carry[hh][:, q0:]
            st = _dot_nt(kf_sc[r, hh * ATTN_KP:(hh + 1) * ATTN_KP], qf_sc[q0:, hh * ATTN_KP:(hh + 1) * ATTN_KP])
            if diag is not None:
                kc = lax.broadcasted_iota(jnp.int32, (tk, nq_cols), 0) // CHUNK
                qc = lax.broadcasted_iota(jnp.int32, (tk, nq_cols), 1) // CHUNK
                st = jnp.where(kc <= qc, st, -jnp.inf)
            m_new = jnp.maximum(m, jnp.max(st, axis=0, keepdims=True))
            alpha = jnp.exp2(m - m_new)
            pt = jnp.exp2(st - m_new).astype(BF16)
            vt = vf_sc[kb, hh * ATTN_VP:(hh + 1) * ATTN_VP, :]
            acc_sc[hh, :, q0:] = alpha * acc_sc[hh, :, q0:] + jnp.dot(vt, pt, preferred_element_type=F32)
            out.append(m_new if q0 == 0 else jnp.concatenate([carry[hh][:, :q0], m_new], axis=1))
        return tuple(out)

    n_diag = tq // tk
    n_full = qi * n_diag
    init = (jnp.full((1, tq), -jnp.inf, F32),) * 2
    carry = lax.fori_loop(0, n_full, lambda kb, c: block(kb, c, None), init)
    for d in range(n_diag):
        carry = block(n_full + d, carry, d)
    for hh in range(2):
        o = acc_sc[hh, 0:V_HEAD, :] / acc_sc[hh, V_HEAD:V_HEAD + 1, :]
        o_ref[:, hh * V_HEAD:(hh + 1) * V_HEAD] = o.T.astype(o_ref.dtype)


def _attn_prompt(qn, qr, kn, krb, vt, base, *, nb, seq, heads, tq):
    tk = vt.shape[2]
    nq = seq // tq
    nk = seq // tk
    return pl.pallas_call(
        _attn_prompt_body,
        grid=(nb, heads // 2, nq),
        in_specs=[pl.BlockSpec((tq, 2 * QK_NOPE), lambda b, hp, qi: (b * nq + qi, hp)),
                  pl.BlockSpec((tq, 2 * QK_ROPE), lambda b, hp, qi: (b * nq + qi, hp)),
                  pl.BlockSpec((seq, 2 * QK_NOPE), lambda b, hp, qi: (b, hp)),
                  pl.BlockSpec((seq, QK_ROPE), lambda b, hp, qi: (b, 0)),
                  pl.BlockSpec((nk, 2 * V_HEAD, tk), lambda b, hp, qi: (b, hp, 0)),
                  pl.BlockSpec(memory_space=pl.ANY)],
        out_specs=pl.BlockSpec((tq, 2 * V_HEAD), lambda b, hp, qi: (b * nq + qi, hp)),
        out_shape=jax.ShapeDtypeStruct(base.shape, base.dtype),
        input_output_aliases={5: 0},
        scratch_shapes=[pltpu.VMEM((seq, 2 * ATTN_KP), BF16), pltpu.VMEM((nk, 2 * ATTN_VP, tk), BF16),
                        pltpu.VMEM((tq, 2 * ATTN_KP), BF16), pltpu.VMEM((2, ATTN_VP, tq), F32)],
        compiler_params=_cparams(("parallel", "parallel", "arbitrary")),
        name="attn_prompt",
    )(qn, qr, kn, krb, vt, base)


def _bmm_body(a_ref, w_ref, o_ref):
    o_ref[0] = jnp.dot(a_ref[0], w_ref[0], preferred_element_type=F32).astype(o_ref.dtype)


def _bmm(a, w, out_dtype, name):
    h, m, k = a.shape
    n = w.shape[2]
    return pl.pallas_call(
        _bmm_body,
        grid=(h,),
        in_specs=[pl.BlockSpec((1, m, k), lambda i: (i, 0, 0)),
                  pl.BlockSpec((1, k, n), lambda i: (i, 0, 0))],
        out_specs=pl.BlockSpec((1, m, n), lambda i: (i, 0, 0)),
        out_shape=jax.ShapeDtypeStruct((h, m, n), out_dtype),
        compiler_params=_cparams(("parallel",)),
        name=name,
    )(a, w)


def _attn_sample_body(ql_ref, qr_ref, cc_ref, ck_ref, nc_ref, nk_ref, o_ref, ccb_sc, ckb_sc):
    @pl.when(pl.program_id(1) == 0)
    def _():
        ccb_sc[...] = cc_ref[0].astype(BF16)
        ckb_sc[...] = ck_ref[0].astype(BF16)

    ql = ql_ref[0]
    qr = qr_ref[0]
    ncb = nc_ref[...].astype(BF16)
    nkb = nk_ref[...].astype(BF16)
    s1 = _dot_nt(ql, ccb_sc[...]) + _dot_nt(qr, ckb_sc[...])
    s2 = _dot_nt(ql, ncb) + _dot_nt(qr, nkb)
    m = jnp.maximum(jnp.max(s1, axis=1, keepdims=True), jnp.max(s2, axis=1, keepdims=True))
    p1 = jnp.exp2(s1 - m)
    p2 = jnp.exp2(s2 - m)
    l = jnp.sum(p1, axis=1, keepdims=True) + jnp.sum(p2, axis=1, keepdims=True)
    o = (jnp.dot(p1.astype(BF16), ccb_sc[...], preferred_element_type=F32)
         + jnp.dot(p2.astype(BF16), ncb, preferred_element_type=F32))
    o_ref[0] = (o / l).astype(o_ref.dtype)


def _attn_sample(q_lat, q_rope, cache_ckv, cache_krope, ckv, krope):
    rows = q_lat.shape[1]
    tr = 256
    new_blk = T_PROMPT // DEC_SEQ
    return pl.pallas_call(
        _attn_sample_body,
        grid=(DEC_BATCH, rows // tr),
        in_specs=[pl.BlockSpec((1, tr, KV_LORA), lambda b, r: (b, r, 0)),
                  pl.BlockSpec((1, tr, QK_ROPE), lambda b, r: (b, r, 0)),
                  pl.BlockSpec((1, PAST_LEN, KV_LORA), lambda b, r: (b, 0, 0)),
                  pl.BlockSpec((1, PAST_LEN, QK_ROPE), lambda b, r: (b, 0, 0)),
                  pl.BlockSpec((DEC_SEQ, KV_LORA), lambda b, r: (new_blk + b, 0)),
                  pl.BlockSpec((DEC_SEQ, QK_ROPE), lambda b, r: (new_blk + b, 0))],
        out_specs=pl.BlockSpec((1, tr, KV_LORA), lambda b, r: (b, r, 0)),
        out_shape=jax.ShapeDtypeStruct((DEC_BATCH, rows, KV_LORA), BF16),
        scratch_shapes=[pltpu.VMEM((PAST_LEN, KV_LORA), BF16), pltpu.VMEM((PAST_LEN, QK_ROPE), BF16)],
        compiler_params=_cparams(("parallel", "arbitrary")),
        name="attn_sample",
    )(q_lat, q_rope, cache_ckv, cache_krope, ckv, krope)


def _conv_body(x_ref, prev_ref, w_ref, b_ref, o_ref, ext_sc):
    tl = x_ref.shape[0]

    @pl.when(pl.program_id(2) == 0)
    def _():
        ext_sc[0:8, :] = prev_ref[0]

    @pl.when(pl.program_id(2) != 0)
    def _():
        ext_sc[0:8, :] = ext_sc[tl:tl + 8, :]

    ext_sc[8:8 + tl, :] = x_ref[...]
    acc = b_ref[...] + ext_sc[5:5 + tl, :] * w_ref[0:1, :]
    for k in range(1, SSD_CONV):
        acc = acc + ext_sc[5 + k:5 + k + tl, :] * w_ref[k:k + 1, :]
    o_ref[...] = (acc * _sigmoid(acc)).astype(o_ref.dtype)


def _conv_silu(big, prev8, conv_w, conv_b, *, nb, seq, tl, row_off):
    tc = 1024
    nrt = seq // tl
    rb0 = row_off // tl
    cb0 = BIG_XBC // tc
    return pl.pallas_call(
        _conv_body,
        grid=(nb, SSD_CONV_DIM // tc, nrt),
        in_specs=[pl.BlockSpec((tl, tc), lambda b, j, r: (rb0 + b * nrt + r, cb0 + j)),
                  pl.BlockSpec((1, 8, tc), lambda b, j, r: (b, 0, j)),
                  pl.BlockSpec((SSD_CONV, tc), lambda b, j, r: (0, j)),
                  pl.BlockSpec((1, tc), lambda b, j, r: (0, j))],
        out_specs=pl.BlockSpec((tl, tc), lambda b, j, r: (b * nrt + r, j)),
        out_shape=jax.ShapeDtypeStruct((nb * seq, SSD_CONV_DIM), BF16),
        scratch_shapes=[pltpu.VMEM((tl + 8, tc), F32)],
        compiler_params=_cparams(("parallel", "parallel", "arbitrary")),
        name="conv_silu",
    )(big, prev8, conv_w, conv_b.reshape(1, -1))


def _softplus(x):
    return jnp.maximum(x, 0.0) + jnp.log(1.0 + jnp.exp(-jnp.abs(x)))


def _dot_sel(sel, a, *, sel_left):
    hi = a.astype(BF16)
    r1 = a - hi.astype(F32)
    mid = r1.astype(BF16)
    lo = (r1 - mid.astype(F32)).astype(BF16)
    out = None
    for term in (hi, mid, lo):
        d = (jnp.dot(sel, term, preferred_element_type=F32) if sel_left
             else jnp.dot(term, sel, preferred_element_type=F32))
        out = d if out is None else out + d
    return out


def _ssd_body(has_h0, x_ref, b_ref, c_ref, z_ref, dt_ref, bias_ref, al_ref, dsk_ref, nw_ref, *rest):
    h0_ref = rest[0] if has_h0 else None
    y_ref, hout_ref, ht_sc = rest[-3:]
    R, P = SSD_RANK, SSD_HEADDIM
    lc = P
    W = 2 * P
    c_idx = pl.program_id(2)

    @pl.when(c_idx == 0)
    def _():
        if has_h0:
            ht_sc[...] = h0_ref[0].T
        else:
            ht_sc[...] = jnp.zeros(ht_sc.shape, F32)

    ri = lax.broadcasted_iota(jnp.int32, (lc, lc), 0)
    ci = lax.broadcasted_iota(jnp.int32, (lc, lc), 1)
    tril_b = jnp.where(ri >= ci, 1.0, 0.0).astype(BF16)
    er = lax.broadcasted_iota(jnp.int32, (R, R * P), 0)
    ec = lax.broadcasted_iota(jnp.int32, (R, R * P), 1) // P
    expand = jnp.where(er == ec, 1.0, 0.0).astype(BF16)
    li = lax.broadcasted_iota(jnp.int32, (lc, R * P), 0)
    si = lax.broadcasted_iota(jnp.int32, (lc, R * P), 1) % P
    first_head = lax.broadcasted_iota(jnp.int32, (lc, W), 1) < P
    neg_a = -jnp.exp(al_ref[0])

    ht = ht_sc[...]
    for k in range(x_ref.shape[0] // lc):
        rows = slice(k * lc, (k + 1) * lc)
        dt = _softplus(dt_ref[0, rows, :] + bias_ref[0])
        a_cs = _dot_sel(tril_b, dt * neg_a, sel_left=True)
        a_exp = _dot_sel(expand, a_cs, sel_left=False)
        dt_exp = _dot_sel(expand, dt, sel_left=False)
        a_last = a_exp[lc - 1:lc, :]
        a_key = jnp.sum(jnp.where(li == si, a_exp, 0.0), axis=0, keepdims=True)
        decay = jnp.exp(jnp.where(li >= si, a_exp - a_key, -jnp.inf))

        x = x_ref[rows, :].astype(F32)
        xdt = x * dt_exp
        bm = b_ref[rows, :].astype(BF16)
        cm = c_ref[rows, :].astype(BF16)
        cb2 = _dot_nt(cm, jnp.concatenate([bm, bm], axis=0))
        m_all = (jnp.tile(cb2, (1, R // 2)) * decay).astype(BF16)
        y_off = jnp.dot(cm, ht.astype(BF16), preferred_element_type=F32) * jnp.exp(a_exp)
        xdt_b = xdt.astype(BF16)
        y_diag = []
        for j in range(R // 2):
            xp = xdt_b[:, j * W:(j + 1) * W]
            zero = jnp.zeros_like(xp)
            stack = jnp.concatenate([jnp.where(first_head, xp, zero), jnp.where(first_head, zero, xp)], axis=0)
            y_diag.append(jnp.dot(m_all[:, j * W:(j + 1) * W], stack, preferred_element_type=F32))
        wgt = (xdt * jnp.exp(a_last - a_exp)).astype(BF16)
        states_t = lax.dot_general(bm, wgt, (((0,), (0,)), ((), ())), preferred_element_type=F32)
        ht = ht * jnp.exp(a_last) + states_t

        y = jnp.concatenate(y_diag, axis=1) + y_off + x * dsk_ref[...]
        z = z_ref[rows, :]
        y = y * (z * _sigmoid(z))
        y = y * lax.rsqrt(jnp.mean(y * y, axis=-1, keepdims=True) + EPS) * nw_ref[...]
        y_ref[rows, :] = y.astype(y_ref.dtype)
    ht_sc[...] = ht

    @pl.when(c_idx == pl.num_programs(2) - 1)
    def _():
        hout_ref[0] = ht.T


def _ssd(xbc, zsrc, dt_g, h0, dt_bias, a_log, d_skip, ssd_norm_w, *, nb, seq, sub, base=None):
    lc = sub * SSD_HEADDIM
    nc = seq // lc
    G, R, GC, N = SSD_GROUPS, SSD_RANK, SSD_GCOLS, SSD_STATE
    row = lambda b, g, c: b * nc + c
    in_specs = [
        pl.BlockSpec((lc, GC), lambda b, g, c: (row(b, g, c), g)),
        pl.BlockSpec((lc, N), lambda b, g, c: (row(b, g, c), SSD_INNER // N + g)),
        pl.BlockSpec((lc, N), lambda b, g, c: (row(b, g, c), SSD_INNER // N + G + g)),
        pl.BlockSpec((lc, GC), lambda b, g, c: (row(b, g, c), g)),
        pl.BlockSpec((1, lc, R), lambda b, g, c: (g, row(b, g, c), 0)),
        pl.BlockSpec((1, 1, R), lambda b, g, c: (g, 0, 0)),
        pl.BlockSpec((1, 1, R), lambda b, g, c: (g, 0, 0)),
        pl.BlockSpec((1, GC), lambda b, g, c: (0, g)),
        pl.BlockSpec((1, GC), lambda b, g, c: (0, g)),
    ]
    args = [xbc, xbc, xbc, zsrc, dt_g, dt_bias.reshape(G, 1, R), a_log.reshape(G, 1, R),
            jnp.repeat(d_skip, SSD_HEADDIM).reshape(1, SSD_INNER), ssd_norm_w.reshape(1, SSD_INNER)]
    if h0 is not None:
        in_specs.append(pl.BlockSpec((1, GC, N), lambda b, g, c: (b, g, 0)))
        args.append(h0)
    aliases = {}
    y_shape = jax.ShapeDtypeStruct((nb * seq, SSD_INNER), BF16)
    if base is not None:
        aliases = {len(args): 0}
        in_specs.append(pl.BlockSpec(memory_space=pl.ANY))
        args.append(base)
        y_shape = jax.ShapeDtypeStruct(base.shape, base.dtype)
    return pl.pallas_call(
        functools.partial(_ssd_body, h0 is not None),
        grid=(nb, G, nc),
        in_specs=in_specs,
        out_specs=[pl.BlockSpec((lc, GC), lambda b, g, c: (row(b, g, c), g)),
                   pl.BlockSpec((1, GC, N), lambda b, g, c: (b, g, 0))],
        out_shape=[y_shape, jax.ShapeDtypeStruct((nb, SSD_HEADS * SSD_HEADDIM, N), F32)],
        input_output_aliases=aliases,
        scratch_shapes=[pltpu.VMEM((N, GC), F32)],
        compiler_params=_cparams(("parallel", "parallel", "arbitrary")),
        name="ssd",
    )(*args)


def _norm_router_body(xp_ref, xs_ref, mix_ref, w_ref, wr_ref, br_ref, h_ref, xn_ref, lg_ref):
    h = _stacked_x(xp_ref, xs_ref) + mix_ref[...]
    h_ref[...] = h
    xn = _rms(h, w_ref[...])
    xn_ref[...] = xn.astype(BF16)
    lg_ref[...] = jnp.dot(xn, wr_ref[...], precision=HI, preferred_element_type=F32) + br_ref[...]


def _norm_router(x_p, x_s, mix, norm_w, w_router, b_router):
    m, d = mix.shape
    n = w_router.shape[1]
    row = pl.BlockSpec((ROW_TILE, d), lambda i: (i, 0))
    return pl.pallas_call(
        _norm_router_body,
        grid=(m // ROW_TILE,),
        in_specs=_stacked_x_specs() + [row,
                                       pl.BlockSpec((1, d), lambda i: (0, 0)),
                                       pl.BlockSpec((d, n), lambda i: (0, 0)),
                                       pl.BlockSpec((1, n), lambda i: (0, 0))],
        out_specs=[row, row, pl.BlockSpec((ROW_TILE, n), lambda i: (i, 0))],
        out_shape=[jax.ShapeDtypeStruct((m, d), F32), jax.ShapeDtypeStruct((m, d), BF16),
                   jax.ShapeDtypeStruct((m, n), F32)],
        compiler_params=_cparams(("parallel",)),
        name="norm_router",
    )(x_p, x_s, mix, norm_w.reshape(1, d), w_router, b_router)


def _moe_up_body(be_ref, nused_ref, x_ref, wg_ref, wu_ref, o_ref):
    blk = pl.program_id(1)

    @pl.when(blk < nused_ref[0])
    def _():
        x = x_ref[...]
        g = jnp.dot(x, wg_ref[0].astype(BF16), preferred_element_type=F32)
        u = jnp.dot(x, wu_ref[0].astype(BF16), preferred_element_type=F32)
        o_ref[...] = (g * _sigmoid(g) * u).astype(o_ref.dtype)

    @pl.when(blk >= nused_ref[0])
    def _():
        o_ref[...] = jnp.zeros(o_ref.shape, o_ref.dtype)


def _moe_up(be, nused, xs, w_gate, w_up):
    p, d = xs.shape
    nblk = p // MOE_TM
    grid_spec = pltpu.PrefetchScalarGridSpec(
        num_scalar_prefetch=2,
        grid=(D_EXPERT // MOE_TF, nblk),
        in_specs=[pl.BlockSpec((MOE_TM, d), lambda f, i, be, nu: (i, 0)),
                  pl.BlockSpec((1, d, MOE_TF), lambda f, i, be, nu: (be[i], 0, f)),
                  pl.BlockSpec((1, d, MOE_TF), lambda f, i, be, nu: (be[i], 0, f))],
        out_specs=pl.BlockSpec((MOE_TM, MOE_TF), lambda f, i, be, nu: (i, f)),
    )
    return pl.pallas_call(
        _moe_up_body,
        grid_spec=grid_spec,
        out_shape=jax.ShapeDtypeStruct((p, D_EXPERT), BF16),
        compiler_params=_cparams(("arbitrary", "arbitrary")),
        name="moe_up",
    )(be, nused, xs, w_gate, w_up)


def _moe_down_body(be_ref, nused_ref, h_ref, wd_ref, rw_ref, o_ref):
    blk = pl.program_id(1)

    @pl.when(blk < nused_ref[0])
    def _():
        y = jnp.dot(h_ref[...], wd_ref[0].astype(BF16), preferred_element_type=F32)
        o_ref[...] = (y * rw_ref[...]).astype(o_ref.dtype)

    @pl.when(blk >= nused_ref[0])
    def _():
        o_ref[...] = jnp.zeros(o_ref.shape, o_ref.dtype)


def _moe_down(be, nused, hact, w_down, row_w):
    p, f = hact.shape
    d = w_down.shape[2]
    nblk = p // MOE_TM
    grid_spec = pltpu.PrefetchScalarGridSpec(
        num_scalar_prefetch=2,
        grid=(d // MOE_TN, nblk),
        in_specs=[pl.BlockSpec((MOE_TM, f), lambda n, i, be, nu: (i, 0)),
                  pl.BlockSpec((1, f, MOE_TN), lambda n, i, be, nu: (be[i], 0, n)),
                  pl.BlockSpec((MOE_TM, 1), lambda n, i, be, nu: (i, 0))],
        out_specs=pl.BlockSpec((MOE_TM, MOE_TN), lambda n, i, be, nu: (i, n)),
    )
    return pl.pallas_call(
        _moe_down_body,
        grid_spec=grid_spec,
        out_shape=jax.ShapeDtypeStruct((p, d), BF16),
        compiler_params=_cparams(("arbitrary", "arbitrary")),
        name="moe_down",
    )(be, nused, hact, w_down, row_w)


def _route(logits):
    t = logits.shape[0]
    g_logits = logits[:, :N_GROUPS]
    g_sel = jnp.argmax(g_logits, axis=-1)
    g_w = jnp.max(jax.nn.softmax(g_logits, axis=-1), axis=-1)
    e_logits = logits[:, N_GROUPS:N_GROUPS + N_EXPERTS].reshape(t, N_GROUPS, EXPERTS_PER_GROUP)
    e_in = jnp.take_along_axis(e_logits, g_sel[:, None, None], axis=1)[:, 0]
    e_val, e_idx = lax.top_k(e_in, TOP_K)
    e_w = jax.nn.softmax(e_val, axis=-1) * g_w[:, None]
    expert_id = (g_sel[:, None] * EXPERTS_PER_GROUP + e_idx).reshape(-1).astype(jnp.int32)
    a = t * TOP_K
    e_sorted, order = lax.sort((expert_id, jnp.arange(a, dtype=jnp.int32)), num_keys=1, is_stable=True)
    experts = jnp.arange(N_EXPERTS, dtype=jnp.int32)
    start = jnp.searchsorted(e_sorted, experts, side='left').astype(jnp.int32)
    counts = jnp.searchsorted(e_sorted, experts, side='right').astype(jnp.int32) - start
    padded = (counts + MOE_TM - 1) // MOE_TM * MOE_TM
    pad_end = jnp.cumsum(padded)
    pad_start = pad_end - padded
    shift = pad_start - start
    jumps = shift - jnp.concatenate([jnp.zeros((1,), jnp.int32), shift[:-1]])
    marks = jnp.zeros((a + 1,), jnp.int32).at[start].add(jumps)
    dest = jnp.arange(a, dtype=jnp.int32) + jnp.cumsum(marks)[:a]
    pos = lax.sort((order, dest), num_keys=1)[1].reshape(t, TOP_K)
    nblk = a // MOE_TM + N_EXPERTS
    nused = (pad_end[-1] // MOE_TM).astype(jnp.int32)
    blk = jnp.arange(nblk, dtype=jnp.int32)
    be = jnp.minimum(jnp.searchsorted(pad_end, blk * MOE_TM, side='right'), N_EXPERTS - 1).astype(jnp.int32)
    be = jnp.where(blk < nused, be, be[jnp.maximum(nused - 1, 0)])
    per_row = lambda v: jnp.broadcast_to(v[be][:, None], (nblk, MOE_TM)).reshape(-1)
    rank = jnp.arange(nblk * MOE_TM, dtype=jnp.int32) - per_row(pad_start)
    valid = rank < per_row(counts)
    src = jnp.clip(per_row(start) + rank, 0, a - 1)
    row_token = jnp.where(valid, order[src] // TOP_K, 0)
    row_w = jnp.where(valid, e_w.reshape(-1)[order[src]], 0.0)
    return row_token, row_w, pos, be, nused.reshape(1)


def _rope_tables():
    half = QK_ROPE // 2
    inv_freq = ROPE_THETA ** (-jnp.arange(half, dtype=F32) / half)
    pos = jnp.concatenate([jnp.tile(jnp.arange(SEQ), BATCH),
                           jnp.tile(PAST_LEN + jnp.arange(DEC_SEQ), DEC_BATCH)]).astype(F32)
    ang = pos[:, None] * inv_freq[None, :]
    cos, sin = jnp.cos(ang), jnp.sin(ang)
    cos2 = jnp.concatenate([cos, cos], axis=1)
    sin2 = jnp.concatenate([-sin, sin], axis=1)
    return cos2, sin2


def kernel(x_prompt, x_sample, cache_ckv, cache_krope, state_conv, state_ssm, norm1_w, w_in, q_norm_w, kv_norm_w, w_uq, w_ukv, conv_w, conv_b, dt_bias, a_log, d_skip, ssd_norm_w, w_mla_o, w_ssd_o, w_out, norm2_w, w_group, b_group, w_erouter, b_erouter, w_gate, w_up, w_down, final_norm_w):
    swap = np.concatenate([np.arange(QK_ROPE // 2, QK_ROPE), np.arange(QK_ROPE // 2)])
    x_p2 = x_prompt.reshape(T_PROMPT, D_MODEL)
    x_s2 = x_sample.reshape(T_SAMPLE, D_MODEL)
    cos2, sin2 = _rope_tables()
    cos128, sin128 = jnp.tile(cos2, (1, 2)), jnp.tile(sin2, (1, 2))

    wi = w_in[0]
    w_kr = wi[:, OFF_KR:OFF_KR + QK_ROPE]
    w_small = jnp.concatenate([wi[:, OFF_CQ:OFF_KR], wi[:, OFF_DT:OFF_DT + SSD_HEADS], w_kr, w_kr[:, swap]],
                              axis=1).astype(BF16)
    w_big = wi[:, OFF_Z:OFF_DT].astype(BF16)
    w_gates = wi[:, OFF_GMLA:].astype(BF16)
    wq = w_uq[0].reshape(Q_LORA, MLA_HEADS, QK_NOPE + QK_ROPE)
    wq_nope = wq[:, :, :QK_NOPE].reshape(Q_LORA, MLA_HEADS * QK_NOPE).astype(BF16)
    wq_rope = wq[:, :, QK_NOPE:]
    wq_r = wq_rope.reshape(Q_LORA, MLA_HEADS * QK_ROPE).astype(BF16)
    wq_rs = wq_rope[:, :, swap].reshape(Q_LORA, MLA_HEADS * QK_ROPE).astype(BF16)
    wkv3 = w_ukv[0].reshape(KV_LORA, MLA_HEADS, QK_NOPE + V_HEAD)
    w_uk_all = wkv3[:, :, :QK_NOPE].reshape(KV_LORA, MLA_HEADS * QK_NOPE).astype(BF16)
    w_uv_all_t = jnp.transpose(wkv3[:, :, QK_NOPE:], (1, 2, 0)).reshape(MLA_HEADS * V_HEAD, KV_LORA).astype(BF16)
    w_uk_t = jnp.transpose(wkv3[:, :, :QK_NOPE], (1, 2, 0)).astype(BF16)
    w_uv_h = jnp.transpose(wkv3[:, :, QK_NOPE:], (1, 0, 2)).astype(BF16)

    u = _rms_rows(x_p2, x_s2, norm1_w[0], BF16)
    small = _mm(u, w_small, tm=MM_TM, tn=SM_N // 2, out_dtype=F32, name="proj_small")
    big = _mm(u, w_big, tm=MM_TM, tn=512, out_dtype=F32, name="proj_big")
    gates = _mm(u, w_gates, tm=MM_TM, tn=512, out_dtype=F32, name="proj_gates")

    cqn, ckv, ckv_b, krope, krope_b = _mla_prep(small, q_norm_w[0], kv_norm_w[0], cos2, sin2)
    qn = _mm(cqn, wq_nope, tm=MM_TM, tn=1024, out_dtype=BF16, epilogue=lambda acc: acc * QSCALE, name="q_nope")
    qr = _q_rope(cqn, wq_r, wq_rs, cos128, sin128, tm=MM_TM, tn=1024)
    kn = _mm(ckv_b, w_uk_all, tm=MM_TM, tn=1024, out_dtype=BF16, name="k_up")
    vt = _mm_t(w_uv_all_t, ckv_b, rows=T_PROMPT, tm=ATTN_TK, tn=1024, out_dtype=BF16, name="v_up_t")

    qn_s = qn[T_PROMPT:].reshape(T_SAMPLE, MLA_HEADS, QK_NOPE).transpose(1, 0, 2)
    q_lat = _bmm(qn_s, w_uk_t, BF16, "q_absorb")
    q_lat = q_lat.reshape(MLA_HEADS, DEC_BATCH, DEC_SEQ, KV_LORA).transpose(1, 0, 2, 3)
    q_lat = q_lat.reshape(DEC_BATCH, MLA_HEADS * DEC_SEQ, KV_LORA)
    qr_s = qr[T_PROMPT:].reshape(DEC_BATCH, DEC_SEQ, MLA_HEADS, QK_ROPE).transpose(0, 2, 1, 3)
    qr_s = qr_s.reshape(DEC_BATCH, MLA_HEADS * DEC_SEQ, QK_ROPE)
    o_lat = _attn_sample(q_lat, qr_s, cache_ckv[0], cache_krope[0], ckv, krope)
    o_lat = o_lat.reshape(DEC_BATCH, MLA_HEADS, DEC_SEQ, KV_LORA).transpose(1, 0, 2, 3)
    o_lat = o_lat.reshape(MLA_HEADS, T_SAMPLE, KV_LORA)
    o_mla_s = _bmm(o_lat, w_uv_h, BF16, "v_absorb")
    o_mla_s = o_mla_s.transpose(1, 0, 2).reshape(T_SAMPLE, MLA_HEADS * V_HEAD)
    o_mla = _attn_prompt(qn, qr, kn, krope_b, vt, jnp.pad(o_mla_s, ((T_PROMPT, 0), (0, 0))),
                         nb=BATCH, seq=SEQ, heads=MLA_HEADS, tq=ATTN_TQ)

    prev_p = jnp.zeros((BATCH, 8, SSD_CONV_DIM), F32)
    prev_s = jnp.concatenate([jnp.zeros((DEC_BATCH, 8 - (SSD_CONV - 1), SSD_CONV_DIM), F32), state_conv[0]], axis=1)
    xbc_p = _conv_silu(big, prev_p, conv_w[0], conv_b[0], nb=BATCH, seq=SEQ, tl=512, row_off=0)
    xbc_s = _conv_silu(big, prev_s, conv_w[0], conv_b[0], nb=DEC_BATCH, seq=DEC_SEQ, tl=DEC_SEQ, row_off=T_PROMPT)
    dt_raw = small[:, SM_DT:SM_DT + SSD_HEADS]
    by_group = lambda d: d.reshape(d.shape[0], SSD_GROUPS, SSD_RANK).transpose(1, 0, 2)

    def pad_seq(a, fill):
        a = a.reshape(DEC_BATCH, DEC_SEQ, a.shape[-1])
        a = jnp.pad(a, ((0, 0), (0, CHUNK - DEC_SEQ), (0, 0)), constant_values=fill)
        return a.reshape(DEC_BATCH * CHUNK, a.shape[-1])

    h0_s = state_ssm[0].reshape(DEC_BATCH, SSD_HEADS * SSD_HEADDIM, SSD_STATE)
    y_s, ssm_s = _ssd(pad_seq(xbc_s, 0.0), pad_seq(big[T_PROMPT:, BIG_Z:BIG_Z + SSD_INNER], 0.0),
                      by_group(pad_seq(dt_raw[T_PROMPT:], -jnp.inf)), h0_s, dt_bias[0], a_log[0], d_skip[0],
                      ssd_norm_w[0], nb=DEC_BATCH, seq=CHUNK, sub=1)
    y_s = y_s.reshape(DEC_BATCH, CHUNK, SSD_INNER)[:, :DEC_SEQ].reshape(T_SAMPLE, SSD_INNER)
    o_ssd, ssm_p = _ssd(xbc_p, big, by_group(dt_raw[:T_PROMPT]), None, dt_bias[0], a_log[0], d_skip[0],
                        ssd_norm_w[0], nb=BATCH, seq=SEQ, sub=SSD_SUB, base=jnp.pad(y_s, ((T_PROMPT, 0), (0, 0))))

    gate = lambda acc, g: _sigmoid(g) * acc
    m1 = _mm(o_mla, w_mla_o[0].astype(BF16), tm=640, tn=1024, out_dtype=F32,
             extras=[(gates, GATE_MLA // 1024)], epilogue=gate, name="mla_out")
    merged = _mm(o_ssd, w_ssd_o[0].astype(BF16), tm=640, tn=512, out_dtype=BF16,
                 extras=[(gates, GATE_SSD // 512), (m1, 0)],
                 epilogue=lambda acc, g, m: m + _sigmoid(g) * acc, name="ssd_out")
    mix = _mm(merged, w_out[0].astype(BF16), tm=640, tn=1024, out_dtype=F32, name="out_proj")

    n_r = 128
    w_router = jnp.concatenate([w_group[0], w_erouter[0],
                                jnp.zeros((D_MODEL, n_r - N_GROUPS - N_EXPERTS), F32)], axis=1)
    b_router = jnp.concatenate([b_group[0], b_erouter[0], jnp.zeros((n_r - N_GROUPS - N_EXPERTS,), F32)])
    h, xn, logits = _norm_router(x_p2, x_s2, mix, norm2_w[0], w_router, b_router.reshape(1, n_r))
    row_token, row_w, pos, be, nused = _route(logits)
    xs = jnp.take(xn, row_token, axis=0, mode="clip")
    hact = _moe_up(be, nused, xs, w_gate[0], w_up[0])
    yb = _moe_down(be, nused, hact, w_down[0], row_w.reshape(-1, 1))
    y_p2, y_s2 = _final_norm(h, jnp.take(yb, pos[:, 0], axis=0, mode="clip"),
                             jnp.take(yb, pos[:, 1], axis=0, mode="clip"), final_norm_w)

    y_prompt = y_p2.reshape(BATCH, SEQ, D_MODEL)
    y_sample = y_s2.reshape(DEC_BATCH, DEC_SEQ, D_MODEL)
    ckv_p = ckv[:T_PROMPT].reshape(1, BATCH, SEQ, KV_LORA)
    ckv_s = ckv[T_PROMPT:].reshape(1, DEC_BATCH, DEC_SEQ, KV_LORA)
    kr_p = krope[:T_PROMPT].reshape(1, BATCH, SEQ, QK_ROPE)
    kr_s = krope[T_PROMPT:].reshape(1, DEC_BATCH, DEC_SEQ, QK_ROPE)
    tail = SSD_CONV - 1
    conv_p = jnp.stack([lax.slice(big, ((b + 1) * SEQ - tail, BIG_XBC), ((b + 1) * SEQ, BIG_XBC + SSD_CONV_DIM))
                        for b in range(BATCH)])[None]
    conv_s = jnp.stack([lax.slice(big, (T_PROMPT + (b + 1) * DEC_SEQ - tail, BIG_XBC),
                                  (T_PROMPT + (b + 1) * DEC_SEQ, BIG_XBC + SSD_CONV_DIM))
                        for b in range(DEC_BATCH)])[None]
    ssm_p = ssm_p.reshape(1, BATCH, SSD_HEADS, SSD_HEADDIM, SSD_STATE)
    ssm_s = ssm_s.reshape(1, DEC_BATCH, SSD_HEADS, SSD_HEADDIM, SSD_STATE)
    return (y_prompt, y_sample, ckv_p, kr_p, conv_p, ssm_p, ckv_s, kr_s, conv_s, ssm_s)
```

```python
import functools
import math

import numpy as np
import jax
import jax.numpy as jnp
from jax import lax
from jax.experimental import pallas as pl
from jax.experimental.pallas import tpu as pltpu

F32 = jnp.float32
BF16 = jnp.bfloat16
HI = lax.Precision.HIGHEST

D_MODEL = 4096
BATCH = 2
SEQ = 8192
DEC_BATCH = 8
DEC_SEQ = 32
PAST_LEN = 4096
CHUNK = 64
EPS = 1e-6
MLA_HEADS = 32
Q_LORA = 1024
KV_LORA = 512
QK_NOPE = 128
QK_ROPE = 64
V_HEAD = 128
ROPE_THETA = 10000.0
SCALE = (QK_NOPE + QK_ROPE) ** -0.5
QSCALE = SCALE * math.log2(math.e)
SSD_INNER = 2 * D_MODEL
SSD_HEADDIM = 64
SSD_HEADS = SSD_INNER // SSD_HEADDIM
SSD_STATE = 128
SSD_GROUPS = 8
SSD_RANK = SSD_HEADS // SSD_GROUPS
SSD_GCOLS = SSD_RANK * SSD_HEADDIM
SSD_CONV = 4
SSD_CONV_DIM = SSD_INNER + 2 * SSD_GROUPS * SSD_STATE
N_GROUPS = 8
EXPERTS_PER_GROUP = 8
N_EXPERTS = N_GROUPS * EXPERTS_PER_GROUP
TOP_K = 2
D_EXPERT = 1024

T_PROMPT = BATCH * SEQ
T_SAMPLE = DEC_BATCH * DEC_SEQ
T_ALL = T_PROMPT + T_SAMPLE

OFF_CQ = 0
OFF_CKV = OFF_CQ + Q_LORA
OFF_KR = OFF_CKV + KV_LORA
OFF_Z = OFF_KR + QK_ROPE
OFF_XBC = OFF_Z + SSD_INNER
OFF_DT = OFF_XBC + SSD_CONV_DIM
OFF_GMLA = OFF_DT + SSD_HEADS
OFF_GSSD = OFF_GMLA + D_MODEL
N_IN = OFF_GSSD + D_MODEL

BIG_Z = 0
BIG_XBC = SSD_INNER
BIG_N = BIG_XBC + SSD_CONV_DIM
GATE_MLA = 0
GATE_SSD = D_MODEL
SM_CQ = 0
SM_CKV = Q_LORA
SM_DT = SM_CKV + KV_LORA
SM_KR = SM_DT + SSD_HEADS
SM_KRS = SM_KR + QK_ROPE
SM_N = SM_KRS + QK_ROPE

V7X_VMEM_LIMIT = 56 * 1024 * 1024
ROW_TILE = 256
MM_TM = 1280
MOE_TM = 256
SSD_SUB = 8
MOE_GATHER_ALIGN = 16
MOE_GATHER_ROWS = T_ALL * TOP_K + N_EXPERTS * MOE_GATHER_ALIGN + MOE_TM
MOE_TF = 512
MOE_TN = 2048
ATTN_TQ = 2048
ATTN_TK = 512


def _cparams(sem):
    return pltpu.CompilerParams(dimension_semantics=sem, vmem_limit_bytes=V7X_VMEM_LIMIT)


def _dot_nt(a, b):
    return lax.dot_general(a, b, (((1,), (1,)), ((), ())), preferred_element_type=F32)


def _rms(x, w):
    return x * lax.rsqrt(jnp.mean(x * x, axis=-1, keepdims=True) + EPS) * w


def _sigmoid(x):
    return 0.5 * jnp.tanh(0.5 * x) + 0.5


def _stacked_x_specs():
    n_p = T_PROMPT // ROW_TILE
    return [pl.BlockSpec((ROW_TILE, D_MODEL), lambda i: (jnp.minimum(i, n_p - 1), 0)),
            pl.BlockSpec((ROW_TILE, D_MODEL), lambda i: (0, 0))]


def _stacked_x(xp_ref, xs_ref):
    is_prompt = pl.program_id(0) < T_PROMPT // ROW_TILE
    return jnp.where(is_prompt, xp_ref[...], xs_ref[...])


def _rms_rows_body(xp_ref, xs_ref, w_ref, o_ref):
    o_ref[...] = _rms(_stacked_x(xp_ref, xs_ref), w_ref[...]).astype(o_ref.dtype)


def _rms_rows(x_p, x_s, w, out_dtype):
    d = D_MODEL
    return pl.pallas_call(
        _rms_rows_body,
        grid=(T_ALL // ROW_TILE,),
        in_specs=_stacked_x_specs() + [pl.BlockSpec((1, d), lambda i: (0, 0))],
        out_specs=pl.BlockSpec((ROW_TILE, d), lambda i: (i, 0)),
        out_shape=jax.ShapeDtypeStruct((T_ALL, d), out_dtype),
        compiler_params=_cparams(("parallel",)),
        name="rms_rows",
    )(x_p, x_s, w.reshape(1, d))


def _final_body(h_ref, ya_ref, yb_ref, w_ref, op_ref, os_ref):
    y = ya_ref[...].astype(F32) + yb_ref[...].astype(F32)
    out = _rms(h_ref[...] + y, w_ref[...])
    is_prompt = pl.program_id(0) < T_PROMPT // ROW_TILE

    @pl.when(is_prompt)
    def _():
        op_ref[...] = out

    @pl.when(jnp.logical_not(is_prompt))
    def _():
        os_ref[...] = out


def _final_norm(h, y_a, y_b, w):
    m, d = h.shape
    n_p = T_PROMPT // ROW_TILE
    row = pl.BlockSpec((ROW_TILE, d), lambda i: (i, 0))
    return pl.pallas_call(
        _final_body,
        grid=(m // ROW_TILE,),
        in_specs=[row, row, row, pl.BlockSpec((1, d), lambda i: (0, 0))],
        out_specs=[pl.BlockSpec((ROW_TILE, d), lambda i: (jnp.minimum(i, n_p - 1), 0)),
                   pl.BlockSpec((ROW_TILE, d), lambda i: (0, 0))],
        out_shape=[jax.ShapeDtypeStruct((T_PROMPT, d), F32), jax.ShapeDtypeStruct((T_SAMPLE, d), F32)],
        compiler_params=_cparams(("arbitrary",)),
        name="final_norm",
    )(h, y_a, y_b, w.reshape(1, d))


def _mm_body(epilogue, a_ref, w_ref, *rest):
    o_ref = rest[-1]
    acc = jnp.dot(a_ref[...], w_ref[...], preferred_element_type=F32)
    if epilogue is not None:
        acc = epilogue(acc, *[r[...] for r in rest[:-1]])
    o_ref[...] = acc.astype(o_ref.dtype)


def _mm(a, w, *, tm, tn, out_dtype, extras=(), epilogue=None, name="mm"):
    m, k = a.shape
    n = w.shape[1]
    in_specs = [pl.BlockSpec((tm, k), lambda i, j: (i, 0)),
                pl.BlockSpec((k, tn), lambda i, j: (0, j))]
    args = [a, w]
    for arr, off in extras:
        in_specs.append(pl.BlockSpec((tm, tn), functools.partial(lambda i, j, off: (i, j + off), off=off)))
        args.append(arr)
    return pl.pallas_call(
        functools.partial(_mm_body, epilogue),
        grid=(m // tm, n // tn),
        in_specs=in_specs,
        out_specs=pl.BlockSpec((tm, tn), lambda i, j: (i, j)),
        out_shape=jax.ShapeDtypeStruct((m, n), out_dtype),
        compiler_params=_cparams(("parallel", "parallel")),
        name=name,
    )(*args)


def _mla_prep_body(s_ref, qw_ref, kvw_ref, cos_ref, sin_ref, cqn_ref, ckv_ref, ckvb_ref, kr_ref, krb_ref):
    cqn_ref[...] = _rms(s_ref[:, SM_CQ:SM_CQ + Q_LORA], qw_ref[...]).astype(BF16)
    c = _rms(s_ref[:, SM_CKV:SM_CKV + KV_LORA], kvw_ref[...])
    ckv_ref[...] = c
    ckvb_ref[...] = c.astype(BF16)
    kr = (s_ref[:, SM_KR:SM_KR + QK_ROPE] * cos_ref[...]
          + s_ref[:, SM_KRS:SM_KRS + QK_ROPE] * sin_ref[...])
    kr_ref[...] = kr
    krb_ref[...] = kr.astype(BF16)


def _mla_prep(small, q_norm_w, kv_norm_w, cos2, sin2):
    m = small.shape[0]
    row = lambda n: pl.BlockSpec((ROW_TILE, n), lambda i: (i, 0))
    vec = lambda n: pl.BlockSpec((1, n), lambda i: (0, 0))
    return pl.pallas_call(
        _mla_prep_body,
        grid=(m // ROW_TILE,),
        in_specs=[row(SM_N), vec(Q_LORA), vec(KV_LORA), row(QK_ROPE), row(QK_ROPE)],
        out_specs=[row(Q_LORA), row(KV_LORA), row(KV_LORA), row(QK_ROPE), row(QK_ROPE)],
        out_shape=[jax.ShapeDtypeStruct((m, Q_LORA), BF16),
                   jax.ShapeDtypeStruct((m, KV_LORA), F32),
                   jax.ShapeDtypeStruct((m, KV_LORA), BF16),
                   jax.ShapeDtypeStruct((m, QK_ROPE), F32),
                   jax.ShapeDtypeStruct((m, QK_ROPE), BF16)],
        compiler_params=_cparams(("parallel",)),
        name="mla_prep",
    )(small, q_norm_w.reshape(1, -1), kv_norm_w.reshape(1, -1), cos2, sin2)


def _q_rope_body(a_ref, w_ref, ws_ref, cos_ref, sin_ref, o_ref):
    reps = o_ref.shape[1] // cos_ref.shape[1]
    a = a_ref[...]
    r = jnp.dot(a, w_ref[...], preferred_element_type=F32)
    rs = jnp.dot(a, ws_ref[...], preferred_element_type=F32)
    cos = jnp.tile(cos_ref[...], (1, reps))
    sin = jnp.tile(sin_ref[...], (1, reps))
    o_ref[...] = ((r * cos + rs * sin) * QSCALE).astype(o_ref.dtype)


def _q_rope(cqn, w_r, w_rs, cos128, sin128, *, tm, tn):
    m, k = cqn.shape
    n = w_r.shape[1]
    return pl.pallas_call(
        _q_rope_body,
        grid=(m // tm, n // tn),
        in_specs=[pl.BlockSpec((tm, k), lambda i, j: (i, 0)),
                  pl.BlockSpec((k, tn), lambda i, j: (0, j)),
                  pl.BlockSpec((k, tn), lambda i, j: (0, j)),
                  pl.BlockSpec((tm, 128), lambda i, j: (i, 0)),
                  pl.BlockSpec((tm, 128), lambda i, j: (i, 0))],
        out_specs=pl.BlockSpec((tm, tn), lambda i, j: (i, j)),
        out_shape=jax.ShapeDtypeStruct((m, n), BF16),
        compiler_params=_cparams(("parallel", "parallel")),
        name="q_rope",
    )(cqn, w_r, w_rs, cos128, sin128)


def _mm_t_body(w_ref, a_ref, o_ref):
    o_ref[0] = _dot_nt(w_ref[...], a_ref[...]).astype(o_ref.dtype)


def _mm_t(w_t, a, *, rows, tm, tn, out_dtype, name):
    n, k = w_t.shape
    return pl.pallas_call(
        _mm_t_body,
        grid=(rows // tm, n // tn),
        in_specs=[pl.BlockSpec((tn, k), lambda i, j: (j, 0)),
                  pl.BlockSpec((tm, k), lambda i, j: (i, 0))],
        out_specs=pl.BlockSpec((1, tn, tm), lambda i, j: (i, j, 0)),
        out_shape=jax.ShapeDtypeStruct((rows // tm, n, tm), out_dtype),
        compiler_params=_cparams(("parallel", "parallel")),
        name=name,
    )(w_t, a)


ATTN_KP = 256


ATTN_VP = 144


def _attn_prompt_body(qn_ref, qr_ref, kn_ref, kr_ref, vt_ref, base_ref, o_ref, kf_sc, vf_sc, qf_sc, acc_sc):
    qi = pl.program_id(2)
    tq = qn_ref.shape[0]
    tk = vt_ref.shape[2]
    seq = kn_ref.shape[0]
    pad = ATTN_KP - QK_NOPE - QK_ROPE

    @pl.when(qi == 0)
    def _():
        def fill(i, c):
            r = pl.ds(pl.multiple_of(i * tk, tk), tk)
            kr = kr_ref[r, :]
            for hh in range(2):
                c0 = hh * ATTN_KP
                kf_sc[r, c0:c0 + QK_NOPE] = kn_ref[r, hh * QK_NOPE:(hh + 1) * QK_NOPE]
                kf_sc[r, c0 + QK_NOPE:c0 + QK_NOPE + QK_ROPE] = kr
                kf_sc[r, c0 + QK_NOPE + QK_ROPE:c0 + ATTN_KP] = jnp.zeros((tk, pad), BF16)
                vf_sc[i, hh * ATTN_VP:hh * ATTN_VP + V_HEAD, :] = vt_ref[i, hh * V_HEAD:(hh + 1) * V_HEAD, :]
                vf_sc[i, hh * ATTN_VP + V_HEAD:(hh + 1) * ATTN_VP, :] = jnp.ones((ATTN_VP - V_HEAD, tk), BF16)
            return c
        lax.fori_loop(0, seq // tk, fill, 0)

    for hh in range(2):
        c0 = hh * ATTN_KP
        qf_sc[:, c0:c0 + QK_NOPE] = qn_ref[:, hh * QK_NOPE:(hh + 1) * QK_NOPE]
        qf_sc[:, c0 + QK_NOPE:c0 + QK_NOPE + QK_ROPE] = qr_ref[:, hh * QK_ROPE:(hh + 1) * QK_ROPE]
        qf_sc[:, c0 + QK_NOPE + QK_ROPE:c0 + ATTN_KP] = jnp.zeros((tq, pad), BF16)
    acc_sc[...] = jnp.zeros(acc_sc.shape, F32)

    def block(kb, carry, diag):
        r = pl.ds(pl.multiple_of(kb * tk, tk), tk)
        q0 = 0 if diag is None else diag * tk
        nq_cols = tq - q0
        out = []
        for hh in range(2):
            m = carry[hh][:, q0:]
            st = _dot_nt(kf_sc[r, hh * ATTN_KP:(hh + 1) * ATTN_KP], qf_sc[q0:, hh * ATTN_KP:(hh + 1) * ATTN_KP])
            if diag is not None:
                kc = lax.broadcasted_iota(jnp.int32, (tk, nq_cols), 0) // CHUNK
                qc = lax.broadcasted_iota(jnp.int32, (tk, nq_cols), 1) // CHUNK
                st = jnp.where(kc <= qc, st, -jnp.inf)
            m_new = jnp.maximum(m, jnp.max(st, axis=0, keepdims=True))
            alpha = jnp.exp2(m - m_new)
            pt = jnp.exp2(st - m_new).astype(BF16)
            vt = vf_sc[kb, hh * ATTN_VP:(hh + 1) * ATTN_VP, :]
            acc_sc[hh, :, q0:] = alpha * acc_sc[hh, :, q0:] + jnp.dot(vt, pt, preferred_element_type=F32)
            out.append(m_new if q0 == 0 else jnp.concatenate([carry[hh][:, :q0], m_new], axis=1))
        return tuple(out)

    n_diag = tq // tk
    n_full = qi * n_diag
    init = (jnp.full((1, tq), -jnp.inf, F32),) * 2
    carry = lax.fori_loop(0, n_full, lambda kb, c: block(kb, c, None), init)
    for d in range(n_diag):
        carry = block(n_full + d, carry, d)
    for hh in range(2):
        o = acc_sc[hh, 0:V_HEAD, :] / acc_sc[hh, V_HEAD:V_HEAD + 1, :]
        o_ref[:, hh * V_HEAD:(hh + 1) * V_HEAD] = o.T.astype(o_ref.dtype)


def _attn_prompt(qn, qr, kn, krb, vt, base, *, nb, seq, heads, tq):
    tk = vt.shape[2]
    nq = seq // tq
    nk = seq // tk
    return pl.pallas_call(
        _attn_prompt_body,
        grid=(nb, heads // 2, nq),
        in_specs=[pl.BlockSpec((tq, 2 * QK_NOPE), lambda b, hp, qi: (b * nq + qi, hp)),
                  pl.BlockSpec((tq, 2 * QK_ROPE), lambda b, hp, qi: (b * nq + qi, hp)),
                  pl.BlockSpec((seq, 2 * QK_NOPE), lambda b, hp, qi: (b, hp)),
                  pl.BlockSpec((seq, QK_ROPE), lambda b, hp, qi: (b, 0)),
                  pl.BlockSpec((nk, 2 * V_HEAD, tk), lambda b, hp, qi: (b, hp, 0)),
                  pl.BlockSpec(memory_space=pl.ANY)],
        out_specs=pl.BlockSpec((tq, 2 * V_HEAD), lambda b, hp, qi: (b * nq + qi, hp)),
        out_shape=jax.ShapeDtypeStruct(base.shape, base.dtype),
        input_output_aliases={5: 0},
        scratch_shapes=[pltpu.VMEM((seq, 2 * ATTN_KP), BF16), pltpu.VMEM((nk, 2 * ATTN_VP, tk), BF16),
                        pltpu.VMEM((tq, 2 * ATTN_KP), BF16), pltpu.VMEM((2, ATTN_VP, tq), F32)],
        compiler_params=_cparams(("parallel", "parallel", "arbitrary")),
        name="attn_prompt",
    )(qn, qr, kn, krb, vt, base)


def _bmm_body(a_ref, w_ref, o_ref):
    o_ref[0] = jnp.dot(a_ref[0], w_ref[0], preferred_element_type=F32).astype(o_ref.dtype)


def _bmm(a, w, out_dtype, name):
    h, m, k = a.shape
    n = w.shape[2]
    return pl.pallas_call(
        _bmm_body,
        grid=(h,),
        in_specs=[pl.BlockSpec((1, m, k), lambda i: (i, 0, 0)),
                  pl.BlockSpec((1, k, n), lambda i: (i, 0, 0))],
        out_specs=pl.BlockSpec((1, m, n), lambda i: (i, 0, 0)),
        out_shape=jax.ShapeDtypeStruct((h, m, n), out_dtype),
        compiler_params=_cparams(("parallel",)),
        name=name,
    )(a, w)


def _attn_sample_body(ql_ref, qr_ref, cc_ref, ck_ref, nc_ref, nk_ref, o_ref, ccb_sc, ckb_sc):
    @pl.when(pl.program_id(1) == 0)
    def _():
        ccb_sc[...] = cc_ref[0].astype(BF16)
        ckb_sc[...] = ck_ref[0].astype(BF16)

    ql = ql_ref[0]
    qr = qr_ref[0]
    ncb = nc_ref[...].astype(BF16)
    nkb = nk_ref[...].astype(BF16)
    s1 = _dot_nt(ql, ccb_sc[...]) + _dot_nt(qr, ckb_sc[...])
    s2 = _dot_nt(ql, ncb) + _dot_nt(qr, nkb)
    m = jnp.maximum(jnp.max(s1, axis=1, keepdims=True), jnp.max(s2, axis=1, keepdims=True))
    p1 = jnp.exp2(s1 - m)
    p2 = jnp.exp2(s2 - m)
    l = jnp.sum(p1, axis=1, keepdims=True) + jnp.sum(p2, axis=1, keepdims=True)
    o = (jnp.dot(p1.astype(BF16), ccb_sc[...], preferred_element_type=F32)
         + jnp.dot(p2.astype(BF16), ncb, preferred_element_type=F32))
    o_ref[0] = (o / l).astype(o_ref.dtype)


def _attn_sample(q_lat, q_rope, cache_ckv, cache_krope, ckv, krope):
    rows = q_lat.shape[1]
    tr = 256
    new_blk = T_PROMPT // DEC_SEQ
    return pl.pallas_call(
        _attn_sample_body,
        grid=(DEC_BATCH, rows // tr),
        in_specs=[pl.BlockSpec((1, tr, KV_LORA), lambda b, r: (b, r, 0)),
                  pl.BlockSpec((1, tr, QK_ROPE), lambda b, r: (b, r, 0)),
                  pl.BlockSpec((1, PAST_LEN, KV_LORA), lambda b, r: (b, 0, 0)),
                  pl.BlockSpec((1, PAST_LEN, QK_ROPE), lambda b, r: (b, 0, 0)),
                  pl.BlockSpec((DEC_SEQ, KV_LORA), lambda b, r: (new_blk + b, 0)),
                  pl.BlockSpec((DEC_SEQ, QK_ROPE), lambda b, r: (new_blk + b, 0))],
        out_specs=pl.BlockSpec((1, tr, KV_LORA), lambda b, r: (b, r, 0)),
        out_shape=jax.ShapeDtypeStruct((DEC_BATCH, rows, KV_LORA), BF16),
        scratch_shapes=[pltpu.VMEM((PAST_LEN, KV_LORA), BF16), pltpu.VMEM((PAST_LEN, QK_ROPE), BF16)],
        compiler_params=_cparams(("parallel", "arbitrary")),
        name="attn_sample",
    )(q_lat, q_rope, cache_ckv, cache_krope, ckv, krope)


def _conv_body(x_ref, prev_ref, w_ref, b_ref, o_ref, ext_sc):
    tl = x_ref.shape[0]

    @pl.when(pl.program_id(2) == 0)
    def _():
        ext_sc[0:8, :] = prev_ref[0]

    @pl.when(pl.program_id(2) != 0)
    def _():
        ext_sc[0:8, :] = ext_sc[tl:tl + 8, :]

    ext_sc[8:8 + tl, :] = x_ref[...]
    acc = b_ref[...] + ext_sc[5:5 + tl, :] * w_ref[0:1, :]
    for k in range(1, SSD_CONV):
        acc = acc + ext_sc[5 + k:5 + k + tl, :] * w_ref[k:k + 1, :]
    o_ref[...] = (acc * _sigmoid(acc)).astype(o_ref.dtype)


def _conv_silu(big, prev8, conv_w, conv_b, *, nb, seq, tl, row_off):
    tc = 1024
    nrt = seq // tl
    rb0 = row_off // tl
    cb0 = BIG_XBC // tc
    return pl.pallas_call(
        _conv_body,
        grid=(nb, SSD_CONV_DIM // tc, nrt),
        in_specs=[pl.BlockSpec((tl, tc), lambda b, j, r: (rb0 + b * nrt + r, cb0 + j)),
                  pl.BlockSpec((1, 8, tc), lambda b, j, r: (b, 0, j)),
                  pl.BlockSpec((SSD_CONV, tc), lambda b, j, r: (0, j)),
                  pl.BlockSpec((1, tc), lambda b, j, r: (0, j))],
        out_specs=pl.BlockSpec((tl, tc), lambda b, j, r: (b * nrt + r, j)),
        out_shape=jax.ShapeDtypeStruct((nb * seq, SSD_CONV_DIM), BF16),
        scratch_shapes=[pltpu.VMEM((tl + 8, tc), F32)],
        compiler_params=_cparams(("parallel", "parallel", "arbitrary")),
        name="conv_silu",
    )(big, prev8, conv_w, conv_b.reshape(1, -1))


def _softplus(x):
    return jnp.maximum(x, 0.0) + jnp.log(1.0 + jnp.exp(-jnp.abs(x)))


def _dot_sel(sel, a, *, sel_left):
    hi = a.astype(BF16)
    r1 = a - hi.astype(F32)
    mid = r1.astype(BF16)
    lo = (r1 - mid.astype(F32)).astype(BF16)
    out = None
    for term in (hi, mid, lo):
        d = (jnp.dot(sel, term, preferred_element_type=F32) if sel_left
             else jnp.dot(term, sel, preferred_element_type=F32))
        out = d if out is None else out + d
    return out


def _ssd_body(has_h0, x_ref, b_ref, c_ref, z_ref, dt_ref, bias_ref, al_ref, dsk_ref, nw_ref, *rest):
    h0_ref = rest[0] if has_h0 else None
    y_ref, hout_ref, ht_sc = rest[-3:]
    R, P = SSD_RANK, SSD_HEADDIM
    lc = P
    W = 2 * P
    c_idx = pl.program_id(2)

    @pl.when(c_idx == 0)
    def _():
        if has_h0:
            ht_sc[...] = h0_ref[0].T
        else:
            ht_sc[...] = jnp.zeros(ht_sc.shape, F32)

    ri = lax.broadcasted_iota(jnp.int32, (lc, lc), 0)
    ci = lax.broadcasted_iota(jnp.int32, (lc, lc), 1)
    tril_b = jnp.where(ri >= ci, 1.0, 0.0).astype(BF16)
    er = lax.broadcasted_iota(jnp.int32, (R, R * P), 0)
    ec = lax.broadcasted_iota(jnp.int32, (R, R * P), 1) // P
    expand = jnp.where(er == ec, 1.0, 0.0).astype(BF16)
    li = lax.broadcasted_iota(jnp.int32, (lc, R * P), 0)
    si = lax.broadcasted_iota(jnp.int32, (lc, R * P), 1) % P
    first_head = lax.broadcasted_iota(jnp.int32, (lc, W), 1) < P
    neg_a = -jnp.exp(al_ref[0])

    ht = ht_sc[...]
    for k in range(x_ref.shape[0] // lc):
        rows = slice(k * lc, (k + 1) * lc)
        dt = _softplus(dt_ref[0, rows, :] + bias_ref[0])
        a_cs = _dot_sel(tril_b, dt * neg_a, sel_left=True)
        a_exp = _dot_sel(expand, a_cs, sel_left=False)
        dt_exp = _dot_sel(expand, dt, sel_left=False)
        a_last = a_exp[lc - 1:lc, :]
        a_key = jnp.sum(jnp.where(li == si, a_exp, 0.0), axis=0, keepdims=True)
        decay = jnp.exp(jnp.where(li >= si, a_exp - a_key, -jnp.inf))

        x = x_ref[rows, :].astype(F32)
        xdt = x * dt_exp
        bm = b_ref[rows, :].astype(BF16)
        cm = c_ref[rows, :].astype(BF16)
        cb2 = _dot_nt(cm, jnp.concatenate([bm, bm], axis=0))
        m_all = (jnp.tile(cb2, (1, R // 2)) * decay).astype(BF16)
        y_off = jnp.dot(cm, ht.astype(BF16), preferred_element_type=F32) * jnp.exp(a_exp)
        xdt_b = xdt.astype(BF16)
        y_diag = []
        for j in range(R // 2):
            xp = xdt_b[:, j * W:(j + 1) * W]
            zero = jnp.zeros_like(xp)
            stack = jnp.concatenate([jnp.where(first_head, xp, zero), jnp.where(first_head, zero, xp)], axis=0)
            y_diag.append(jnp.dot(m_all[:, j * W:(j + 1) * W], stack, preferred_element_type=F32))
        wgt = (xdt * jnp.exp(a_last - a_exp)).astype(BF16)
        states_t = lax.dot_general(bm, wgt, (((0,), (0,)), ((), ())), preferred_element_type=F32)
        ht = ht * jnp.exp(a_last) + states_t

        y = jnp.concatenate(y_diag, axis=1) + y_off + x * dsk_ref[...]
        z = z_ref[rows, :]
        y = y * (z * _sigmoid(z))
        y = y * lax.rsqrt(jnp.mean(y * y, axis=-1, keepdims=True) + EPS) * nw_ref[...]
        y_ref[rows, :] = y.astype(y_ref.dtype)
    ht_sc[...] = ht

    @pl.when(c_idx == pl.num_programs(2) - 1)
    def _():
        hout_ref[0] = ht.T


def _ssd(xbc, zsrc, dt_g, h0, dt_bias, a_log, d_skip, ssd_norm_w, *, nb, seq, sub, base=None):
    lc = sub * SSD_HEADDIM
    nc = seq // lc
    G, R, GC, N = SSD_GROUPS, SSD_RANK, SSD_GCOLS, SSD_STATE
    row = lambda b, g, c: b * nc + c
    in_specs = [
        pl.BlockSpec((lc, GC), lambda b, g, c: (row(b, g, c), g)),
        pl.BlockSpec((lc, N), lambda b, g, c: (row(b, g, c), SSD_INNER // N + g)),
        pl.BlockSpec((lc, N), lambda b, g, c: (row(b, g, c), SSD_INNER // N + G + g)),
        pl.BlockSpec((lc, GC), lambda b, g, c: (row(b, g, c), g)),
        pl.BlockSpec((1, lc, R), lambda b, g, c: (g, row(b, g, c), 0)),
        pl.BlockSpec((1, 1, R), lambda b, g, c: (g, 0, 0)),
        pl.BlockSpec((1, 1, R), lambda b, g, c: (g, 0, 0)),
        pl.BlockSpec((1, GC), lambda b, g, c: (0, g)),
        pl.BlockSpec((1, GC), lambda b, g, c: (0, g)),
    ]
    args = [xbc, xbc, xbc, zsrc, dt_g, dt_bias.reshape(G, 1, R), a_log.reshape(G, 1, R),
            jnp.repeat(d_skip, SSD_HEADDIM).reshape(1, SSD_INNER), ssd_norm_w.reshape(1, SSD_INNER)]
    if h0 is not None:
        in_specs.append(pl.BlockSpec((1, GC, N), lambda b, g, c: (b, g, 0)))
        args.append(h0)
    aliases = {}
    y_shape = jax.ShapeDtypeStruct((nb * seq, SSD_INNER), BF16)
    if base is not None:
        aliases = {len(args): 0}
        in_specs.append(pl.BlockSpec(memory_space=pl.ANY))
        args.append(base)
        y_shape = jax.ShapeDtypeStruct(base.shape, base.dtype)
    return pl.pallas_call(
        functools.partial(_ssd_body, h0 is not None),
        grid=(nb, G, nc),
        in_specs=in_specs,
        out_specs=[pl.BlockSpec((lc, GC), lambda b, g, c: (row(b, g, c), g)),
                   pl.BlockSpec((1, GC, N), lambda b, g, c: (b, g, 0))],
        out_shape=[y_shape, jax.ShapeDtypeStruct((nb, SSD_HEADS * SSD_HEADDIM, N), F32)],
        input_output_aliases=aliases,
        scratch_shapes=[pltpu.VMEM((N, GC), F32)],
        compiler_params=_cparams(("parallel", "parallel", "arbitrary")),
        name="ssd",
    )(*args)


def _norm_router_body(xp_ref, xs_ref, mix_ref, w_ref, wr_ref, br_ref, h_ref, xn_ref, lg_ref):
    h = _stacked_x(xp_ref, xs_ref) + mix_ref[...]
    h_ref[...] = h
    xn = _rms(h, w_ref[...])
    xn_ref[...] = xn.astype(BF16)
    lg_ref[...] = jnp.dot(xn, wr_ref[...], precision=HI, preferred_element_type=F32) + br_ref[...]


def _norm_router(x_p, x_s, mix, norm_w, w_router, b_router):
    m, d = mix.shape
    n = w_router.shape[1]
    row = pl.BlockSpec((ROW_TILE, d), lambda i: (i, 0))
    return pl.pallas_call(
        _norm_router_body,
        grid=(m // ROW_TILE,),
        in_specs=_stacked_x_specs() + [row,
                                       pl.BlockSpec((1, d), lambda i: (0, 0)),
                                       pl.BlockSpec((d, n), lambda i: (0, 0)),
                                       pl.BlockSpec((1, n), lambda i: (0, 0))],
        out_specs=[row, row, pl.BlockSpec((ROW_TILE, n), lambda i: (i, 0))],
        out_shape=[jax.ShapeDtypeStruct((m, d), F32), jax.ShapeDtypeStruct((m, d), BF16),
                   jax.ShapeDtypeStruct((m, n), F32)],
        compiler_params=_cparams(("parallel",)),
        name="norm_router",
    )(x_p, x_s, mix, norm_w.reshape(1, d), w_router, b_router)


def _moe_up_body(be_ref, nused_ref, win_ref, x_ref, wg_ref, wu_ref, o_ref):
    blk = pl.program_id(1)

    @pl.when(blk < nused_ref[0])
    def _():
        x = x_ref[...]
        g = jnp.dot(x, wg_ref[0].astype(BF16), preferred_element_type=F32)
        u = jnp.dot(x, wu_ref[0].astype(BF16), preferred_element_type=F32)
        o_ref[...] = (g * _sigmoid(g) * u).astype(o_ref.dtype)

    @pl.when(blk >= nused_ref[0])
    def _():
        o_ref[...] = jnp.zeros(o_ref.shape, o_ref.dtype)


def _moe_up(be, nused, win, xs, w_gate, w_up):
    d = xs.shape[1]
    nblk = be.shape[0]
    grid_spec = pltpu.PrefetchScalarGridSpec(
        num_scalar_prefetch=3,
        grid=(D_EXPERT // MOE_TF, nblk),
        in_specs=[pl.BlockSpec((pl.Element(MOE_TM), pl.Element(d)),
                               lambda f, i, be, nu, win: (win[i] * MOE_GATHER_ALIGN, 0)),
                  pl.BlockSpec((1, d, MOE_TF), lambda f, i, be, nu, win: (be[i], 0, f)),
                  pl.BlockSpec((1, d, MOE_TF), lambda f, i, be, nu, win: (be[i], 0, f))],
        out_specs=pl.BlockSpec((MOE_TM, MOE_TF), lambda f, i, be, nu, win: (i, f)),
    )
    return pl.pallas_call(
        _moe_up_body,
        grid_spec=grid_spec,
        out_shape=jax.ShapeDtypeStruct((nblk * MOE_TM, D_EXPERT), BF16),
        compiler_params=_cparams(("arbitrary", "arbitrary")),
        name="moe_up",
    )(be, nused, win, xs, w_gate, w_up)


def _moe_down_body(be_ref, nused_ref, h_ref, wd_ref, rw_ref, o_ref):
    blk = pl.program_id(1)

    @pl.when(blk < nused_ref[0])
    def _():
        y = jnp.dot(h_ref[...], wd_ref[0].astype(BF16), preferred_element_type=F32)
        o_ref[...] = (y * rw_ref[...]).astype(o_ref.dtype)

    @pl.when(blk >= nused_ref[0])
    def _():
        o_ref[...] = jnp.zeros(o_ref.shape, o_ref.dtype)


def _moe_down(be, nused, hact, w_down, row_w):
    p, f = hact.shape
    d = w_down.shape[2]
    nblk = p // MOE_TM
    grid_spec = pltpu.PrefetchScalarGridSpec(
        num_scalar_prefetch=2,
        grid=(d // MOE_TN, nblk),
        in_specs=[pl.BlockSpec((MOE_TM, f), lambda n, i, be, nu: (i, 0)),
                  pl.BlockSpec((1, f, MOE_TN), lambda n, i, be, nu: (be[i], 0, n)),
                  pl.BlockSpec((MOE_TM, 1), lambda n, i, be, nu: (i, 0))],
        out_specs=pl.BlockSpec((MOE_TM, MOE_TN), lambda n, i, be, nu: (i, n)),
    )
    return pl.pallas_call(
        _moe_down_body,
        grid_spec=grid_spec,
        out_shape=jax.ShapeDtypeStruct((p, d), BF16),
        compiler_params=_cparams(("arbitrary", "arbitrary")),
        name="moe_down",
    )(be, nused, hact, w_down, row_w)


def _route(logits):
    t = logits.shape[0]
    g_logits = logits[:, :N_GROUPS]
    g_sel = jnp.argmax(g_logits, axis=-1)
    g_w = jnp.max(jax.nn.softmax(g_logits, axis=-1), axis=-1)
    e_logits = logits[:, N_GROUPS:N_GROUPS + N_EXPERTS].reshape(t, N_GROUPS, EXPERTS_PER_GROUP)
    e_in = jnp.take_along_axis(e_logits, g_sel[:, None, None], axis=1)[:, 0]
    e_val, e_idx = lax.top_k(e_in, TOP_K)
    e_w = jax.nn.softmax(e_val, axis=-1) * g_w[:, None]
    expert_id = (g_sel[:, None] * EXPERTS_PER_GROUP + e_idx).reshape(-1).astype(jnp.int32)
    a = t * TOP_K
    e_sorted, order = lax.sort((expert_id, jnp.arange(a, dtype=jnp.int32)), num_keys=1, is_stable=True)
    experts = jnp.arange(N_EXPERTS, dtype=jnp.int32)
    start = jnp.searchsorted(e_sorted, experts, side='left').astype(jnp.int32)
    counts = jnp.searchsorted(e_sorted, experts, side='right').astype(jnp.int32) - start
    padded = (counts + MOE_TM - 1) // MOE_TM * MOE_TM
    pad_end = jnp.cumsum(padded)
    pad_start = pad_end - padded
    shift = pad_start - start
    jumps = shift - jnp.concatenate([jnp.zeros((1,), jnp.int32), shift[:-1]])
    marks = jnp.zeros((a + 1,), jnp.int32).at[start].add(jumps)
    dest = jnp.arange(a, dtype=jnp.int32) + jnp.cumsum(marks)[:a]
    pos = lax.sort((order, dest), num_keys=1)[1].reshape(t, TOP_K)
    nblk = a // MOE_TM + N_EXPERTS
    nused = (pad_end[-1] // MOE_TM).astype(jnp.int32)
    blk = jnp.arange(nblk, dtype=jnp.int32)
    be = jnp.minimum(jnp.searchsorted(pad_end, blk * MOE_TM, side='right'), N_EXPERTS - 1).astype(jnp.int32)
    be = jnp.where(blk < nused, be, be[jnp.maximum(nused - 1, 0)])
    per_row = lambda v: jnp.broadcast_to(v[be][:, None], (nblk, MOE_TM)).reshape(-1)
    rank = jnp.arange(nblk * MOE_TM, dtype=jnp.int32) - per_row(pad_start)
    valid = rank < per_row(counts)
    src = jnp.clip(per_row(start) + rank, 0, a - 1)
    row_w = jnp.where(valid, e_w.reshape(-1)[order[src]], 0.0)
    g = MOE_GATHER_ALIGN
    tight = (counts + g - 1) // g * g
    tight_end = jnp.cumsum(tight)
    tight_start = tight_end - tight
    n_mini = MOE_GATHER_ROWS // g
    mini_e = jnp.minimum(jnp.searchsorted(tight_end, jnp.arange(n_mini, dtype=jnp.int32) * g, side='right'),
                         N_EXPERTS - 1).astype(jnp.int32)
    per_mini = lambda v: jnp.broadcast_to(v[mini_e][:, None], (n_mini, g)).reshape(-1)
    grank = jnp.arange(MOE_GATHER_ROWS, dtype=jnp.int32) - per_mini(tight_start)
    gvalid = (grank >= 0) & (grank < per_mini(counts))
    gather_token = jnp.where(gvalid, order[jnp.clip(per_mini(start) + grank, 0, a - 1)] // TOP_K, 0)
    win = (tight_start[be] + blk * MOE_TM - pad_start[be]) // g
    win = jnp.where(blk < nused, win, 0).astype(jnp.int32)
    return gather_token, win, row_w, pos, be, nused.reshape(1)


def _rope_tables():
    half = QK_ROPE // 2
    inv_freq = ROPE_THETA ** (-jnp.arange(half, dtype=F32) / half)
    pos = jnp.concatenate([jnp.tile(jnp.arange(SEQ), BATCH),
                           jnp.tile(PAST_LEN + jnp.arange(DEC_SEQ), DEC_BATCH)]).astype(F32)
    ang = pos[:, None] * inv_freq[None, :]
    cos, sin = jnp.cos(ang), jnp.sin(ang)
    cos2 = jnp.concatenate([cos, cos], axis=1)
    sin2 = jnp.concatenate([-sin, sin], axis=1)
    return cos2, sin2


def kernel(x_prompt, x_sample, cache_ckv, cache_krope, state_conv, state_ssm, norm1_w, w_in, q_norm_w, kv_norm_w, w_uq, w_ukv, conv_w, conv_b, dt_bias, a_log, d_skip, ssd_norm_w, w_mla_o, w_ssd_o, w_out, norm2_w, w_group, b_group, w_erouter, b_erouter, w_gate, w_up, w_down, final_norm_w):
    swap = np.concatenate([np.arange(QK_ROPE // 2, QK_ROPE), np.arange(QK_ROPE // 2)])
    x_p2 = x_prompt.reshape(T_PROMPT, D_MODEL)
    x_s2 = x_sample.reshape(T_SAMPLE, D_MODEL)
    cos2, sin2 = _rope_tables()
    cos128, sin128 = jnp.tile(cos2, (1, 2)), jnp.tile(sin2, (1, 2))

    wi = w_in[0]
    w_kr = wi[:, OFF_KR:OFF_KR + QK_ROPE]
    w_small = jnp.concatenate([wi[:, OFF_CQ:OFF_KR], wi[:, OFF_DT:OFF_DT + SSD_HEADS], w_kr, w_kr[:, swap]],
                              axis=1).astype(BF16)
    w_big = wi[:, OFF_Z:OFF_DT].astype(BF16)
    w_gates = wi[:, OFF_GMLA:].astype(BF16)
    wq = w_uq[0].reshape(Q_LORA, MLA_HEADS, QK_NOPE + QK_ROPE)
    wq_nope = wq[:, :, :QK_NOPE].reshape(Q_LORA, MLA_HEADS * QK_NOPE).astype(BF16)
    wq_rope = wq[:, :, QK_NOPE:]
    wq_r = wq_rope.reshape(Q_LORA, MLA_HEADS * QK_ROPE).astype(BF16)
    wq_rs = wq_rope[:, :, swap].reshape(Q_LORA, MLA_HEADS * QK_ROPE).astype(BF16)
    wkv3 = w_ukv[0].reshape(KV_LORA, MLA_HEADS, QK_NOPE + V_HEAD)
    w_uk_all = wkv3[:, :, :QK_NOPE].reshape(KV_LORA, MLA_HEADS * QK_NOPE).astype(BF16)
    w_uv_all_t = jnp.transpose(wkv3[:, :, QK_NOPE:], (1, 2, 0)).reshape(MLA_HEADS * V_HEAD, KV_LORA).astype(BF16)
    w_uk_t = jnp.transpose(wkv3[:, :, :QK_NOPE], (1, 2, 0)).astype(BF16)
    w_uv_h = jnp.transpose(wkv3[:, :, QK_NOPE:], (1, 0, 2)).astype(BF16)

    u = _rms_rows(x_p2, x_s2, norm1_w[0], BF16)
    small = _mm(u, w_small, tm=MM_TM, tn=SM_N // 2, out_dtype=F32, name="proj_small")
    big = _mm(u, w_big, tm=MM_TM, tn=512, out_dtype=F32, name="proj_big")
    gates = _mm(u, w_gates, tm=MM_TM, tn=512, out_dtype=F32, name="proj_gates")

    cqn, ckv, ckv_b, krope, krope_b = _mla_prep(small, q_norm_w[0], kv_norm_w[0], cos2, sin2)
    qn = _mm(cqn, wq_nope, tm=MM_TM, tn=1024, out_dtype=BF16, epilogue=lambda acc: acc * QSCALE, name="q_nope")
    qr = _q_rope(cqn, wq_r, wq_rs, cos128, sin128, tm=MM_TM, tn=1024)
    kn = _mm(ckv_b, w_uk_all, tm=MM_TM, tn=1024, out_dtype=BF16, name="k_up")
    vt = _mm_t(w_uv_all_t, ckv_b, rows=T_PROMPT, tm=ATTN_TK, tn=1024, out_dtype=BF16, name="v_up_t")

    qn_s = qn[T_PROMPT:].reshape(T_SAMPLE, MLA_HEADS, QK_NOPE).transpose(1, 0, 2)
    q_lat = _bmm(qn_s, w_uk_t, BF16, "q_absorb")
    q_lat = q_lat.reshape(MLA_HEADS, DEC_BATCH, DEC_SEQ, KV_LORA).transpose(1, 0, 2, 3)
    q_lat = q_lat.reshape(DEC_BATCH, MLA_HEADS * DEC_SEQ, KV_LORA)
    qr_s = qr[T_PROMPT:].reshape(DEC_BATCH, DEC_SEQ, MLA_HEADS, QK_ROPE).transpose(0, 2, 1, 3)
    qr_s = qr_s.reshape(DEC_BATCH, MLA_HEADS * DEC_SEQ, QK_ROPE)
    o_lat = _attn_sample(q_lat, qr_s, cache_ckv[0], cache_krope[0], ckv, krope)
    o_lat = o_lat.reshape(DEC_BATCH, MLA_HEADS, DEC_SEQ, KV_LORA).transpose(1, 0, 2, 3)
    o_lat = o_lat.reshape(MLA_HEADS, T_SAMPLE, KV_LORA)
    o_mla_s = _bmm(o_lat, w_uv_h, BF16, "v_absorb")
    o_mla_s = o_mla_s.transpose(1, 0, 2).reshape(T_SAMPLE, MLA_HEADS * V_HEAD)
    o_mla = _attn_prompt(qn, qr, kn, krope_b, vt, jnp.pad(o_mla_s, ((T_PROMPT, 0), (0, 0))),
                         nb=BATCH, seq=SEQ, heads=MLA_HEADS, tq=ATTN_TQ)

    prev_p = jnp.zeros((BATCH, 8, SSD_CONV_DIM), F32)
    prev_s = jnp.concatenate([jnp.zeros((DEC_BATCH, 8 - (SSD_CONV - 1), SSD_CONV_DIM), F32), state_conv[0]], axis=1)
    xbc_p = _conv_silu(big, prev_p, conv_w[0], conv_b[0], nb=BATCH, seq=SEQ, tl=512, row_off=0)
    xbc_s = _conv_silu(big, prev_s, conv_w[0], conv_b[0], nb=DEC_BATCH, seq=DEC_SEQ, tl=DEC_SEQ, row_off=T_PROMPT)
    dt_raw = small[:, SM_DT:SM_DT + SSD_HEADS]
    by_group = lambda d: d.reshape(d.shape[0], SSD_GROUPS, SSD_RANK).transpose(1, 0, 2)

    def pad_seq(a, fill):
        a = a.reshape(DEC_BATCH, DEC_SEQ, a.shape[-1])
        a = jnp.pad(a, ((0, 0), (0, CHUNK - DEC_SEQ), (0, 0)), constant_values=fill)
        return a.reshape(DEC_BATCH * CHUNK, a.shape[-1])

    h0_s = state_ssm[0].reshape(DEC_BATCH, SSD_HEADS * SSD_HEADDIM, SSD_STATE)
    y_s, ssm_s = _ssd(pad_seq(xbc_s, 0.0), pad_seq(big[T_PROMPT:, BIG_Z:BIG_Z + SSD_INNER], 0.0),
                      by_group(pad_seq(dt_raw[T_PROMPT:], -jnp.inf)), h0_s, dt_bias[0], a_log[0], d_skip[0],
                      ssd_norm_w[0], nb=DEC_BATCH, seq=CHUNK, sub=1)
    y_s = y_s.reshape(DEC_BATCH, CHUNK, SSD_INNER)[:, :DEC_SEQ].reshape(T_SAMPLE, SSD_INNER)
    o_ssd, ssm_p = _ssd(xbc_p, big, by_group(dt_raw[:T_PROMPT]), None, dt_bias[0], a_log[0], d_skip[0],
                        ssd_norm_w[0], nb=BATCH, seq=SEQ, sub=SSD_SUB, base=jnp.pad(y_s, ((T_PROMPT, 0), (0, 0))))

    gate = lambda acc, g: _sigmoid(g) * acc
    m1 = _mm(o_mla, w_mla_o[0].astype(BF16), tm=640, tn=1024, out_dtype=F32,
             extras=[(gates, GATE_MLA // 1024)], epilogue=gate, name="mla_out")
    merged = _mm(o_ssd, w_ssd_o[0].astype(BF16), tm=640, tn=512, out_dtype=BF16,
                 extras=[(gates, GATE_SSD // 512), (m1, 0)],
                 epilogue=lambda acc, g, m: m + _sigmoid(g) * acc, name="ssd_out")
    mix = _mm(merged, w_out[0].astype(BF16), tm=640, tn=1024, out_dtype=F32, name="out_proj")

    n_r = 128
    w_router = jnp.concatenate([w_group[0], w_erouter[0],
                                jnp.zeros((D_MODEL, n_r - N_GROUPS - N_EXPERTS), F32)], axis=1)
    b_router = jnp.concatenate([b_group[0], b_erouter[0], jnp.zeros((n_r - N_GROUPS - N_EXPERTS,), F32)])
    h, xn, logits = _norm_router(x_p2, x_s2, mix, norm2_w[0], w_router, b_router.reshape(1, n_r))
    gather_token, win, row_w, pos, be, nused = _route(logits)
    xs = jnp.take(xn, gather_token, axis=0, mode="clip")
    hact = _moe_up(be, nused, win, xs, w_gate[0], w_up[0])
    yb = _moe_down(be, nused, hact, w_down[0], row_w.reshape(-1, 1))
    y_p2, y_s2 = _final_norm(h, jnp.take(yb, pos[:, 0], axis=0, mode="clip"),
                             jnp.take(yb, pos[:, 1], axis=0, mode="clip"), final_norm_w)

    y_prompt = y_p2.reshape(BATCH, SEQ, D_MODEL)
    y_sample = y_s2.reshape(DEC_BATCH, DEC_SEQ, D_MODEL)
    ckv_p = ckv[:T_PROMPT].reshape(1, BATCH, SEQ, KV_LORA)
    ckv_s = ckv[T_PROMPT:].reshape(1, DEC_BATCH, DEC_SEQ, KV_LORA)
    kr_p = krope[:T_PROMPT].reshape(1, BATCH, SEQ, QK_ROPE)
    kr_s = krope[T_PROMPT:].reshape(1, DEC_BATCH, DEC_SEQ, QK_ROPE)
    tail = SSD_CONV - 1
    conv_p = jnp.stack([lax.slice(big, ((b + 1) * SEQ - tail, BIG_XBC), ((b + 1) * SEQ, BIG_XBC + SSD_CONV_DIM))
                        for b in range(BATCH)])[None]
    conv_s = jnp.stack([lax.slice(big, (T_PROMPT + (b + 1) * DEC_SEQ - tail, BIG_XBC),
                                  (T_PROMPT + (b + 1) * DEC_SEQ, BIG_XBC + SSD_CONV_DIM))
                        for b in range(DEC_BATCH)])[None]
    ssm_p = ssm_p.reshape(1, BATCH, SSD_HEADS, SSD_HEADDIM, SSD_STATE)
    ssm_s = ssm_s.reshape(1, DEC_BATCH, SSD_HEADS, SSD_HEADDIM, SSD_STATE)
    return (y_prompt, y_sample, ckv_p, kr_p, conv_p, ssm_p, ckv_s, kr_s, conv_s, ssm_s)
```

```python
import functools
import math

import numpy as np
import jax
import jax.numpy as jnp
from jax import lax
from jax.experimental import pallas as pl
from jax.experimental.pallas import tpu as pltpu

F32 = jnp.float32
BF16 = jnp.bfloat16
HI = lax.Precision.HIGHEST

D_MODEL = 4096
BATCH = 2
SEQ = 8192
DEC_BATCH = 8
DEC_SEQ = 32
PAST_LEN = 4096
CHUNK = 64
EPS = 1e-6
MLA_HEADS = 32
Q_LORA = 1024
KV_LORA = 512
QK_NOPE = 128
QK_ROPE = 64
V_HEAD = 128
ROPE_THETA = 10000.0
SCALE = (QK_NOPE + QK_ROPE) ** -0.5
QSCALE = SCALE * math.log2(math.e)
SSD_INNER = 2 * D_MODEL
SSD_HEADDIM = 64
SSD_HEADS = SSD_INNER // SSD_HEADDIM
SSD_STATE = 128
SSD_GROUPS = 8
SSD_RANK = SSD_HEADS // SSD_GROUPS
SSD_GCOLS = SSD_RANK * SSD_HEADDIM
SSD_CONV = 4
SSD_CONV_DIM = SSD_INNER + 2 * SSD_GROUPS * SSD_STATE
N_GROUPS = 8
EXPERTS_PER_GROUP = 8
N_EXPERTS = N_GROUPS * EXPERTS_PER_GROUP
TOP_K = 2
D_EXPERT = 1024

T_PROMPT = BATCH * SEQ
T_SAMPLE = DEC_BATCH * DEC_SEQ
T_ALL = T_PROMPT + T_SAMPLE

OFF_CQ = 0
OFF_CKV = OFF_CQ + Q_LORA
OFF_KR = OFF_CKV + KV_LORA
OFF_Z = OFF_KR + QK_ROPE
OFF_XBC = OFF_Z + SSD_INNER
OFF_DT = OFF_XBC + SSD_CONV_DIM
OFF_GMLA = OFF_DT + SSD_HEADS
OFF_GSSD = OFF_GMLA + D_MODEL
N_IN = OFF_GSSD + D_MODEL

BIG_Z = 0
BIG_XBC = SSD_INNER
BIG_N = BIG_XBC + SSD_CONV_DIM
GATE_MLA = 0
GATE_SSD = D_MODEL
SM_CQ = 0
SM_CKV = Q_LORA
SM_DT = SM_CKV + KV_LORA
SM_KR = SM_DT + SSD_HEADS
SM_KRS = SM_KR + QK_ROPE
SM_N = SM_KRS + QK_ROPE

V7X_VMEM_LIMIT = 56 * 1024 * 1024
ROW_TILE = 256
MM_TM = 1280
MOE_TM = 256
SSD_SUB = 8
MOE_GATHER_ALIGN = 16
MOE_GATHER_ROWS = T_ALL * TOP_K + N_EXPERTS * MOE_GATHER_ALIGN + MOE_TM
MOE_TF = 512
MOE_TN = 2048
ATTN_TQ = 2048
ATTN_TK = 512


def _cparams(sem):
    return pltpu.CompilerParams(dimension_semantics=sem, vmem_limit_bytes=V7X_VMEM_LIMIT)


def _dot_nt(a, b):
    return lax.dot_general(a, b, (((1,), (1,)), ((), ())), preferred_element_type=F32)


def _rms(x, w):
    return x * lax.rsqrt(jnp.mean(x * x, axis=-1, keepdims=True) + EPS) * w


def _sigmoid(x):
    return 0.5 * jnp.tanh(0.5 * x) + 0.5


def _stacked_x_specs():
    n_p = T_PROMPT // ROW_TILE
    return [pl.BlockSpec((ROW_TILE, D_MODEL), lambda i: (jnp.minimum(i, n_p - 1), 0)),
            pl.BlockSpec((ROW_TILE, D_MODEL), lambda i: (0, 0))]


def _stacked_x(xp_ref, xs_ref):
    is_prompt = pl.program_id(0) < T_PROMPT // ROW_TILE
    return jnp.where(is_prompt, xp_ref[...], xs_ref[...])


def _rms_rows_body(xp_ref, xs_ref, w_ref, o_ref):
    o_ref[...] = _rms(_stacked_x(xp_ref, xs_ref), w_ref[...]).astype(o_ref.dtype)


def _rms_rows(x_p, x_s, w, out_dtype):
    d = D_MODEL
    return pl.pallas_call(
        _rms_rows_body,
        grid=(T_ALL // ROW_TILE,),
        in_specs=_stacked_x_specs() + [pl.BlockSpec((1, d), lambda i: (0, 0))],
        out_specs=pl.BlockSpec((ROW_TILE, d), lambda i: (i, 0)),
        out_shape=jax.ShapeDtypeStruct((T_ALL, d), out_dtype),
        compiler_params=_cparams(("parallel",)),
        name="rms_rows",
    )(x_p, x_s, w.reshape(1, d))


def _final_body(h_ref, ya_ref, yb_ref, w_ref, op_ref, os_ref):
    y = ya_ref[...].astype(F32) + yb_ref[...].astype(F32)
    out = _rms(h_ref[...] + y, w_ref[...])
    is_prompt = pl.program_id(0) < T_PROMPT // ROW_TILE

    @pl.when(is_prompt)
    def _():
        op_ref[...] = out

    @pl.when(jnp.logical_not(is_prompt))
    def _():
        os_ref[...] = out


def _final_norm(h, y_a, y_b, w):
    m, d = h.shape
    n_p = T_PROMPT // ROW_TILE
    row = pl.BlockSpec((ROW_TILE, d), lambda i: (i, 0))
    return pl.pallas_call(
        _final_body,
        grid=(m // ROW_TILE,),
        in_specs=[row, row, row, pl.BlockSpec((1, d), lambda i: (0, 0))],
        out_specs=[pl.BlockSpec((ROW_TILE, d), lambda i: (jnp.minimum(i, n_p - 1), 0)),
                   pl.BlockSpec((ROW_TILE, d), lambda i: (0, 0))],
        out_shape=[jax.ShapeDtypeStruct((T_PROMPT, d), F32), jax.ShapeDtypeStruct((T_SAMPLE, d), F32)],
        compiler_params=_cparams(("arbitrary",)),
        name="final_norm",
    )(h, y_a, y_b, w.reshape(1, d))


def _mm_body(epilogue, a_ref, w_ref, *rest):
    o_ref = rest[-1]
    acc = jnp.dot(a_ref[...], w_ref[...], preferred_element_type=F32)
    if epilogue is not None:
        acc = epilogue(acc, *[r[...] for r in rest[:-1]])
    o_ref[...] = acc.astype(o_ref.dtype)


def _mm(a, w, *, tm, tn, out_dtype, extras=(), epilogue=None, name="mm"):
    m, k = a.shape
    n = w.shape[1]
    in_specs = [pl.BlockSpec((tm, k), lambda i, j: (i, 0)),
                pl.BlockSpec((k, tn), lambda i, j: (0, j))]
    args = [a, w]
    for arr, off in extras:
        in_specs.append(pl.BlockSpec((tm, tn), functools.partial(lambda i, j, off: (i, j + off), off=off)))
        args.append(arr)
    return pl.pallas_call(
        functools.partial(_mm_body, epilogue),
        grid=(m // tm, n // tn),
        in_specs=in_specs,
        out_specs=pl.BlockSpec((tm, tn), lambda i, j: (i, j)),
        out_shape=jax.ShapeDtypeStruct((m, n), out_dtype),
        compiler_params=_cparams(("parallel", "parallel")),
        name=name,
    )(*args)


def _mla_prep_body(s_ref, qw_ref, kvw_ref, cos_ref, sin_ref, cqn_ref, ckv_ref, ckvb_ref, kr_ref, krb_ref):
    cqn_ref[...] = _rms(s_ref[:, SM_CQ:SM_CQ + Q_LORA], qw_ref[...]).astype(BF16)
    c = _rms(s_ref[:, SM_CKV:SM_CKV + KV_LORA], kvw_ref[...])
    ckv_ref[...] = c
    ckvb_ref[...] = c.astype(BF16)
    kr = (s_ref[:, SM_KR:SM_KR + QK_ROPE] * cos_ref[...]
          + s_ref[:, SM_KRS:SM_KRS + QK_ROPE] * sin_ref[...])
    kr_ref[...] = kr
    krb_ref[...] = kr.astype(BF16)


def _mla_prep(small, q_norm_w, kv_norm_w, cos2, sin2):
    m = small.shape[0]
    row = lambda n: pl.BlockSpec((ROW_TILE, n), lambda i: (i, 0))
    vec = lambda n: pl.BlockSpec((1, n), lambda i: (0, 0))
    return pl.pallas_call(
        _mla_prep_body,
        grid=(m // ROW_TILE,),
        in_specs=[row(SM_N), vec(Q_LORA), vec(KV_LORA), row(QK_ROPE), row(QK_ROPE)],
        out_specs=[row(Q_LORA), row(KV_LORA), row(KV_LORA), row(QK_ROPE), row(QK_ROPE)],
        out_shape=[jax.ShapeDtypeStruct((m, Q_LORA), BF16),
                   jax.ShapeDtypeStruct((m, KV_LORA), F32),
                   jax.ShapeDtypeStruct((m, KV_LORA), BF16),
                   jax.ShapeDtypeStruct((m, QK_ROPE), F32),
                   jax.ShapeDtypeStruct((m, QK_ROPE), BF16)],
        compiler_params=_cparams(("parallel",)),
        name="mla_prep",
    )(small, q_norm_w.reshape(1, -1), kv_norm_w.reshape(1, -1), cos2, sin2)


def _q_rope_body(a_ref, w_ref, ws_ref, cos_ref, sin_ref, o_ref):
    reps = o_ref.shape[1] // cos_ref.shape[1]
    a = a_ref[...]
    r = jnp.dot(a, w_ref[...], preferred_element_type=F32)
    rs = jnp.dot(a, ws_ref[...], preferred_element_type=F32)
    cos = jnp.tile(cos_ref[...], (1, reps))
    sin = jnp.tile(sin_ref[...], (1, reps))
    o_ref[...] = ((r * cos + rs * sin) * QSCALE).astype(o_ref.dtype)


def _q_rope(cqn, w_r, w_rs, cos128, sin128, *, tm, tn):
    m, k = cqn.shape
    n = w_r.shape[1]
    return pl.pallas_call(
        _q_rope_body,
        grid=(m // tm, n // tn),
        in_specs=[pl.BlockSpec((tm, k), lambda i, j: (i, 0)),
                  pl.BlockSpec((k, tn), lambda i, j: (0, j)),
                  pl.BlockSpec((k, tn), lambda i, j: (0, j)),
                  pl.BlockSpec((tm, 128), lambda i, j: (i, 0)),
                  pl.BlockSpec((tm, 128), lambda i, j: (i, 0))],
        out_specs=pl.BlockSpec((tm, tn), lambda i, j: (i, j)),
        out_shape=jax.ShapeDtypeStruct((m, n), BF16),
        compiler_params=_cparams(("parallel", "parallel")),
        name="q_rope",
    )(cqn, w_r, w_rs, cos128, sin128)


def _mm_t_body(w_ref, a_ref, o_ref):
    o_ref[0] = _dot_nt(w_ref[...], a_ref[...]).astype(o_ref.dtype)


def _mm_t(w_t, a, *, rows, tm, tn, out_dtype, name):
    n, k = w_t.shape
    return pl.pallas_call(
        _mm_t_body,
        grid=(rows // tm, n // tn),
        in_specs=[pl.BlockSpec((tn, k), lambda i, j: (j, 0)),
                  pl.BlockSpec((tm, k), lambda i, j: (i, 0))],
        out_specs=pl.BlockSpec((1, tn, tm), lambda i, j: (i, j, 0)),
        out_shape=jax.ShapeDtypeStruct((rows // tm, n, tm), out_dtype),
        compiler_params=_cparams(("parallel", "parallel")),
        name=name,
    )(w_t, a)


ATTN_KP = 256


ATTN_VP = 144


def _attn_prompt_body(qn_ref, qr_ref, kn_ref, kr_ref, vt_ref, base_ref, o_ref, kf_sc, vf_sc, qf_sc, acc_sc):
    qi = pl.program_id(2)
    tq = qn_ref.shape[0]
    tk = vt_ref.shape[2]
    seq = kn_ref.shape[0]
    pad = ATTN_KP - QK_NOPE - QK_ROPE

    @pl.when(qi == 0)
    def _():
        def fill(i, c):
            r = pl.ds(pl.multiple_of(i * tk, tk), tk)
            kr = kr_ref[r, :]
            for hh in range(2):
                c0 = hh * ATTN_KP
                kf_sc[r, c0:c0 + QK_NOPE] = kn_ref[r, hh * QK_NOPE:(hh + 1) * QK_NOPE]
                kf_sc[r, c0 + QK_NOPE:c0 + QK_NOPE + QK_ROPE] = kr
                kf_sc[r, c0 + QK_NOPE + QK_ROPE:c0 + ATTN_KP] = jnp.zeros((tk, pad), BF16)
                vf_sc[i, hh * ATTN_VP:hh * ATTN_VP + V_HEAD, :] = vt_ref[i, hh * V_HEAD:(hh + 1) * V_HEAD, :]
                vf_sc[i, hh * ATTN_VP + V_HEAD:(hh + 1) * ATTN_VP, :] = jnp.ones((ATTN_VP - V_HEAD, tk), BF16)
            return c
        lax.fori_loop(0, seq // tk, fill, 0)

    for hh in range(2):
        c0 = hh * ATTN_KP
        qf_sc[:, c0:c0 + QK_NOPE] = qn_ref[:, hh * QK_NOPE:(hh + 1) * QK_NOPE]
        qf_sc[:, c0 + QK_NOPE:c0 + QK_NOPE + QK_ROPE] = qr_ref[:, hh * QK_ROPE:(hh + 1) * QK_ROPE]
        qf_sc[:, c0 + QK_NOPE + QK_ROPE:c0 + ATTN_KP] = jnp.zeros((tq, pad), BF16)
    acc_sc[...] = jnp.zeros(acc_sc.shape, F32)

    def block(kb, carry, diag):
        r = pl.ds(pl.multiple_of(kb * tk, tk), tk)
        q0 = 0 if diag is None else diag * tk
        nq_cols = tq - q0
        out = []
        for hh in range(2):
            m = carry[hh][:, q0:]
            st = _dot_nt(kf_sc[r, hh * ATTN_KP:(hh + 1) * ATTN_KP], qf_sc[q0:, hh * ATTN_KP:(hh + 1) * ATTN_KP])
            if diag is not None:
                kc = lax.broadcasted_iota(jnp.int32, (tk, nq_cols), 0) // CHUNK
                qc = lax.broadcasted_iota(jnp.int32, (tk, nq_cols), 1) // CHUNK
                st = jnp.where(kc <= qc, st, -jnp.inf)
            m_new = jnp.maximum(m, jnp.max(st, axis=0, keepdims=True))
            alpha = jnp.exp2(m - m_new)
            pt = jnp.exp2(st - m_new).astype(BF16)
            vt = vf_sc[kb, hh * ATTN_VP:(hh + 1) * ATTN_VP, :]
            acc_sc[hh, :, q0:] = alpha * acc_sc[hh, :, q0:] + jnp.dot(vt, pt, preferred_element_type=F32)
            out.append(m_new if q0 == 0 else jnp.concatenate([carry[hh][:, :q0], m_new], axis=1))
        return tuple(out)

    n_diag = tq // tk
    n_full = qi * n_diag
    init = (jnp.full((1, tq), -jnp.inf, F32),) * 2
    carry = lax.fori_loop(0, n_full, lambda kb, c: block(kb, c, None), init)
    for d in range(n_diag):
        carry = block(n_full + d, carry, d)
    for hh in range(2):
        o = acc_sc[hh, 0:V_HEAD, :] / acc_sc[hh, V_HEAD:V_HEAD + 1, :]
        o_ref[:, hh * V_HEAD:(hh + 1) * V_HEAD] = o.T.astype(o_ref.dtype)


def _attn_prompt(qn, qr, kn, krb, vt, base, *, nb, seq, heads, tq):
    tk = vt.shape[2]
    nq = seq // tq
    nk = seq // tk
    return pl.pallas_call(
        _attn_prompt_body,
        grid=(nb, heads // 2, nq),
        in_specs=[pl.BlockSpec((tq, 2 * QK_NOPE), lambda b, hp, qi: (b * nq + qi, hp)),
                  pl.BlockSpec((tq, 2 * QK_ROPE), lambda b, hp, qi: (b * nq + qi, hp)),
                  pl.BlockSpec((seq, 2 * QK_NOPE), lambda b, hp, qi: (b, hp)),
                  pl.BlockSpec((seq, QK_ROPE), lambda b, hp, qi: (b, 0)),
                  pl.BlockSpec((nk, 2 * V_HEAD, tk), lambda b, hp, qi: (b, hp, 0)),
                  pl.BlockSpec(memory_space=pl.ANY)],
        out_specs=pl.BlockSpec((tq, 2 * V_HEAD), lambda b, hp, qi: (b * nq + qi, hp)),
        out_shape=jax.ShapeDtypeStruct(base.shape, base.dtype),
        input_output_aliases={5: 0},
        scratch_shapes=[pltpu.VMEM((seq, 2 * ATTN_KP), BF16), pltpu.VMEM((nk, 2 * ATTN_VP, tk), BF16),
                        pltpu.VMEM((tq, 2 * ATTN_KP), BF16), pltpu.VMEM((2, ATTN_VP, tq), F32)],
        compiler_params=_cparams(("parallel", "parallel", "arbitrary")),
        name="attn_prompt",
    )(qn, qr, kn, krb, vt, base)


def _bmm_body(a_ref, w_ref, o_ref):
    o_ref[0] = jnp.dot(a_ref[0], w_ref[0], preferred_element_type=F32).astype(o_ref.dtype)


def _bmm(a, w, out_dtype, name):
    h, m, k = a.shape
    n = w.shape[2]
    return pl.pallas_call(
        _bmm_body,
        grid=(h,),
        in_specs=[pl.BlockSpec((1, m, k), lambda i: (i, 0, 0)),
                  pl.BlockSpec((1, k, n), lambda i: (i, 0, 0))],
        out_specs=pl.BlockSpec((1, m, n), lambda i: (i, 0, 0)),
        out_shape=jax.ShapeDtypeStruct((h, m, n), out_dtype),
        compiler_params=_cparams(("parallel",)),
        name=name,
    )(a, w)


def _attn_sample_body(ql_ref, qr_ref, cc_ref, ck_ref, nc_ref, nk_ref, o_ref, ccb_sc, ckb_sc):
    @pl.when(pl.program_id(1) == 0)
    def _():
        ccb_sc[...] = cc_ref[0].astype(BF16)
        ckb_sc[...] = ck_ref[0].astype(BF16)

    ql = ql_ref[0]
    qr = qr_ref[0]
    ncb = nc_ref[...].astype(BF16)
    nkb = nk_ref[...].astype(BF16)
    s1 = _dot_nt(ql, ccb_sc[...]) + _dot_nt(qr, ckb_sc[...])
    s2 = _dot_nt(ql, ncb) + _dot_nt(qr, nkb)
    m = jnp.maximum(jnp.max(s1, axis=1, keepdims=True), jnp.max(s2, axis=1, keepdims=True))
    p1 = jnp.exp2(s1 - m)
    p2 = jnp.exp2(s2 - m)
    l = jnp.sum(p1, axis=1, keepdims=True) + jnp.sum(p2, axis=1, keepdims=True)
    o = (jnp.dot(p1.astype(BF16), ccb_sc[...], preferred_element_type=F32)
         + jnp.dot(p2.astype(BF16), ncb, preferred_element_type=F32))
    o_ref[0] = (o / l).astype(o_ref.dtype)


def _attn_sample(q_lat, q_rope, cache_ckv, cache_krope, ckv, krope):
    rows = q_lat.shape[1]
    tr = 256
    new_blk = T_PROMPT // DEC_SEQ
    return pl.pallas_call(
        _attn_sample_body,
        grid=(DEC_BATCH, rows // tr),
        in_specs=[pl.BlockSpec((1, tr, KV_LORA), lambda b, r: (b, r, 0)),
                  pl.BlockSpec((1, tr, QK_ROPE), lambda b, r: (b, r, 0)),
                  pl.BlockSpec((1, PAST_LEN, KV_LORA), lambda b, r: (b, 0, 0)),
                  pl.BlockSpec((1, PAST_LEN, QK_ROPE), lambda b, r: (b, 0, 0)),
                  pl.BlockSpec((DEC_SEQ, KV_LORA), lambda b, r: (new_blk + b, 0)),
                  pl.BlockSpec((DEC_SEQ, QK_ROPE), lambda b, r: (new_blk + b, 0))],
        out_specs=pl.BlockSpec((1, tr, KV_LORA), lambda b, r: (b, r, 0)),
        out_shape=jax.ShapeDtypeStruct((DEC_BATCH, rows, KV_LORA), BF16),
        scratch_shapes=[pltpu.VMEM((PAST_LEN, KV_LORA), BF16), pltpu.VMEM((PAST_LEN, QK_ROPE), BF16)],
        compiler_params=_cparams(("parallel", "arbitrary")),
        name="attn_sample",
    )(q_lat, q_rope, cache_ckv, cache_krope, ckv, krope)


def _conv_body(x_ref, prev_ref, w_ref, b_ref, o_ref, ext_sc):
    tl = x_ref.shape[0]

    @pl.when(pl.program_id(2) == 0)
    def _():
        ext_sc[0:8, :] = prev_ref[0]

    @pl.when(pl.program_id(2) != 0)
    def _():
        ext_sc[0:8, :] = ext_sc[tl:tl + 8, :]

    ext_sc[8:8 + tl, :] = x_ref[...]
    acc = b_ref[...] + ext_sc[5:5 + tl, :] * w_ref[0:1, :]
    for k in range(1, SSD_CONV):
        acc = acc + ext_sc[5 + k:5 + k + tl, :] * w_ref[k:k + 1, :]
    o_ref[...] = (acc * _sigmoid(acc)).astype(o_ref.dtype)


def _conv_silu(big, prev8, conv_w, conv_b, *, nb, seq, tl, row_off):
    tc = 1024
    nrt = seq // tl
    rb0 = row_off // tl
    cb0 = BIG_XBC // tc
    return pl.pallas_call(
        _conv_body,
        grid=(nb, SSD_CONV_DIM // tc, nrt),
        in_specs=[pl.BlockSpec((tl, tc), lambda b, j, r: (rb0 + b * nrt + r, cb0 + j)),
                  pl.BlockSpec((1, 8, tc), lambda b, j, r: (b, 0, j)),
                  pl.BlockSpec((SSD_CONV, tc), lambda b, j, r: (0, j)),
                  pl.BlockSpec((1, tc), lambda b, j, r: (0, j))],
        out_specs=pl.BlockSpec((tl, tc), lambda b, j, r: (b * nrt + r, j)),
        out_shape=jax.ShapeDtypeStruct((nb * seq, SSD_CONV_DIM), BF16),
        scratch_shapes=[pltpu.VMEM((tl + 8, tc), F32)],
        compiler_params=_cparams(("parallel", "parallel", "arbitrary")),
        name="conv_silu",
    )(big, prev8, conv_w, conv_b.reshape(1, -1))


def _softplus(x):
    return jnp.maximum(x, 0.0) + jnp.log(1.0 + jnp.exp(-jnp.abs(x)))


def _dot_sel(sel, a, *, sel_left):
    hi = a.astype(BF16)
    r1 = a - hi.astype(F32)
    mid = r1.astype(BF16)
    lo = (r1 - mid.astype(F32)).astype(BF16)
    out = None
    for term in (hi, mid, lo):
        d = (jnp.dot(sel, term, preferred_element_type=F32) if sel_left
             else jnp.dot(term, sel, preferred_element_type=F32))
        out = d if out is None else out + d
    return out


def _ssd_body(has_h0, x_ref, b_ref, c_ref, z_ref, dt_ref, bias_ref, al_ref, dsk_ref, nw_ref, *rest):
    h0_ref = rest[0] if has_h0 else None
    y_ref, hout_ref, ht_sc = rest[-3:]
    R, P = SSD_RANK, SSD_HEADDIM
    lc = P
    W = 2 * P
    c_idx = pl.program_id(2)

    @pl.when(c_idx == 0)
    def _():
        if has_h0:
            ht_sc[...] = h0_ref[0].T
        else:
            ht_sc[...] = jnp.zeros(ht_sc.shape, F32)

    ri = lax.broadcasted_iota(jnp.int32, (lc, lc), 0)
    ci = lax.broadcasted_iota(jnp.int32, (lc, lc), 1)
    tril_b = jnp.where(ri >= ci, 1.0, 0.0).astype(BF16)
    er = lax.broadcasted_iota(jnp.int32, (R, R * P), 0)
    ec = lax.broadcasted_iota(jnp.int32, (R, R * P), 1) // P
    expand = jnp.where(er == ec, 1.0, 0.0).astype(BF16)
    li = lax.broadcasted_iota(jnp.int32, (lc, R * P), 0)
    si = lax.broadcasted_iota(jnp.int32, (lc, R * P), 1) % P
    first_head = lax.broadcasted_iota(jnp.int32, (lc, W), 1) < P
    neg_a = -jnp.exp(al_ref[0])

    ht = ht_sc[...]
    for k in range(x_ref.shape[0] // lc):
        rows = slice(k * lc, (k + 1) * lc)
        dt = _softplus(dt_ref[0, rows, :] + bias_ref[0])
        a_cs = _dot_sel(tril_b, dt * neg_a, sel_left=True)
        a_exp = _dot_sel(expand, a_cs, sel_left=False)
        dt_exp = _dot_sel(expand, dt, sel_left=False)
        a_last = a_exp[lc - 1:lc, :]
        a_key = jnp.sum(jnp.where(li == si, a_exp, 0.0), axis=0, keepdims=True)
        decay = jnp.exp(jnp.where(li >= si, a_exp - a_key, -jnp.inf))

        x = x_ref[rows, :].astype(F32)
        xdt = x * dt_exp
        bm = b_ref[rows, :].astype(BF16)
        cm = c_ref[rows, :].astype(BF16)
        cb2 = _dot_nt(cm, jnp.concatenate([bm, bm], axis=0))
        m_all = (jnp.tile(cb2, (1, R // 2)) * decay).astype(BF16)
        y_off = jnp.dot(cm, ht.astype(BF16), preferred_element_type=F32) * jnp.exp(a_exp)
        xdt_b = xdt.astype(BF16)
        y_diag = []
        for j in range(R // 2):
            xp = xdt_b[:, j * W:(j + 1) * W]
            zero = jnp.zeros_like(xp)
            stack = jnp.concatenate([jnp.where(first_head, xp, zero), jnp.where(first_head, zero, xp)], axis=0)
            y_diag.append(jnp.dot(m_all[:, j * W:(j + 1) * W], stack, preferred_element_type=F32))
        wgt = (xdt * jnp.exp(a_last - a_exp)).astype(BF16)
        states_t = lax.dot_general(bm, wgt, (((0,), (0,)), ((), ())), preferred_element_type=F32)
        ht = ht * jnp.exp(a_last) + states_t

        y = jnp.concatenate(y_diag, axis=1) + y_off + x * dsk_ref[...]
        z = z_ref[rows, :]
        y = y * (z * _sigmoid(z))
        y = y * lax.rsqrt(jnp.mean(y * y, axis=-1, keepdims=True) + EPS) * nw_ref[...]
        y_ref[rows, :] = y.astype(y_ref.dtype)
    ht_sc[...] = ht

    @pl.when(c_idx == pl.num_programs(2) - 1)
    def _():
        hout_ref[0] = ht.T


def _ssd(xbc, zsrc, dt_g, h0, dt_bias, a_log, d_skip, ssd_norm_w, *, nb, seq, sub, base=None):
    lc = sub * SSD_HEADDIM
    nc = seq // lc
    G, R, GC, N = SSD_GROUPS, SSD_RANK, SSD_GCOLS, SSD_STATE
    row = lambda b, g, c: b * nc + c
    in_specs = [
        pl.BlockSpec((lc, GC), lambda b, g, c: (row(b, g, c), g)),
        pl.BlockSpec((lc, N), lambda b, g, c: (row(b, g, c), SSD_INNER // N + g)),
        pl.BlockSpec((lc, N), lambda b, g, c: (row(b, g, c), SSD_INNER // N + G + g)),
        pl.BlockSpec((lc, GC), lambda b, g, c: (row(b, g, c), g)),
        pl.BlockSpec((1, lc, R), lambda b, g, c: (g, row(b, g, c), 0)),
        pl.BlockSpec((1, 1, R), lambda b, g, c: (g, 0, 0)),
        pl.BlockSpec((1, 1, R), lambda b, g, c: (g, 0, 0)),
        pl.BlockSpec((1, GC), lambda b, g, c: (0, g)),
        pl.BlockSpec((1, GC), lambda b, g, c: (0, g)),
    ]
    args = [xbc, xbc, xbc, zsrc, dt_g, dt_bias.reshape(G, 1, R), a_log.reshape(G, 1, R),
            jnp.repeat(d_skip, SSD_HEADDIM).reshape(1, SSD_INNER), ssd_norm_w.reshape(1, SSD_INNER)]
    if h0 is not None:
        in_specs.append(pl.BlockSpec((1, GC, N), lambda b, g, c: (b, g, 0)))
        args.append(h0)
    aliases = {}
    y_shape = jax.ShapeDtypeStruct((nb * seq, SSD_INNER), BF16)
    if base is not None:
        aliases = {len(args): 0}
        in_specs.append(pl.BlockSpec(memory_space=pl.ANY))
        args.append(base)
        y_shape = jax.ShapeDtypeStruct(base.shape, base.dtype)
    return pl.pallas_call(
        functools.partial(_ssd_body, h0 is not None),
        grid=(nb, G, nc),
        in_specs=in_specs,
        out_specs=[pl.BlockSpec((lc, GC), lambda b, g, c: (row(b, g, c), g)),
                   pl.BlockSpec((1, GC, N), lambda b, g, c: (b, g, 0))],
        out_shape=[y_shape, jax.ShapeDtypeStruct((nb, SSD_HEADS * SSD_HEADDIM, N), F32)],
        input_output_aliases=aliases,
        scratch_shapes=[pltpu.VMEM((N, GC), F32)],
        compiler_params=_cparams(("parallel", "parallel", "arbitrary")),
        name="ssd",
    )(*args)


def _norm_router_body(xp_ref, xs_ref, mix_ref, w_ref, wr_ref, br_ref, h_ref, xn_ref, lg_ref):
    h = _stacked_x(xp_ref, xs_ref) + mix_ref[...]
    h_ref[...] = h
    xn = _rms(h, w_ref[...])
    xn_ref[...] = xn.astype(BF16)
    lg_ref[...] = jnp.dot(xn, wr_ref[...], precision=HI, preferred_element_type=F32) + br_ref[...]


def _norm_router(x_p, x_s, mix, norm_w, w_router, b_router):
    m, d = mix.shape
    n = w_router.shape[1]
    row = pl.BlockSpec((ROW_TILE, d), lambda i: (i, 0))
    return pl.pallas_call(
        _norm_router_body,
        grid=(m // ROW_TILE,),
        in_specs=_stacked_x_specs() + [row,
                                       pl.BlockSpec((1, d), lambda i: (0, 0)),
                                       pl.BlockSpec((d, n), lambda i: (0, 0)),
                                       pl.BlockSpec((1, n), lambda i: (0, 0))],
        out_specs=[row, row, pl.BlockSpec((ROW_TILE, n), lambda i: (i, 0))],
        out_shape=[jax.ShapeDtypeStruct((m, d), F32), jax.ShapeDtypeStruct((m, d), BF16),
                   jax.ShapeDtypeStruct((m, n), F32)],
        compiler_params=_cparams(("parallel",)),
        name="norm_router",
    )(x_p, x_s, mix, norm_w.reshape(1, d), w_router, b_router)


def _moe_up_body(be_ref, nused_ref, win_ref, x_ref, wg_ref, wu_ref, o_ref):
    blk = pl.program_id(1)

    @pl.when(blk < nused_ref[0])
    def _():
        x = x_ref[...]
        g = jnp.dot(x, wg_ref[0].astype(BF16), preferred_element_type=F32)
        u = jnp.dot(x, wu_ref[0].astype(BF16), preferred_element_type=F32)
        o_ref[...] = (g * _sigmoid(g) * u).astype(o_ref.dtype)

    @pl.when(blk >= nused_ref[0])
    def _():
        o_ref[...] = jnp.zeros(o_ref.shape, o_ref.dtype)


def _moe_up(be, nused, win, xs, w_gate, w_up):
    d = xs.shape[1]
    nblk = be.shape[0]
    grid_spec = pltpu.PrefetchScalarGridSpec(
        num_scalar_prefetch=3,
        grid=(D_EXPERT // MOE_TF, nblk),
        in_specs=[pl.BlockSpec((pl.Element(MOE_TM), pl.Element(d)),
                               lambda f, i, be, nu, win: (win[i] * MOE_GATHER_ALIGN, 0)),
                  pl.BlockSpec((1, d, MOE_TF), lambda f, i, be, nu, win: (be[i], 0, f)),
                  pl.BlockSpec((1, d, MOE_TF), lambda f, i, be, nu, win: (be[i], 0, f))],
        out_specs=pl.BlockSpec((MOE_TM, MOE_TF), lambda f, i, be, nu, win: (i, f)),
    )
    return pl.pallas_call(
        _moe_up_body,
        grid_spec=grid_spec,
        out_shape=jax.ShapeDtypeStruct((nblk * MOE_TM, D_EXPERT), BF16),
        compiler_params=_cparams(("arbitrary", "arbitrary")),
        name="moe_up",
    )(be, nused, win, xs, w_gate, w_up)


def _moe_down_body(be_ref, nused_ref, h_ref, wd_ref, rw_ref, o_ref):
    blk = pl.program_id(1)

    @pl.when(blk < nused_ref[0])
    def _():
        y = jnp.dot(h_ref[...], wd_ref[0].astype(BF16), preferred_element_type=F32)
        o_ref[...] = (y * rw_ref[...]).astype(o_ref.dtype)

    @pl.when(blk >= nused_ref[0])
    def _():
        o_ref[...] = jnp.zeros(o_ref.shape, o_ref.dtype)


def _moe_down(be, nused, hact, w_down, row_w):
    p, f = hact.shape
    d = w_down.shape[2]
    nblk = p // MOE_TM
    grid_spec = pltpu.PrefetchScalarGridSpec(
        num_scalar_prefetch=2,
        grid=(d // MOE_TN, nblk),
        in_specs=[pl.BlockSpec((MOE_TM, f), lambda n, i, be, nu: (i, 0)),
                  pl.BlockSpec((1, f, MOE_TN), lambda n, i, be, nu: (be[i], 0, n)),
                  pl.BlockSpec((MOE_TM, 1), lambda n, i, be, nu: (i, 0))],
        out_specs=pl.BlockSpec((MOE_TM, MOE_TN), lambda n, i, be, nu: (i, n)),
    )
    return pl.pallas_call(
        _moe_down_body,
        grid_spec=grid_spec,
        out_shape=jax.ShapeDtypeStruct((p, d), BF16),
        compiler_params=_cparams(("arbitrary", "arbitrary")),
        name="moe_down",
    )(be, nused, hact, w_down, row_w)


def _route(logits):
    t = logits.shape[0]
    g_logits = logits[:, :N_GROUPS]
    g_sel = jnp.argmax(g_logits, axis=-1)
    g_w = jnp.max(jax.nn.softmax(g_logits, axis=-1), axis=-1)
    e_logits = logits[:, N_GROUPS:N_GROUPS + N_EXPERTS].reshape(t, N_GROUPS, EXPERTS_PER_GROUP)
    e_in = jnp.take_along_axis(e_logits, g_sel[:, None, None], axis=1)[:, 0]
    e_val, e_idx = lax.top_k(e_in, TOP_K)
    e_w = jax.nn.softmax(e_val, axis=-1) * g_w[:, None]
    expert_id = (g_sel[:, None] * EXPERTS_PER_GROUP + e_idx).reshape(-1).astype(jnp.int32)
    a = t * TOP_K
    e_sorted, order = lax.sort((expert_id, jnp.arange(a, dtype=jnp.int32)), num_keys=1, is_stable=True)
    experts = jnp.arange(N_EXPERTS, dtype=jnp.int32)
    start = jnp.searchsorted(e_sorted, experts, side='left').astype(jnp.int32)
    counts = jnp.searchsorted(e_sorted, experts, side='right').astype(jnp.int32) - start
    padded = (counts + MOE_TM - 1) // MOE_TM * MOE_TM
    pad_end = jnp.cumsum(padded)
    pad_start = pad_end - padded
    shift = pad_start - start
    jumps = shift - jnp.concatenate([jnp.zeros((1,), jnp.int32), shift[:-1]])
    marks = jnp.zeros((a + 1,), jnp.int32).at[start].add(jumps)
    dest = jnp.arange(a, dtype=jnp.int32) + jnp.cumsum(marks)[:a]
    pos = lax.sort((order, dest), num_keys=1)[1].reshape(t, TOP_K)
    nblk = a // MOE_TM + N_EXPERTS
    nused = (pad_end[-1] // MOE_TM).astype(jnp.int32)
    blk = jnp.arange(nblk, dtype=jnp.int32)
    owner = lambda ends, offs: jnp.minimum(jnp.sum((ends[None, :] <= offs[:, None]).astype(jnp.int32), axis=1),
                                           N_EXPERTS - 1)
    be = owner(pad_end, blk * MOE_TM)
    be = jnp.where(blk < nused, be, be[jnp.maximum(nused - 1, 0)])
    per_row = lambda v: jnp.broadcast_to(v[be][:, None], (nblk, MOE_TM)).reshape(-1)
    rank = jnp.arange(nblk * MOE_TM, dtype=jnp.int32) - per_row(pad_start)
    valid = rank < per_row(counts)
    src = jnp.clip(per_row(start) + rank, 0, a - 1)
    row_w = jnp.where(valid, e_w.reshape(-1)[order[src]], 0.0)
    g = MOE_GATHER_ALIGN
    tight = (counts + g - 1) // g * g
    tight_end = jnp.cumsum(tight)
    tight_start = tight_end - tight
    n_mini = MOE_GATHER_ROWS // g
    mini_e = owner(tight_end, jnp.arange(n_mini, dtype=jnp.int32) * g)
    per_mini = lambda v: jnp.broadcast_to(v[mini_e][:, None], (n_mini, g)).reshape(-1)
    grank = jnp.arange(MOE_GATHER_ROWS, dtype=jnp.int32) - per_mini(tight_start)
    gvalid = (grank >= 0) & (grank < per_mini(counts))
    gather_token = jnp.where(gvalid, order[jnp.clip(per_mini(start) + grank, 0, a - 1)] // TOP_K, 0)
    win = (tight_start[be] + blk * MOE_TM - pad_start[be]) // g
    win = jnp.where(blk < nused, win, 0).astype(jnp.int32)
    return gather_token, win, row_w, pos, be, nused.reshape(1)


def _rope_tables():
    half = QK_ROPE // 2
    inv_freq = ROPE_THETA ** (-jnp.arange(half, dtype=F32) / half)
    pos = jnp.concatenate([jnp.tile(jnp.arange(SEQ), BATCH),
                           jnp.tile(PAST_LEN + jnp.arange(DEC_SEQ), DEC_BATCH)]).astype(F32)
    ang = pos[:, None] * inv_freq[None, :]
    cos, sin = jnp.cos(ang), jnp.sin(ang)
    cos2 = jnp.concatenate([cos, cos], axis=1)
    sin2 = jnp.concatenate([-sin, sin], axis=1)
    return cos2, sin2


def kernel(x_prompt, x_sample, cache_ckv, cache_krope, state_conv, state_ssm, norm1_w, w_in, q_norm_w, kv_norm_w, w_uq, w_ukv, conv_w, conv_b, dt_bias, a_log, d_skip, ssd_norm_w, w_mla_o, w_ssd_o, w_out, norm2_w, w_group, b_group, w_erouter, b_erouter, w_gate, w_up, w_down, final_norm_w):
    swap = np.concatenate([np.arange(QK_ROPE // 2, QK_ROPE), np.arange(QK_ROPE // 2)])
    x_p2 = x_prompt.reshape(T_PROMPT, D_MODEL)
    x_s2 = x_sample.reshape(T_SAMPLE, D_MODEL)
    cos2, sin2 = _rope_tables()
    cos128, sin128 = jnp.tile(cos2, (1, 2)), jnp.tile(sin2, (1, 2))

    wi = w_in[0]
    w_kr = wi[:, OFF_KR:OFF_KR + QK_ROPE]
    w_small = jnp.concatenate([wi[:, OFF_CQ:OFF_KR], wi[:, OFF_DT:OFF_DT + SSD_HEADS], w_kr, w_kr[:, swap]],
                              axis=1).astype(BF16)
    w_big = wi[:, OFF_Z:OFF_DT].astype(BF16)
    w_gates = wi[:, OFF_GMLA:].astype(BF16)
    wq = w_uq[0].reshape(Q_LORA, MLA_HEADS, QK_NOPE + QK_ROPE)
    wq_nope = wq[:, :, :QK_NOPE].reshape(Q_LORA, MLA_HEADS * QK_NOPE).astype(BF16)
    wq_rope = wq[:, :, QK_NOPE:]
    wq_r = wq_rope.reshape(Q_LORA, MLA_HEADS * QK_ROPE).astype(BF16)
    wq_rs = wq_rope[:, :, swap].reshape(Q_LORA, MLA_HEADS * QK_ROPE).astype(BF16)
    wkv3 = w_ukv[0].reshape(KV_LORA, MLA_HEADS, QK_NOPE + V_HEAD)
    w_uk_all = wkv3[:, :, :QK_NOPE].reshape(KV_LORA, MLA_HEADS * QK_NOPE).astype(BF16)
    w_uv_all_t = jnp.transpose(wkv3[:, :, QK_NOPE:], (1, 2, 0)).reshape(MLA_HEADS * V_HEAD, KV_LORA).astype(BF16)
    w_uk_t = jnp.transpose(wkv3[:, :, :QK_NOPE], (1, 2, 0)).astype(BF16)
    w_uv_h = jnp.transpose(wkv3[:, :, QK_NOPE:], (1, 0, 2)).astype(BF16)

    u = _rms_rows(x_p2, x_s2, norm1_w[0], BF16)
    small = _mm(u, w_small, tm=MM_TM, tn=SM_N // 2, out_dtype=F32, name="proj_small")
    big = _mm(u, w_big, tm=MM_TM, tn=512, out_dtype=F32, name="proj_big")
    gates = _mm(u, w_gates, tm=MM_TM, tn=512, out_dtype=F32, name="proj_gates")

    cqn, ckv, ckv_b, krope, krope_b = _mla_prep(small, q_norm_w[0], kv_norm_w[0], cos2, sin2)
    qn = _mm(cqn, wq_nope, tm=MM_TM, tn=1024, out_dtype=BF16, epilogue=lambda acc: acc * QSCALE, name="q_nope")
    qr = _q_rope(cqn, wq_r, wq_rs, cos128, sin128, tm=MM_TM, tn=1024)
    kn = _mm(ckv_b, w_uk_all, tm=MM_TM, tn=1024, out_dtype=BF16, name="k_up")
    vt = _mm_t(w_uv_all_t, ckv_b, rows=T_PROMPT, tm=ATTN_TK, tn=1024, out_dtype=BF16, name="v_up_t")

    qn_s = qn[T_PROMPT:].reshape(T_SAMPLE, MLA_HEADS, QK_NOPE).transpose(1, 0, 2)
    q_lat = _bmm(qn_s, w_uk_t, BF16, "q_absorb")
    q_lat = q_lat.reshape(MLA_HEADS, DEC_BATCH, DEC_SEQ, KV_LORA).transpose(1, 0, 2, 3)
    q_lat = q_lat.reshape(DEC_BATCH, MLA_HEADS * DEC_SEQ, KV_LORA)
    qr_s = qr[T_PROMPT:].reshape(DEC_BATCH, DEC_SEQ, MLA_HEADS, QK_ROPE).transpose(0, 2, 1, 3)
    qr_s = qr_s.reshape(DEC_BATCH, MLA_HEADS * DEC_SEQ, QK_ROPE)
    o_lat = _attn_sample(q_lat, qr_s, cache_ckv[0], cache_krope[0], ckv, krope)
    o_lat = o_lat.reshape(DEC_BATCH, MLA_HEADS, DEC_SEQ, KV_LORA).transpose(1, 0, 2, 3)
    o_lat = o_lat.reshape(MLA_HEADS, T_SAMPLE, KV_LORA)
    o_mla_s = _bmm(o_lat, w_uv_h, BF16, "v_absorb")
    o_mla_s = o_mla_s.transpose(1, 0, 2).reshape(T_SAMPLE, MLA_HEADS * V_HEAD)
    o_mla = _attn_prompt(qn, qr, kn, krope_b, vt, jnp.pad(o_mla_s, ((T_PROMPT, 0), (0, 0))),
                         nb=BATCH, seq=SEQ, heads=MLA_HEADS, tq=ATTN_TQ)

    prev_p = jnp.zeros((BATCH, 8, SSD_CONV_DIM), F32)
    prev_s = jnp.concatenate([jnp.zeros((DEC_BATCH, 8 - (SSD_CONV - 1), SSD_CONV_DIM), F32), state_conv[0]], axis=1)
    xbc_p = _conv_silu(big, prev_p, conv_w[0], conv_b[0], nb=BATCH, seq=SEQ, tl=512, row_off=0)
    xbc_s = _conv_silu(big, prev_s, conv_w[0], conv_b[0], nb=DEC_BATCH, seq=DEC_SEQ, tl=DEC_SEQ, row_off=T_PROMPT)
    dt_raw = small[:, SM_DT:SM_DT + SSD_HEADS]
    by_group = lambda d: d.reshape(d.shape[0], SSD_GROUPS, SSD_RANK).transpose(1, 0, 2)

    def pad_seq(a, fill):
        a = a.reshape(DEC_BATCH, DEC_SEQ, a.shape[-1])
        a = jnp.pad(a, ((0, 0), (0, CHUNK - DEC_SEQ), (0, 0)), constant_values=fill)
        return a.reshape(DEC_BATCH * CHUNK, a.shape[-1])

    h0_s = state_ssm[0].reshape(DEC_BATCH, SSD_HEADS * SSD_HEADDIM, SSD_STATE)
    y_s, ssm_s = _ssd(pad_seq(xbc_s, 0.0), pad_seq(big[T_PROMPT:, BIG_Z:BIG_Z + SSD_INNER], 0.0),
                      by_group(pad_seq(dt_raw[T_PROMPT:], -jnp.inf)), h0_s, dt_bias[0], a_log[0], d_skip[0],
                      ssd_norm_w[0], nb=DEC_BATCH, seq=CHUNK, sub=1)
    y_s = y_s.reshape(DEC_BATCH, CHUNK, SSD_INNER)[:, :DEC_SEQ].reshape(T_SAMPLE, SSD_INNER)
    o_ssd, ssm_p = _ssd(xbc_p, big, by_group(dt_raw[:T_PROMPT]), None, dt_bias[0], a_log[0], d_skip[0],
                        ssd_norm_w[0], nb=BATCH, seq=SEQ, sub=SSD_SUB, base=jnp.pad(y_s, ((T_PROMPT, 0), (0, 0))))

    gate = lambda acc, g: _sigmoid(g) * acc
    m1 = _mm(o_mla, w_mla_o[0].astype(BF16), tm=640, tn=1024, out_dtype=F32,
             extras=[(gates, GATE_MLA // 1024)], epilogue=gate, name="mla_out")
    merged = _mm(o_ssd, w_ssd_o[0].astype(BF16), tm=640, tn=512, out_dtype=BF16,
                 extras=[(gates, GATE_SSD // 512), (m1, 0)],
                 epilogue=lambda acc, g, m: m + _sigmoid(g) * acc, name="ssd_out")
    mix = _mm(merged, w_out[0].astype(BF16), tm=640, tn=1024, out_dtype=F32, name="out_proj")

    n_r = 128
    w_router = jnp.concatenate([w_group[0], w_erouter[0],
                                jnp.zeros((D_MODEL, n_r - N_GROUPS - N_EXPERTS), F32)], axis=1)
    b_router = jnp.concatenate([b_group[0], b_erouter[0], jnp.zeros((n_r - N_GROUPS - N_EXPERTS,), F32)])
    h, xn, logits = _norm_router(x_p2, x_s2, mix, norm2_w[0], w_router, b_router.reshape(1, n_r))
    gather_token, win, row_w, pos, be, nused = _route(logits)
    xs = jnp.take(xn, gather_token, axis=0, mode="clip")
    hact = _moe_up(be, nused, win, xs, w_gate[0], w_up[0])
    yb = _moe_down(be, nused, hact, w_down[0], row_w.reshape(-1, 1))
    y_p2, y_s2 = _final_norm(h, jnp.take(yb, pos[:, 0], axis=0, mode="clip"),
                             jnp.take(yb, pos[:, 1], axis=0, mode="clip"), final_norm_w)

    y_prompt = y_p2.reshape(BATCH, SEQ, D_MODEL)
    y_sample = y_s2.reshape(DEC_BATCH, DEC_SEQ, D_MODEL)
    ckv_p = ckv[:T_PROMPT].reshape(1, BATCH, SEQ, KV_LORA)
    ckv_s = ckv[T_PROMPT:].reshape(1, DEC_BATCH, DEC_SEQ, KV_LORA)
    kr_p = krope[:T_PROMPT].reshape(1, BATCH, SEQ, QK_ROPE)
    kr_s = krope[T_PROMPT:].reshape(1, DEC_BATCH, DEC_SEQ, QK_ROPE)
    tail = SSD_CONV - 1
    conv_p = jnp.stack([lax.slice(big, ((b + 1) * SEQ - tail, BIG_XBC), ((b + 1) * SEQ, BIG_XBC + SSD_CONV_DIM))
                        for b in range(BATCH)])[None]
    conv_s = jnp.stack([lax.slice(big, (T_PROMPT + (b + 1) * DEC_SEQ - tail, BIG_XBC),
                                  (T_PROMPT + (b + 1) * DEC_SEQ, BIG_XBC + SSD_CONV_DIM))
                        for b in range(DEC_BATCH)])[None]
    ssm_p = ssm_p.reshape(1, BATCH, SSD_HEADS, SSD_HEADDIM, SSD_STATE)
    ssm_s = ssm_s.reshape(1, DEC_BATCH, SSD_HEADS, SSD_HEADDIM, SSD_STATE)
    return (y_prompt, y_sample, ckv_p, kr_p, conv_p, ssm_p, ckv_s, kr_s, conv_s, ssm_s)
```

```python
import functools
import math

import numpy as np
import jax
import jax.numpy as jnp
from jax import lax
from jax.experimental import pallas as pl
from jax.experimental.pallas import tpu as pltpu

F32 = jnp.float32
BF16 = jnp.bfloat16
HI = lax.Precision.HIGHEST

D_MODEL = 4096
BATCH = 2
SEQ = 8192
DEC_BATCH = 8
DEC_SEQ = 32
PAST_LEN = 4096
CHUNK = 64
EPS = 1e-6
MLA_HEADS = 32
Q_LORA = 1024
KV_LORA = 512
QK_NOPE = 128
QK_ROPE = 64
V_HEAD = 128
ROPE_THETA = 10000.0
SCALE = (QK_NOPE + QK_ROPE) ** -0.5
QSCALE = SCALE * math.log2(math.e)
SSD_INNER = 2 * D_MODEL
SSD_HEADDIM = 64
SSD_HEADS = SSD_INNER // SSD_HEADDIM
SSD_STATE = 128
SSD_GROUPS = 8
SSD_RANK = SSD_HEADS // SSD_GROUPS
SSD_GCOLS = SSD_RANK * SSD_HEADDIM
SSD_CONV = 4
SSD_CONV_DIM = SSD_INNER + 2 * SSD_GROUPS * SSD_STATE
N_GROUPS = 8
EXPERTS_PER_GROUP = 8
N_EXPERTS = N_GROUPS * EXPERTS_PER_GROUP
TOP_K = 2
D_EXPERT = 1024

T_PROMPT = BATCH * SEQ
T_SAMPLE = DEC_BATCH * DEC_SEQ
T_ALL = T_PROMPT + T_SAMPLE

OFF_CQ = 0
OFF_CKV = OFF_CQ + Q_LORA
OFF_KR = OFF_CKV + KV_LORA
OFF_Z = OFF_KR + QK_ROPE
OFF_XBC = OFF_Z + SSD_INNER
OFF_DT = OFF_XBC + SSD_CONV_DIM
OFF_GMLA = OFF_DT + SSD_HEADS
OFF_GSSD = OFF_GMLA + D_MODEL
N_IN = OFF_GSSD + D_MODEL

BIG_Z = 0
BIG_XBC = SSD_INNER
BIG_N = BIG_XBC + SSD_CONV_DIM
GATE_MLA = 0
GATE_SSD = D_MODEL
SM_CQ = 0
SM_CKV = Q_LORA
SM_DT = SM_CKV + KV_LORA
SM_KR = SM_DT + SSD_HEADS
SM_KRS = SM_KR + QK_ROPE
SM_N = SM_KRS + QK_ROPE

V7X_VMEM_LIMIT = 56 * 1024 * 1024
ROW_TILE = 256
MM_TM = 1280
MOE_TM = 512
SSD_SUB = 8
MOE_GATHER_ALIGN = 16
MOE_GATHER_ROWS = T_ALL * TOP_K + N_EXPERTS * MOE_GATHER_ALIGN + MOE_TM
MOE_TF = 512
MOE_TN = 2048
ATTN_TQ = 2048
ATTN_TK = 512


def _cparams(sem):
    return pltpu.CompilerParams(dimension_semantics=sem, vmem_limit_bytes=V7X_VMEM_LIMIT)


def _dot_nt(a, b):
    return lax.dot_general(a, b, (((1,), (1,)), ((), ())), preferred_element_type=F32)


def _rms(x, w):
    return x * lax.rsqrt(jnp.mean(x * x, axis=-1, keepdims=True) + EPS) * w


def _sigmoid(x):
    return 0.5 * jnp.tanh(0.5 * x) + 0.5


def _stacked_x_specs():
    n_p = T_PROMPT // ROW_TILE
    return [pl.BlockSpec((ROW_TILE, D_MODEL), lambda i: (jnp.minimum(i, n_p - 1), 0)),
            pl.BlockSpec((ROW_TILE, D_MODEL), lambda i: (0, 0))]


def _stacked_x(xp_ref, xs_ref):
    is_prompt = pl.program_id(0) < T_PROMPT // ROW_TILE
    return jnp.where(is_prompt, xp_ref[...], xs_ref[...])


def _rms_rows_body(xp_ref, xs_ref, w_ref, o_ref):
    o_ref[...] = _rms(_stacked_x(xp_ref, xs_ref), w_ref[...]).astype(o_ref.dtype)


def _rms_rows(x_p, x_s, w, out_dtype):
    d = D_MODEL
    return pl.pallas_call(
        _rms_rows_body,
        grid=(T_ALL // ROW_TILE,),
        in_specs=_stacked_x_specs() + [pl.BlockSpec((1, d), lambda i: (0, 0))],
        out_specs=pl.BlockSpec((ROW_TILE, d), lambda i: (i, 0)),
        out_shape=jax.ShapeDtypeStruct((T_ALL, d), out_dtype),
        compiler_params=_cparams(("parallel",)),
        name="rms_rows",
    )(x_p, x_s, w.reshape(1, d))


def _final_body(h_ref, ya_ref, yb_ref, w_ref, op_ref, os_ref):
    y = ya_ref[...].astype(F32) + yb_ref[...].astype(F32)
    out = _rms(h_ref[...] + y, w_ref[...])
    is_prompt = pl.program_id(0) < T_PROMPT // ROW_TILE

    @pl.when(is_prompt)
    def _():
        op_ref[...] = out

    @pl.when(jnp.logical_not(is_prompt))
    def _():
        os_ref[...] = out


def _final_norm(h, y_a, y_b, w):
    m, d = h.shape
    n_p = T_PROMPT // ROW_TILE
    row = pl.BlockSpec((ROW_TILE, d), lambda i: (i, 0))
    return pl.pallas_call(
        _final_body,
        grid=(m // ROW_TILE,),
        in_specs=[row, row, row, pl.BlockSpec((1, d), lambda i: (0, 0))],
        out_specs=[pl.BlockSpec((ROW_TILE, d), lambda i: (jnp.minimum(i, n_p - 1), 0)),
                   pl.BlockSpec((ROW_TILE, d), lambda i: (0, 0))],
        out_shape=[jax.ShapeDtypeStruct((T_PROMPT, d), F32), jax.ShapeDtypeStruct((T_SAMPLE, d), F32)],
        compiler_params=_cparams(("arbitrary",)),
        name="final_norm",
    )(h, y_a, y_b, w.reshape(1, d))


def _mm_body(epilogue, a_ref, w_ref, *rest):
    o_ref = rest[-1]
    acc = jnp.dot(a_ref[...], w_ref[...], preferred_element_type=F32)
    if epilogue is not None:
        acc = epilogue(acc, *[r[...] for r in rest[:-1]])
    o_ref[...] = acc.astype(o_ref.dtype)


def _mm(a, w, *, tm, tn, out_dtype, extras=(), epilogue=None, name="mm"):
    m, k = a.shape
    n = w.shape[1]
    in_specs = [pl.BlockSpec((tm, k), lambda i, j: (i, 0)),
                pl.BlockSpec((k, tn), lambda i, j: (0, j))]
    args = [a, w]
    for arr, off in extras:
        in_specs.append(pl.BlockSpec((tm, tn), functools.partial(lambda i, j, off: (i, j + off), off=off)))
        args.append(arr)
    return pl.pallas_call(
        functools.partial(_mm_body, epilogue),
        grid=(m // tm, n // tn),
        in_specs=in_specs,
        out_specs=pl.BlockSpec((tm, tn), lambda i, j: (i, j)),
        out_shape=jax.ShapeDtypeStruct((m, n), out_dtype),
        compiler_params=_cparams(("parallel", "parallel")),
        name=name,
    )(*args)


def _mla_prep_body(s_ref, qw_ref, kvw_ref, cos_ref, sin_ref, cqn_ref, ckv_ref, ckvb_ref, kr_ref, krb_ref):
    cqn_ref[...] = _rms(s_ref[:, SM_CQ:SM_CQ + Q_LORA], qw_ref[...]).astype(BF16)
    c = _rms(s_ref[:, SM_CKV:SM_CKV + KV_LORA], kvw_ref[...])
    ckv_ref[...] = c
    ckvb_ref[...] = c.astype(BF16)
    kr = (s_ref[:, SM_KR:SM_KR + QK_ROPE] * cos_ref[...]
          + s_ref[:, SM_KRS:SM_KRS + QK_ROPE] * sin_ref[...])
    kr_ref[...] = kr
    krb_ref[...] = kr.astype(BF16)


def _mla_prep(small, q_norm_w, kv_norm_w, cos2, sin2):
    m = small.shape[0]
    row = lambda n: pl.BlockSpec((ROW_TILE, n), lambda i: (i, 0))
    vec = lambda n: pl.BlockSpec((1, n), lambda i: (0, 0))
    return pl.pallas_call(
        _mla_prep_body,
        grid=(m // ROW_TILE,),
        in_specs=[row(SM_N), vec(Q_LORA), vec(KV_LORA), row(QK_ROPE), row(QK_ROPE)],
        out_specs=[row(Q_LORA), row(KV_LORA), row(KV_LORA), row(QK_ROPE), row(QK_ROPE)],
        out_shape=[jax.ShapeDtypeStruct((m, Q_LORA), BF16),
                   jax.ShapeDtypeStruct((m, KV_LORA), F32),
                   jax.ShapeDtypeStruct((m, KV_LORA), BF16),
                   jax.ShapeDtypeStruct((m, QK_ROPE), F32),
                   jax.ShapeDtypeStruct((m, QK_ROPE), BF16)],
        compiler_params=_cparams(("parallel",)),
        name="mla_prep",
    )(small, q_norm_w.reshape(1, -1), kv_norm_w.reshape(1, -1), cos2, sin2)


def _q_rope_body(a_ref, w_ref, ws_ref, cos_ref, sin_ref, o_ref):
    reps = o_ref.shape[1] // cos_ref.shape[1]
    a = a_ref[...]
    r = jnp.dot(a, w_ref[...], preferred_element_type=F32)
    rs = jnp.dot(a, ws_ref[...], preferred_element_type=F32)
    cos = jnp.tile(cos_ref[...], (1, reps))
    sin = jnp.tile(sin_ref[...], (1, reps))
    o_ref[...] = ((r * cos + rs * sin) * QSCALE).astype(o_ref.dtype)


def _q_rope(cqn, w_r, w_rs, cos128, sin128, *, tm, tn):
    m, k = cqn.shape
    n = w_r.shape[1]
    return pl.pallas_call(
        _q_rope_body,
        grid=(m // tm, n // tn),
        in_specs=[pl.BlockSpec((tm, k), lambda i, j: (i, 0)),
                  pl.BlockSpec((k, tn), lambda i, j: (0, j)),
                  pl.BlockSpec((k, tn), lambda i, j: (0, j)),
                  pl.BlockSpec((tm, 128), lambda i, j: (i, 0)),
                  pl.BlockSpec((tm, 128), lambda i, j: (i, 0))],
        out_specs=pl.BlockSpec((tm, tn), lambda i, j: (i, j)),
        out_shape=jax.ShapeDtypeStruct((m, n), BF16),
        compiler_params=_cparams(("parallel", "parallel")),
        name="q_rope",
    )(cqn, w_r, w_rs, cos128, sin128)


def _mm_t_body(w_ref, a_ref, o_ref):
    o_ref[0] = _dot_nt(w_ref[...], a_ref[...]).astype(o_ref.dtype)


def _mm_t(w_t, a, *, rows, tm, tn, out_dtype, name):
    n, k = w_t.shape
    return pl.pallas_call(
        _mm_t_body,
        grid=(rows // tm, n // tn),
        in_specs=[pl.BlockSpec((tn, k), lambda i, j: (j, 0)),
                  pl.BlockSpec((tm, k), lambda i, j: (i, 0))],
        out_specs=pl.BlockSpec((1, tn, tm), lambda i, j: (i, j, 0)),
        out_shape=jax.ShapeDtypeStruct((rows // tm, n, tm), out_dtype),
        compiler_params=_cparams(("parallel", "parallel")),
        name=name,
    )(w_t, a)


ATTN_KP = 256


ATTN_VP = 144


def _attn_prompt_body(qn_ref, qr_ref, kn_ref, kr_ref, vt_ref, base_ref, o_ref, kf_sc, vf_sc, qf_sc, acc_sc):
    qi = pl.program_id(2)
    tq = qn_ref.shape[0]
    tk = vt_ref.shape[2]
    seq = kn_ref.shape[0]
    pad = ATTN_KP - QK_NOPE - QK_ROPE

    @pl.when(qi == 0)
    def _():
        def fill(i, c):
            r = pl.ds(pl.multiple_of(i * tk, tk), tk)
            kr = kr_ref[r, :]
            for hh in range(2):
                c0 = hh * ATTN_KP
                kf_sc[r, c0:c0 + QK_NOPE] = kn_ref[r, hh * QK_NOPE:(hh + 1) * QK_NOPE]
                kf_sc[r, c0 + QK_NOPE:c0 + QK_NOPE + QK_ROPE] = kr
                kf_sc[r, c0 + QK_NOPE + QK_ROPE:c0 + ATTN_KP] = jnp.zeros((tk, pad), BF16)
                vf_sc[i, hh * ATTN_VP:hh * ATTN_VP + V_HEAD, :] = vt_ref[i, hh * V_HEAD:(hh + 1) * V_HEAD, :]
                vf_sc[i, hh * ATTN_VP + V_HEAD:(hh + 1) * ATTN_VP, :] = jnp.ones((ATTN_VP - V_HEAD, tk), BF16)
            return c
        lax.fori_loop(0, seq // tk, fill, 0)

    for hh in range(2):
        c0 = hh * ATTN_KP
        qf_sc[:, c0:c0 + QK_NOPE] = qn_ref[:, hh * QK_NOPE:(hh + 1) * QK_NOPE]
        qf_sc[:, c0 + QK_NOPE:c0 + QK_NOPE + QK_ROPE] = qr_ref[:, hh * QK_ROPE:(hh + 1) * QK_ROPE]
        qf_sc[:, c0 + QK_NOPE + QK_ROPE:c0 + ATTN_KP] = jnp.zeros((tq, pad), BF16)
    acc_sc[...] = jnp.zeros(acc_sc.shape, F32)

    def block(kb, carry, diag):
        r = pl.ds(pl.multiple_of(kb * tk, tk), tk)
        q0 = 0 if diag is None else diag * tk
        nq_cols = tq - q0
        out = []
        for hh in range(2):
            m = carry[hh][:, q0:]
            st = _dot_nt(kf_sc[r, hh * ATTN_KP:(hh + 1) * ATTN_KP], qf_sc[q0:, hh * ATTN_KP:(hh + 1) * ATTN_KP])
            if diag is not None:
                kc = lax.broadcasted_iota(jnp.int32, (tk, nq_cols), 0) // CHUNK
                qc = lax.broadcasted_iota(jnp.int32, (tk, nq_cols), 1) // CHUNK
                st = jnp.where(kc <= qc, st, -jnp.inf)
            m_new = jnp.maximum(m, jnp.max(st, axis=0, keepdims=True))
            alpha = jnp.exp2(m - m_new)
            pt = jnp.exp2(st - m_new).astype(BF16)
            vt = vf_sc[kb, hh * ATTN_VP:(hh + 1) * ATTN_VP, :]
            acc_sc[hh, :, q0:] = alpha * acc_sc[hh, :, q0:] + jnp.dot(vt, pt, preferred_element_type=F32)
            out.append(m_new if q0 == 0 else jnp.concatenate([carry[hh][:, :q0], m_new], axis=1))
        return tuple(out)

    n_diag = tq // tk
    n_full = qi * n_diag
    init = (jnp.full((1, tq), -jnp.inf, F32),) * 2
    carry = lax.fori_loop(0, n_full, lambda kb, c: block(kb, c, None), init)
    for d in range(n_diag):
        carry = block(n_full + d, carry, d)
    for hh in range(2):
        o = acc_sc[hh, 0:V_HEAD, :] / acc_sc[hh, V_HEAD:V_HEAD + 1, :]
        o_ref[:, hh * V_HEAD:(hh + 1) * V_HEAD] = o.T.astype(o_ref.dtype)


def _attn_prompt(qn, qr, kn, krb, vt, base, *, nb, seq, heads, tq):
    tk = vt.shape[2]
    nq = seq // tq
    nk = seq // tk
    return pl.pallas_call(
        _attn_prompt_body,
        grid=(nb, heads // 2, nq),
        in_specs=[pl.BlockSpec((tq, 2 * QK_NOPE), lambda b, hp, qi: (b * nq + qi, hp)),
                  pl.BlockSpec((tq, 2 * QK_ROPE), lambda b, hp, qi: (b * nq + qi, hp)),
                  pl.BlockSpec((seq, 2 * QK_NOPE), lambda b, hp, qi: (b, hp)),
                  pl.BlockSpec((seq, QK_ROPE), lambda b, hp, qi: (b, 0)),
                  pl.BlockSpec((nk, 2 * V_HEAD, tk), lambda b, hp, qi: (b, hp, 0)),
                  pl.BlockSpec(memory_space=pl.ANY)],
        out_specs=pl.BlockSpec((tq, 2 * V_HEAD), lambda b, hp, qi: (b * nq + qi, hp)),
        out_shape=jax.ShapeDtypeStruct(base.shape, base.dtype),
        input_output_aliases={5: 0},
        scratch_shapes=[pltpu.VMEM((seq, 2 * ATTN_KP), BF16), pltpu.VMEM((nk, 2 * ATTN_VP, tk), BF16),
                        pltpu.VMEM((tq, 2 * ATTN_KP), BF16), pltpu.VMEM((2, ATTN_VP, tq), F32)],
        compiler_params=_cparams(("parallel", "parallel", "arbitrary")),
        name="attn_prompt",
    )(qn, qr, kn, krb, vt, base)


def _bmm_body(a_ref, w_ref, o_ref):
    o_ref[0] = jnp.dot(a_ref[0], w_ref[0], preferred_element_type=F32).astype(o_ref.dtype)


def _bmm(a, w, out_dtype, name):
    h, m, k = a.shape
    n = w.shape[2]
    return pl.pallas_call(
        _bmm_body,
        grid=(h,),
        in_specs=[pl.BlockSpec((1, m, k), lambda i: (i, 0, 0)),
                  pl.BlockSpec((1, k, n), lambda i: (i, 0, 0))],
        out_specs=pl.BlockSpec((1, m, n), lambda i: (i, 0, 0)),
        out_shape=jax.ShapeDtypeStruct((h, m, n), out_dtype),
        compiler_params=_cparams(("parallel",)),
        name=name,
    )(a, w)


def _attn_sample_body(ql_ref, qr_ref, cc_ref, ck_ref, nc_ref, nk_ref, o_ref, ccb_sc, ckb_sc):
    @pl.when(pl.program_id(1) == 0)
    def _():
        ccb_sc[...] = cc_ref[0].astype(BF16)
        ckb_sc[...] = ck_ref[0].astype(BF16)

    ql = ql_ref[0]
    qr = qr_ref[0]
    ncb = nc_ref[...].astype(BF16)
    nkb = nk_ref[...].astype(BF16)
    s1 = _dot_nt(ql, ccb_sc[...]) + _dot_nt(qr, ckb_sc[...])
    s2 = _dot_nt(ql, ncb) + _dot_nt(qr, nkb)
    m = jnp.maximum(jnp.max(s1, axis=1, keepdims=True), jnp.max(s2, axis=1, keepdims=True))
    p1 = jnp.exp2(s1 - m)
    p2 = jnp.exp2(s2 - m)
    l = jnp.sum(p1, axis=1, keepdims=True) + jnp.sum(p2, axis=1, keepdims=True)
    o = (jnp.dot(p1.astype(BF16), ccb_sc[...], preferred_element_type=F32)
         + jnp.dot(p2.astype(BF16), ncb, preferred_element_type=F32))
    o_ref[0] = (o / l).astype(o_ref.dtype)


def _attn_sample(q_lat, q_rope, cache_ckv, cache_krope, ckv, krope):
    rows = q_lat.shape[1]
    tr = 256
    new_blk = T_PROMPT // DEC_SEQ
    return pl.pallas_call(
        _attn_sample_body,
        grid=(DEC_BATCH, rows // tr),
        in_specs=[pl.BlockSpec((1, tr, KV_LORA), lambda b, r: (b, r, 0)),
                  pl.BlockSpec((1, tr, QK_ROPE), lambda b, r: (b, r, 0)),
                  pl.BlockSpec((1, PAST_LEN, KV_LORA), lambda b, r: (b, 0, 0)),
                  pl.BlockSpec((1, PAST_LEN, QK_ROPE), lambda b, r: (b, 0, 0)),
                  pl.BlockSpec((DEC_SEQ, KV_LORA), lambda b, r: (new_blk + b, 0)),
                  pl.BlockSpec((DEC_SEQ, QK_ROPE), lambda b, r: (new_blk + b, 0))],
        out_specs=pl.BlockSpec((1, tr, KV_LORA), lambda b, r: (b, r, 0)),
        out_shape=jax.ShapeDtypeStruct((DEC_BATCH, rows, KV_LORA), BF16),
        scratch_shapes=[pltpu.VMEM((PAST_LEN, KV_LORA), BF16), pltpu.VMEM((PAST_LEN, QK_ROPE), BF16)],
        compiler_params=_cparams(("parallel", "arbitrary")),
        name="attn_sample",
    )(q_lat, q_rope, cache_ckv, cache_krope, ckv, krope)


def _conv_body(x_ref, prev_ref, w_ref, b_ref, o_ref, ext_sc):
    tl = x_ref.shape[0]

    @pl.when(pl.program_id(2) == 0)
    def _():
        ext_sc[0:8, :] = prev_ref[0]

    @pl.when(pl.program_id(2) != 0)
    def _():
        ext_sc[0:8, :] = ext_sc[tl:tl + 8, :]

    ext_sc[8:8 + tl, :] = x_ref[...]
    acc = b_ref[...] + ext_sc[5:5 + tl, :] * w_ref[0:1, :]
    for k in range(1, SSD_CONV):
        acc = acc + ext_sc[5 + k:5 + k + tl, :] * w_ref[k:k + 1, :]
    o_ref[...] = (acc * _sigmoid(acc)).astype(o_ref.dtype)


def _conv_silu(big, prev8, conv_w, conv_b, *, nb, seq, tl, row_off):
    tc = 1024
    nrt = seq // tl
    rb0 = row_off // tl
    cb0 = BIG_XBC // tc
    return pl.pallas_call(
        _conv_body,
        grid=(nb, SSD_CONV_DIM // tc, nrt),
        in_specs=[pl.BlockSpec((tl, tc), lambda b, j, r: (rb0 + b * nrt + r, cb0 + j)),
                  pl.BlockSpec((1, 8, tc), lambda b, j, r: (b, 0, j)),
                  pl.BlockSpec((SSD_CONV, tc), lambda b, j, r: (0, j)),
                  pl.BlockSpec((1, tc), lambda b, j, r: (0, j))],
        out_specs=pl.BlockSpec((tl, tc), lambda b, j, r: (b * nrt + r, j)),
        out_shape=jax.ShapeDtypeStruct((nb * seq, SSD_CONV_DIM), BF16),
        scratch_shapes=[pltpu.VMEM((tl + 8, tc), F32)],
        compiler_params=_cparams(("parallel", "parallel", "arbitrary")),
        name="conv_silu",
    )(big, prev8, conv_w, conv_b.reshape(1, -1))


def _softplus(x):
    return jnp.maximum(x, 0.0) + jnp.log(1.0 + jnp.exp(-jnp.abs(x)))


def _dot_sel(sel, a, *, sel_left):
    hi = a.astype(BF16)
    r1 = a - hi.astype(F32)
    mid = r1.astype(BF16)
    lo = (r1 - mid.astype(F32)).astype(BF16)
    out = None
    for term in (hi, mid, lo):
        d = (jnp.dot(sel, term, preferred_element_type=F32) if sel_left
             else jnp.dot(term, sel, preferred_element_type=F32))
        out = d if out is None else out + d
    return out


def _ssd_body(has_h0, x_ref, b_ref, c_ref, z_ref, dt_ref, bias_ref, al_ref, dsk_ref, nw_ref, *rest):
    h0_ref = rest[0] if has_h0 else None
    y_ref, hout_ref, ht_sc = rest[-3:]
    R, P = SSD_RANK, SSD_HEADDIM
    lc = P
    W = 2 * P
    c_idx = pl.program_id(2)

    @pl.when(c_idx == 0)
    def _():
        if has_h0:
            ht_sc[...] = h0_ref[0].T
        else:
            ht_sc[...] = jnp.zeros(ht_sc.shape, F32)

    ri = lax.broadcasted_iota(jnp.int32, (lc, lc), 0)
    ci = lax.broadcasted_iota(jnp.int32, (lc, lc), 1)
    tril_b = jnp.where(ri >= ci, 1.0, 0.0).astype(BF16)
    er = lax.broadcasted_iota(jnp.int32, (R, R * P), 0)
    ec = lax.broadcasted_iota(jnp.int32, (R, R * P), 1) // P
    expand = jnp.where(er == ec, 1.0, 0.0).astype(BF16)
    li = lax.broadcasted_iota(jnp.int32, (lc, R * P), 0)
    si = lax.broadcasted_iota(jnp.int32, (lc, R * P), 1) % P
    first_head = lax.broadcasted_iota(jnp.int32, (lc, W), 1) < P
    neg_a = -jnp.exp(al_ref[0])

    ht = ht_sc[...]
    for k in range(x_ref.shape[0] // lc):
        rows = slice(k * lc, (k + 1) * lc)
        dt = _softplus(dt_ref[0, rows, :] + bias_ref[0])
        a_cs = _dot_sel(tril_b, dt * neg_a, sel_left=True)
        a_exp = _dot_sel(expand, a_cs, sel_left=False)
        dt_exp = _dot_sel(expand, dt, sel_left=False)
        a_last = a_exp[lc - 1:lc, :]
        a_key = jnp.sum(jnp.where(li == si, a_exp, 0.0), axis=0, keepdims=True)
        decay = jnp.exp(jnp.where(li >= si, a_exp - a_key, -jnp.inf))

        x = x_ref[rows, :].astype(F32)
        xdt = x * dt_exp
        bm = b_ref[rows, :].astype(BF16)
        cm = c_ref[rows, :].astype(BF16)
        cb2 = _dot_nt(cm, jnp.concatenate([bm, bm], axis=0))
        m_all = (jnp.tile(cb2, (1, R // 2)) * decay).astype(BF16)
        y_off = jnp.dot(cm, ht.astype(BF16), preferred_element_type=F32) * jnp.exp(a_exp)
        xdt_b = xdt.astype(BF16)
        y_diag = []
        for j in range(R // 2):
            xp = xdt_b[:, j * W:(j + 1) * W]
            zero = jnp.zeros_like(xp)
            stack = jnp.concatenate([jnp.where(first_head, xp, zero), jnp.where(first_head, zero, xp)], axis=0)
            y_diag.append(jnp.dot(m_all[:, j * W:(j + 1) * W], stack, preferred_element_type=F32))
        wgt = (xdt * jnp.exp(a_last - a_exp)).astype(BF16)
        states_t = lax.dot_general(bm, wgt, (((0,), (0,)), ((), ())), preferred_element_type=F32)
        ht = ht * jnp.exp(a_last) + states_t

        y = jnp.concatenate(y_diag, axis=1) + y_off + x * dsk_ref[...]
        z = z_ref[rows, :]
        y = y * (z * _sigmoid(z))
        y = y * lax.rsqrt(jnp.mean(y * y, axis=-1, keepdims=True) + EPS) * nw_ref[...]
        y_ref[rows, :] = y.astype(y_ref.dtype)
    ht_sc[...] = ht

    @pl.when(c_idx == pl.num_programs(2) - 1)
    def _():
        hout_ref[0] = ht.T


def _ssd(xbc, zsrc, dt_g, h0, dt_bias, a_log, d_skip, ssd_norm_w, *, nb, seq, sub, base=None):
    lc = sub * SSD_HEADDIM
    nc = seq // lc
    G, R, GC, N = SSD_GROUPS, SSD_RANK, SSD_GCOLS, SSD_STATE
    row = lambda b, g, c: b * nc + c
    in_specs = [
        pl.BlockSpec((lc, GC), lambda b, g, c: (row(b, g, c), g)),
        pl.BlockSpec((lc, N), lambda b, g, c: (row(b, g, c), SSD_INNER // N + g)),
        pl.BlockSpec((lc, N), lambda b, g, c: (row(b, g, c), SSD_INNER // N + G + g)),
        pl.BlockSpec((lc, GC), lambda b, g, c: (row(b, g, c), g)),
        pl.BlockSpec((1, lc, R), lambda b, g, c: (g, row(b, g, c), 0)),
        pl.BlockSpec((1, 1, R), lambda b, g, c: (g, 0, 0)),
        pl.BlockSpec((1, 1, R), lambda b, g, c: (g, 0, 0)),
        pl.BlockSpec((1, GC), lambda b, g, c: (0, g)),
        pl.BlockSpec((1, GC), lambda b, g, c: (0, g)),
    ]
    args = [xbc, xbc, xbc, zsrc, dt_g, dt_bias.reshape(G, 1, R), a_log.reshape(G, 1, R),
            jnp.repeat(d_skip, SSD_HEADDIM).reshape(1, SSD_INNER), ssd_norm_w.reshape(1, SSD_INNER)]
    if h0 is not None:
        in_specs.append(pl.BlockSpec((1, GC, N), lambda b, g, c: (b, g, 0)))
        args.append(h0)
    aliases = {}
    y_shape = jax.ShapeDtypeStruct((nb * seq, SSD_INNER), BF16)
    if base is not None:
        aliases = {len(args): 0}
        in_specs.append(pl.BlockSpec(memory_space=pl.ANY))
        args.append(base)
        y_shape = jax.ShapeDtypeStruct(base.shape, base.dtype)
    return pl.pallas_call(
        functools.partial(_ssd_body, h0 is not None),
        grid=(nb, G, nc),
        in_specs=in_specs,
        out_specs=[pl.BlockSpec((lc, GC), lambda b, g, c: (row(b, g, c), g)),
                   pl.BlockSpec((1, GC, N), lambda b, g, c: (b, g, 0))],
        out_shape=[y_shape, jax.ShapeDtypeStruct((nb, SSD_HEADS * SSD_HEADDIM, N), F32)],
        input_output_aliases=aliases,
        scratch_shapes=[pltpu.VMEM((N, GC), F32)],
        compiler_params=_cparams(("parallel", "parallel", "arbitrary")),
        name="ssd",
    )(*args)


def _norm_router_body(xp_ref, xs_ref, mix_ref, w_ref, wr_ref, br_ref, h_ref, xn_ref, lg_ref):
    h = _stacked_x(xp_ref, xs_ref) + mix_ref[...]
    h_ref[...] = h
    xn = _rms(h, w_ref[...])
    xn_ref[...] = xn.astype(BF16)
    lg_ref[...] = jnp.dot(xn, wr_ref[...], precision=HI, preferred_element_type=F32) + br_ref[...]


def _norm_router(x_p, x_s, mix, norm_w, w_router, b_router):
    m, d = mix.shape
    n = w_router.shape[1]
    row = pl.BlockSpec((ROW_TILE, d), lambda i: (i, 0))
    return pl.pallas_call(
        _norm_router_body,
        grid=(m // ROW_TILE,),
        in_specs=_stacked_x_specs() + [row,
                                       pl.BlockSpec((1, d), lambda i: (0, 0)),
                                       pl.BlockSpec((d, n), lambda i: (0, 0)),
                                       pl.BlockSpec((1, n), lambda i: (0, 0))],
        out_specs=[row, row, pl.BlockSpec((ROW_TILE, n), lambda i: (i, 0))],
        out_shape=[jax.ShapeDtypeStruct((m, d), F32), jax.ShapeDtypeStruct((m, d), BF16),
                   jax.ShapeDtypeStruct((m, n), F32)],
        compiler_params=_cparams(("parallel",)),
        name="norm_router",
    )(x_p, x_s, mix, norm_w.reshape(1, d), w_router, b_router)


def _moe_up_body(be_ref, nused_ref, win_ref, x_ref, wg_ref, wu_ref, o_ref):
    blk = pl.program_id(1)

    @pl.when(blk < nused_ref[0])
    def _():
        x = x_ref[...]
        g = jnp.dot(x, wg_ref[0].astype(BF16), preferred_element_type=F32)
        u = jnp.dot(x, wu_ref[0].astype(BF16), preferred_element_type=F32)
        o_ref[...] = (g * _sigmoid(g) * u).astype(o_ref.dtype)

    @pl.when(blk >= nused_ref[0])
    def _():
        o_ref[...] = jnp.zeros(o_ref.shape, o_ref.dtype)


def _moe_up(be, nused, win, xs, w_gate, w_up):
    d = xs.shape[1]
    nblk = be.shape[0]
    grid_spec = pltpu.PrefetchScalarGridSpec(
        num_scalar_prefetch=3,
        grid=(D_EXPERT // MOE_TF, nblk),
        in_specs=[pl.BlockSpec((pl.Element(MOE_TM), pl.Element(d)),
                               lambda f, i, be, nu, win: (win[i] * MOE_GATHER_ALIGN, 0)),
                  pl.BlockSpec((1, d, MOE_TF), lambda f, i, be, nu, win: (be[i], 0, f)),
                  pl.BlockSpec((1, d, MOE_TF), lambda f, i, be, nu, win: (be[i], 0, f))],
        out_specs=pl.BlockSpec((MOE_TM, MOE_TF), lambda f, i, be, nu, win: (i, f)),
    )
    return pl.pallas_call(
        _moe_up_body,
        grid_spec=grid_spec,
        out_shape=jax.ShapeDtypeStruct((nblk * MOE_TM, D_EXPERT), BF16),
        compiler_params=_cparams(("arbitrary", "arbitrary")),
        name="moe_up",
    )(be, nused, win, xs, w_gate, w_up)


def _moe_down_body(be_ref, nused_ref, h_ref, wd_ref, rw_ref, o_ref):
    blk = pl.program_id(1)

    @pl.when(blk < nused_ref[0])
    def _():
        y = jnp.dot(h_ref[...], wd_ref[0].astype(BF16), preferred_element_type=F32)
        o_ref[...] = (y * rw_ref[...]).astype(o_ref.dtype)

    @pl.when(blk >= nused_ref[0])
    def _():
        o_ref[...] = jnp.zeros(o_ref.shape, o_ref.dtype)


def _moe_down(be, nused, hact, w_down, row_w):
    p, f = hact.shape
    d = w_down.shape[2]
    nblk = p // MOE_TM
    grid_spec = pltpu.PrefetchScalarGridSpec(
        num_scalar_prefetch=2,
        grid=(d // MOE_TN, nblk),
        in_specs=[pl.BlockSpec((MOE_TM, f), lambda n, i, be, nu: (i, 0)),
                  pl.BlockSpec((1, f, MOE_TN), lambda n, i, be, nu: (be[i], 0, n)),
                  pl.BlockSpec((MOE_TM, 1), lambda n, i, be, nu: (i, 0))],
        out_specs=pl.BlockSpec((MOE_TM, MOE_TN), lambda n, i, be, nu: (i, n)),
    )
    return pl.pallas_call(
        _moe_down_body,
        grid_spec=grid_spec,
        out_shape=jax.ShapeDtypeStruct((p, d), BF16),
        compiler_params=_cparams(("arbitrary", "arbitrary")),
        name="moe_down",
    )(be, nused, hact, w_down, row_w)


def _route(logits):
    t = logits.shape[0]
    g_logits = logits[:, :N_GROUPS]
    g_sel = jnp.argmax(g_logits, axis=-1)
    g_w = jnp.max(jax.nn.softmax(g_logits, axis=-1), axis=-1)
    e_logits = logits[:, N_GROUPS:N_GROUPS + N_EXPERTS].reshape(t, N_GROUPS, EXPERTS_PER_GROUP)
    e_in = jnp.take_along_axis(e_logits, g_sel[:, None, None], axis=1)[:, 0]
    e_val, e_idx = lax.top_k(e_in, TOP_K)
    e_w = jax.nn.softmax(e_val, axis=-1) * g_w[:, None]
    expert_id = (g_sel[:, None] * EXPERTS_PER_GROUP + e_idx).reshape(-1).astype(jnp.int32)
    a = t * TOP_K
    e_sorted, order = lax.sort((expert_id, jnp.arange(a, dtype=jnp.int32)), num_keys=1, is_stable=True)
    experts = jnp.arange(N_EXPERTS, dtype=jnp.int32)
    start = jnp.searchsorted(e_sorted, experts, side='left').astype(jnp.int32)
    counts = jnp.searchsorted(e_sorted, experts, side='right').astype(jnp.int32) - start
    padded = (counts + MOE_TM - 1) // MOE_TM * MOE_TM
    pad_end = jnp.cumsum(padded)
    pad_start = pad_end - padded
    shift = pad_start - start
    jumps = shift - jnp.concatenate([jnp.zeros((1,), jnp.int32), shift[:-1]])
    marks = jnp.zeros((a + 1,), jnp.int32).at[start].add(jumps)
    dest = jnp.arange(a, dtype=jnp.int32) + jnp.cumsum(marks)[:a]
    pos = lax.sort((order, dest), num_keys=1)[1].reshape(t, TOP_K)
    nblk = a // MOE_TM + N_EXPERTS
    nused = (pad_end[-1] // MOE_TM).astype(jnp.int32)
    blk = jnp.arange(nblk, dtype=jnp.int32)
    owner = lambda ends, offs: jnp.minimum(jnp.sum((ends[None, :] <= offs[:, None]).astype(jnp.int32), axis=1),
                                           N_EXPERTS - 1)
    be = owner(pad_end, blk * MOE_TM)
    be = jnp.where(blk < nused, be, be[jnp.maximum(nused - 1, 0)])
    per_row = lambda v: jnp.broadcast_to(v[be][:, None], (nblk, MOE_TM)).reshape(-1)
    rank = jnp.arange(nblk * MOE_TM, dtype=jnp.int32) - per_row(pad_start)
    valid = rank < per_row(counts)
    src = jnp.clip(per_row(start) + rank, 0, a - 1)
    row_w = jnp.where(valid, e_w.reshape(-1)[order[src]], 0.0)
    g = MOE_GATHER_ALIGN
    tight = (counts + g - 1) // g * g
    tight_end = jnp.cumsum(tight)
    tight_start = tight_end - tight
    n_mini = MOE_GATHER_ROWS // g
    mini_e = owner(tight_end, jnp.arange(n_mini, dtype=jnp.int32) * g)
    per_mini = lambda v: jnp.broadcast_to(v[mini_e][:, None], (n_mini, g)).reshape(-1)
    grank = jnp.arange(MOE_GATHER_ROWS, dtype=jnp.int32) - per_mini(tight_start)
    gvalid = (grank >= 0) & (grank < per_mini(counts))
    gather_token = jnp.where(gvalid, order[jnp.clip(per_mini(start) + grank, 0, a - 1)] // TOP_K, 0)
    win = (tight_start[be] + blk * MOE_TM - pad_start[be]) // g
    win = jnp.where(blk < nused, win, 0).astype(jnp.int32)
    return gather_token, win, row_w, pos, be, nused.reshape(1)


def _rope_tables():
    half = QK_ROPE // 2
    inv_freq = ROPE_THETA ** (-jnp.arange(half, dtype=F32) / half)
    pos = jnp.concatenate([jnp.tile(jnp.arange(SEQ), BATCH),
                           jnp.tile(PAST_LEN + jnp.arange(DEC_SEQ), DEC_BATCH)]).astype(F32)
    ang = pos[:, None] * inv_freq[None, :]
    cos, sin = jnp.cos(ang), jnp.sin(ang)
    cos2 = jnp.concatenate([cos, cos], axis=1)
    sin2 = jnp.concatenate([-sin, sin], axis=1)
    return cos2, sin2


def kernel(x_prompt, x_sample, cache_ckv, cache_krope, state_conv, state_ssm, norm1_w, w_in, q_norm_w, kv_norm_w, w_uq, w_ukv, conv_w, conv_b, dt_bias, a_log, d_skip, ssd_norm_w, w_mla_o, w_ssd_o, w_out, norm2_w, w_group, b_group, w_erouter, b_erouter, w_gate, w_up, w_down, final_norm_w):
    swap = np.concatenate([np.arange(QK_ROPE // 2, QK_ROPE), np.arange(QK_ROPE // 2)])
    x_p2 = x_prompt.reshape(T_PROMPT, D_MODEL)
    x_s2 = x_sample.reshape(T_SAMPLE, D_MODEL)
    cos2, sin2 = _rope_tables()
    cos128, sin128 = jnp.tile(cos2, (1, 2)), jnp.tile(sin2, (1, 2))

    wi = w_in[0]
    w_kr = wi[:, OFF_KR:OFF_KR + QK_ROPE]
    w_small = jnp.concatenate([wi[:, OFF_CQ:OFF_KR], wi[:, OFF_DT:OFF_DT + SSD_HEADS], w_kr, w_kr[:, swap]],
                              axis=1).astype(BF16)
    w_big = wi[:, OFF_Z:OFF_DT].astype(BF16)
    w_gates = wi[:, OFF_GMLA:].astype(BF16)
    wq = w_uq[0].reshape(Q_LORA, MLA_HEADS, QK_NOPE + QK_ROPE)
    wq_nope = wq[:, :, :QK_NOPE].reshape(Q_LORA, MLA_HEADS * QK_NOPE).astype(BF16)
    wq_rope = wq[:, :, QK_NOPE:]
    wq_r = wq_rope.reshape(Q_LORA, MLA_HEADS * QK_ROPE).astype(BF16)
    wq_rs = wq_rope[:, :, swap].reshape(Q_LORA, MLA_HEADS * QK_ROPE).astype(BF16)
    wkv3 = w_ukv[0].reshape(KV_LORA, MLA_HEADS, QK_NOPE + V_HEAD)
    w_uk_all = wkv3[:, :, :QK_NOPE].reshape(KV_LORA, MLA_HEADS * QK_NOPE).astype(BF16)
    w_uv_all_t = jnp.transpose(wkv3[:, :, QK_NOPE:], (1, 2, 0)).reshape(MLA_HEADS * V_HEAD, KV_LORA).astype(BF16)
    w_uk_t = jnp.transpose(wkv3[:, :, :QK_NOPE], (1, 2, 0)).astype(BF16)
    w_uv_h = jnp.transpose(wkv3[:, :, QK_NOPE:], (1, 0, 2)).astype(BF16)

    u = _rms_rows(x_p2, x_s2, norm1_w[0], BF16)
    small = _mm(u, w_small, tm=MM_TM, tn=SM_N // 2, out_dtype=F32, name="proj_small")
    big = _mm(u, w_big, tm=MM_TM, tn=512, out_dtype=F32, name="proj_big")
    gates = _mm(u, w_gates, tm=MM_TM, tn=512, out_dtype=F32, name="proj_gates")

    cqn, ckv, ckv_b, krope, krope_b = _mla_prep(small, q_norm_w[0], kv_norm_w[0], cos2, sin2)
    qn = _mm(cqn, wq_nope, tm=MM_TM, tn=1024, out_dtype=BF16, epilogue=lambda acc: acc * QSCALE, name="q_nope")
    qr = _q_rope(cqn, wq_r, wq_rs, cos128, sin128, tm=MM_TM, tn=1024)
    kn = _mm(ckv_b, w_uk_all, tm=MM_TM, tn=1024, out_dtype=BF16, name="k_up")
    vt = _mm_t(w_uv_all_t, ckv_b, rows=T_PROMPT, tm=ATTN_TK, tn=1024, out_dtype=BF16, name="v_up_t")

    qn_s = qn[T_PROMPT:].reshape(T_SAMPLE, MLA_HEADS, QK_NOPE).transpose(1, 0, 2)
    q_lat = _bmm(qn_s, w_uk_t, BF16, "q_absorb")
    q_lat = q_lat.reshape(MLA_HEADS, DEC_BATCH, DEC_SEQ, KV_LORA).transpose(1, 0, 2, 3)
    q_lat = q_lat.reshape(DEC_BATCH, MLA_HEADS * DEC_SEQ, KV_LORA)
    qr_s = qr[T_PROMPT:].reshape(DEC_BATCH, DEC_SEQ, MLA_HEADS, QK_ROPE).transpose(0, 2, 1, 3)
    qr_s = qr_s.reshape(DEC_BATCH, MLA_HEADS * DEC_SEQ, QK_ROPE)
    o_lat = _attn_sample(q_lat, qr_s, cache_ckv[0], cache_krope[0], ckv, krope)
    o_lat = o_lat.reshape(DEC_BATCH, MLA_HEADS, DEC_SEQ, KV_LORA).transpose(1, 0, 2, 3)
    o_lat = o_lat.reshape(MLA_HEADS, T_SAMPLE, KV_LORA)
    o_mla_s = _bmm(o_lat, w_uv_h, BF16, "v_absorb")
    o_mla_s = o_mla_s.transpose(1, 0, 2).reshape(T_SAMPLE, MLA_HEADS * V_HEAD)
    o_mla = _attn_prompt(qn, qr, kn, krope_b, vt, jnp.pad(o_mla_s, ((T_PROMPT, 0), (0, 0))),
                         nb=BATCH, seq=SEQ, heads=MLA_HEADS, tq=ATTN_TQ)

    prev_p = jnp.zeros((BATCH, 8, SSD_CONV_DIM), F32)
    prev_s = jnp.concatenate([jnp.zeros((DEC_BATCH, 8 - (SSD_CONV - 1), SSD_CONV_DIM), F32), state_conv[0]], axis=1)
    xbc_p = _conv_silu(big, prev_p, conv_w[0], conv_b[0], nb=BATCH, seq=SEQ, tl=512, row_off=0)
    xbc_s = _conv_silu(big, prev_s, conv_w[0], conv_b[0], nb=DEC_BATCH, seq=DEC_SEQ, tl=DEC_SEQ, row_off=T_PROMPT)
    dt_raw = small[:, SM_DT:SM_DT + SSD_HEADS]
    by_group = lambda d: d.reshape(d.shape[0], SSD_GROUPS, SSD_RANK).transpose(1, 0, 2)

    def pad_seq(a, fill):
        a = a.reshape(DEC_BATCH, DEC_SEQ, a.shape[-1])
        a = jnp.pad(a, ((0, 0), (0, CHUNK - DEC_SEQ), (0, 0)), constant_values=fill)
        return a.reshape(DEC_BATCH * CHUNK, a.shape[-1])

    h0_s = state_ssm[0].reshape(DEC_BATCH, SSD_HEADS * SSD_HEADDIM, SSD_STATE)
    y_s, ssm_s = _ssd(pad_seq(xbc_s, 0.0), pad_seq(big[T_PROMPT:, BIG_Z:BIG_Z + SSD_INNER], 0.0),
                      by_group(pad_seq(dt_raw[T_PROMPT:], -jnp.inf)), h0_s, dt_bias[0], a_log[0], d_skip[0],
                      ssd_norm_w[0], nb=DEC_BATCH, seq=CHUNK, sub=1)
    y_s = y_s.reshape(DEC_BATCH, CHUNK, SSD_INNER)[:, :DEC_SEQ].reshape(T_SAMPLE, SSD_INNER)
    o_ssd, ssm_p = _ssd(xbc_p, big, by_group(dt_raw[:T_PROMPT]), None, dt_bias[0], a_log[0], d_skip[0],
                        ssd_norm_w[0], nb=BATCH, seq=SEQ, sub=SSD_SUB, base=jnp.pad(y_s, ((T_PROMPT, 0), (0, 0))))

    gate = lambda acc, g: _sigmoid(g) * acc
    m1 = _mm(o_mla, w_mla_o[0].astype(BF16), tm=640, tn=1024, out_dtype=F32,
             extras=[(gates, GATE_MLA // 1024)], epilogue=gate, name="mla_out")
    merged = _mm(o_ssd, w_ssd_o[0].astype(BF16), tm=640, tn=512, out_dtype=BF16,
                 extras=[(gates, GATE_SSD // 512), (m1, 0)],
                 epilogue=lambda acc, g, m: m + _sigmoid(g) * acc, name="ssd_out")
    mix = _mm(merged, w_out[0].astype(BF16), tm=640, tn=1024, out_dtype=F32, name="out_proj")

    n_r = 128
    w_router = jnp.concatenate([w_group[0], w_erouter[0],
                                jnp.zeros((D_MODEL, n_r - N_GROUPS - N_EXPERTS), F32)], axis=1)
    b_router = jnp.concatenate([b_group[0], b_erouter[0], jnp.zeros((n_r - N_GROUPS - N_EXPERTS,), F32)])
    h, xn, logits = _norm_router(x_p2, x_s2, mix, norm2_w[0], w_router, b_router.reshape(1, n_r))
    gather_token, win, row_w, pos, be, nused = _route(logits)
    xs = jnp.take(xn, gather_token, axis=0, mode="clip")
    hact = _moe_up(be, nused, win, xs, w_gate[0], w_up[0])
    yb = _moe_down(be, nused, hact, w_down[0], row_w.reshape(-1, 1))
    y_p2, y_s2 = _final_norm(h, jnp.take(yb, pos[:, 0], axis=0, mode="clip"),
                             jnp.take(yb, pos[:, 1], axis=0, mode="clip"), final_norm_w)

    y_prompt = y_p2.reshape(BATCH, SEQ, D_MODEL)
    y_sample = y_s2.reshape(DEC_BATCH, DEC_SEQ, D_MODEL)
    ckv_p = ckv[:T_PROMPT].reshape(1, BATCH, SEQ, KV_LORA)
    ckv_s = ckv[T_PROMPT:].reshape(1, DEC_BATCH, DEC_SEQ, KV_LORA)
    kr_p = krope[:T_PROMPT].reshape(1, BATCH, SEQ, QK_ROPE)
    kr_s = krope[T_PROMPT:].reshape(1, DEC_BATCH, DEC_SEQ, QK_ROPE)
    tail = SSD_CONV - 1
    conv_p = jnp.stack([lax.slice(big, ((b + 1) * SEQ - tail, BIG_XBC), ((b + 1) * SEQ, BIG_XBC + SSD_CONV_DIM))
                        for b in range(BATCH)])[None]
    conv_s = jnp.stack([lax.slice(big, (T_PROMPT + (b + 1) * DEC_SEQ - tail, BIG_XBC),
                                  (T_PROMPT + (b + 1) * DEC_SEQ, BIG_XBC + SSD_CONV_DIM))
                        for b in range(DEC_BATCH)])[None]
    ssm_p = ssm_p.reshape(1, BATCH, SSD_HEADS, SSD_HEADDIM, SSD_STATE)
    ssm_s = ssm_s.reshape(1, DEC_BATCH, SSD_HEADS, SSD_HEADDIM, SSD_STATE)
    return (y_prompt, y_sample, ckv_p, kr_p, conv_p, ssm_p, ckv_s, kr_s, conv_s, ssm_s)
```

```python
import functools
import math

import numpy as np
import jax
import jax.numpy as jnp
from jax import lax
from jax.experimental import pallas as pl
from jax.experimental.pallas import tpu as pltpu

F32 = jnp.float32
BF16 = jnp.bfloat16
HI = lax.Precision.HIGHEST

D_MODEL = 4096
BATCH = 2
SEQ = 8192
DEC_BATCH = 8
DEC_SEQ = 32
PAST_LEN = 4096
CHUNK = 64
EPS = 1e-6
MLA_HEADS = 32
Q_LORA = 1024
KV_LORA = 512
QK_NOPE = 128
QK_ROPE = 64
V_HEAD = 128
ROPE_THETA = 10000.0
SCALE = (QK_NOPE + QK_ROPE) ** -0.5
QSCALE = SCALE * math.log2(math.e)
SSD_INNER = 2 * D_MODEL
SSD_HEADDIM = 64
SSD_HEADS = SSD_INNER // SSD_HEADDIM
SSD_STATE = 128
SSD_GROUPS = 8
SSD_RANK = SSD_HEADS // SSD_GROUPS
SSD_GCOLS = SSD_RANK * SSD_HEADDIM
SSD_CONV = 4
SSD_CONV_DIM = SSD_INNER + 2 * SSD_GROUPS * SSD_STATE
N_GROUPS = 8
EXPERTS_PER_GROUP = 8
N_EXPERTS = N_GROUPS * EXPERTS_PER_GROUP
TOP_K = 2
D_EXPERT = 1024

T_PROMPT = BATCH * SEQ
T_SAMPLE = DEC_BATCH * DEC_SEQ
T_ALL = T_PROMPT + T_SAMPLE

OFF_CQ = 0
OFF_CKV = OFF_CQ + Q_LORA
OFF_KR = OFF_CKV + KV_LORA
OFF_Z = OFF_KR + QK_ROPE
OFF_XBC = OFF_Z + SSD_INNER
OFF_DT = OFF_XBC + SSD_CONV_DIM
OFF_GMLA = OFF_DT + SSD_HEADS
OFF_GSSD = OFF_GMLA + D_MODEL
N_IN = OFF_GSSD + D_MODEL

BIG_Z = 0
BIG_XBC = SSD_INNER
BIG_N = BIG_XBC + SSD_CONV_DIM
GATE_MLA = 0
GATE_SSD = D_MODEL
SM_CQ = 0
SM_CKV = Q_LORA
SM_DT = SM_CKV + KV_LORA
SM_KR = SM_DT + SSD_HEADS
SM_KRS = SM_KR + QK_ROPE
SM_N = SM_KRS + QK_ROPE

V7X_VMEM_LIMIT = 56 * 1024 * 1024
ROW_TILE = 256
MM_TM = 1280
MOE_TM = 512
SSD_SUB = 8
MOE_GATHER_ALIGN = 16
MOE_GATHER_ROWS = T_ALL * TOP_K + N_EXPERTS * MOE_GATHER_ALIGN + MOE_TM
MOE_TF = 512
MOE_TN = 4096
ATTN_TQ = 2048
ATTN_TK = 512


def _cparams(sem):
    return pltpu.CompilerParams(dimension_semantics=sem, vmem_limit_bytes=V7X_VMEM_LIMIT)


def _dot_nt(a, b):
    return lax.dot_general(a, b, (((1,), (1,)), ((), ())), preferred_element_type=F32)


def _rms(x, w):
    return x * lax.rsqrt(jnp.mean(x * x, axis=-1, keepdims=True) + EPS) * w


def _sigmoid(x):
    return 0.5 * jnp.tanh(0.5 * x) + 0.5


def _stacked_x_specs():
    n_p = T_PROMPT // ROW_TILE
    return [pl.BlockSpec((ROW_TILE, D_MODEL), lambda i: (jnp.minimum(i, n_p - 1), 0)),
            pl.BlockSpec((ROW_TILE, D_MODEL), lambda i: (0, 0))]


def _stacked_x(xp_ref, xs_ref):
    is_prompt = pl.program_id(0) < T_PROMPT // ROW_TILE
    return jnp.where(is_prompt, xp_ref[...], xs_ref[...])


def _rms_rows_body(xp_ref, xs_ref, w_ref, o_ref):
    o_ref[...] = _rms(_stacked_x(xp_ref, xs_ref), w_ref[...]).astype(o_ref.dtype)


def _rms_rows(x_p, x_s, w, out_dtype):
    d = D_MODEL
    return pl.pallas_call(
        _rms_rows_body,
        grid=(T_ALL // ROW_TILE,),
        in_specs=_stacked_x_specs() + [pl.BlockSpec((1, d), lambda i: (0, 0))],
        out_specs=pl.BlockSpec((ROW_TILE, d), lambda i: (i, 0)),
        out_shape=jax.ShapeDtypeStruct((T_ALL, d), out_dtype),
        compiler_params=_cparams(("parallel",)),
        name="rms_rows",
    )(x_p, x_s, w.reshape(1, d))


def _final_body(h_ref, ya_ref, yb_ref, w_ref, op_ref, os_ref):
    y = ya_ref[...].astype(F32) + yb_ref[...].astype(F32)
    out = _rms(h_ref[...] + y, w_ref[...])
    is_prompt = pl.program_id(0) < T_PROMPT // ROW_TILE

    @pl.when(is_prompt)
    def _():
        op_ref[...] = out

    @pl.when(jnp.logical_not(is_prompt))
    def _():
        os_ref[...] = out


def _final_norm(h, y_a, y_b, w):
    m, d = h.shape
    n_p = T_PROMPT // ROW_TILE
    row = pl.BlockSpec((ROW_TILE, d), lambda i: (i, 0))
    return pl.pallas_call(
        _final_body,
        grid=(m // ROW_TILE,),
        in_specs=[row, row, row, pl.BlockSpec((1, d), lambda i: (0, 0))],
        out_specs=[pl.BlockSpec((ROW_TILE, d), lambda i: (jnp.minimum(i, n_p - 1), 0)),
                   pl.BlockSpec((ROW_TILE, d), lambda i: (0, 0))],
        out_shape=[jax.ShapeDtypeStruct((T_PROMPT, d), F32), jax.ShapeDtypeStruct((T_SAMPLE, d), F32)],
        compiler_params=_cparams(("arbitrary",)),
        name="final_norm",
    )(h, y_a, y_b, w.reshape(1, d))


def _mm_body(epilogue, a_ref, w_ref, *rest):
    o_ref = rest[-1]
    acc = jnp.dot(a_ref[...], w_ref[...], preferred_element_type=F32)
    if epilogue is not None:
        acc = epilogue(acc, *[r[...] for r in rest[:-1]])
    o_ref[...] = acc.astype(o_ref.dtype)


def _mm(a, w, *, tm, tn, out_dtype, extras=(), epilogue=None, name="mm"):
    m, k = a.shape
    n = w.shape[1]
    in_specs = [pl.BlockSpec((tm, k), lambda i, j: (i, 0)),
                pl.BlockSpec((k, tn), lambda i, j: (0, j))]
    args = [a, w]
    for arr, off in extras:
        in_specs.append(pl.BlockSpec((tm, tn), functools.partial(lambda i, j, off: (i, j + off), off=off)))
        args.append(arr)
    return pl.pallas_call(
        functools.partial(_mm_body, epilogue),
        grid=(m // tm, n // tn),
        in_specs=in_specs,
        out_specs=pl.BlockSpec((tm, tn), lambda i, j: (i, j)),
        out_shape=jax.ShapeDtypeStruct((m, n), out_dtype),
        compiler_params=_cparams(("parallel", "parallel")),
        name=name,
    )(*args)


def _mla_prep_body(s_ref, qw_ref, kvw_ref, cos_ref, sin_ref, cqn_ref, ckv_ref, ckvb_ref, kr_ref, krb_ref):
    cqn_ref[...] = _rms(s_ref[:, SM_CQ:SM_CQ + Q_LORA], qw_ref[...]).astype(BF16)
    c = _rms(s_ref[:, SM_CKV:SM_CKV + KV_LORA], kvw_ref[...])
    ckv_ref[...] = c
    ckvb_ref[...] = c.astype(BF16)
    kr = (s_ref[:, SM_KR:SM_KR + QK_ROPE] * cos_ref[...]
          + s_ref[:, SM_KRS:SM_KRS + QK_ROPE] * sin_ref[...])
    kr_ref[...] = kr
    krb_ref[...] = kr.astype(BF16)


def _mla_prep(small, q_norm_w, kv_norm_w, cos2, sin2):
    m = small.shape[0]
    row = lambda n: pl.BlockSpec((ROW_TILE, n), lambda i: (i, 0))
    vec = lambda n: pl.BlockSpec((1, n), lambda i: (0, 0))
    return pl.pallas_call(
        _mla_prep_body,
        grid=(m // ROW_TILE,),
        in_specs=[row(SM_N), vec(Q_LORA), vec(KV_LORA), row(QK_ROPE), row(QK_ROPE)],
        out_specs=[row(Q_LORA), row(KV_LORA), row(KV_LORA), row(QK_ROPE), row(QK_ROPE)],
        out_shape=[jax.ShapeDtypeStruct((m, Q_LORA), BF16),
                   jax.ShapeDtypeStruct((m, KV_LORA), F32),
                   jax.ShapeDtypeStruct((m, KV_LORA), BF16),
                   jax.ShapeDtypeStruct((m, QK_ROPE), F32),
                   jax.ShapeDtypeStruct((m, QK_ROPE), BF16)],
        compiler_params=_cparams(("parallel",)),
        name="mla_prep",
    )(small, q_norm_w.reshape(1, -1), kv_norm_w.reshape(1, -1), cos2, sin2)


def _q_rope_body(a_ref, w_ref, ws_ref, cos_ref, sin_ref, o_ref):
    reps = o_ref.shape[1] // cos_ref.shape[1]
    a = a_ref[...]
    r = jnp.dot(a, w_ref[...], preferred_element_type=F32)
    rs = jnp.dot(a, ws_ref[...], preferred_element_type=F32)
    cos = jnp.tile(cos_ref[...], (1, reps))
    sin = jnp.tile(sin_ref[...], (1, reps))
    o_ref[...] = ((r * cos + rs * sin) * QSCALE).astype(o_ref.dtype)


def _q_rope(cqn, w_r, w_rs, cos128, sin128, *, tm, tn):
    m, k = cqn.shape
    n = w_r.shape[1]
    return pl.pallas_call(
        _q_rope_body,
        grid=(m // tm, n // tn),
        in_specs=[pl.BlockSpec((tm, k), lambda i, j: (i, 0)),
                  pl.BlockSpec((k, tn), lambda i, j: (0, j)),
                  pl.BlockSpec((k, tn), lambda i, j: (0, j)),
                  pl.BlockSpec((tm, 128), lambda i, j: (i, 0)),
                  pl.BlockSpec((tm, 128), lambda i, j: (i, 0))],
        out_specs=pl.BlockSpec((tm, tn), lambda i, j: (i, j)),
        out_shape=jax.ShapeDtypeStruct((m, n), BF16),
        compiler_params=_cparams(("parallel", "parallel")),
        name="q_rope",
    )(cqn, w_r, w_rs, cos128, sin128)


def _mm_t_body(w_ref, a_ref, o_ref):
    o_ref[0] = _dot_nt(w_ref[...], a_ref[...]).astype(o_ref.dtype)


def _mm_t(w_t, a, *, rows, tm, tn, out_dtype, name):
    n, k = w_t.shape
    return pl.pallas_call(
        _mm_t_body,
        grid=(rows // tm, n // tn),
        in_specs=[pl.BlockSpec((tn, k), lambda i, j: (j, 0)),
                  pl.BlockSpec((tm, k), lambda i, j: (i, 0))],
        out_specs=pl.BlockSpec((1, tn, tm), lambda i, j: (i, j, 0)),
        out_shape=jax.ShapeDtypeStruct((rows // tm, n, tm), out_dtype),
        compiler_params=_cparams(("parallel", "parallel")),
        name=name,
    )(w_t, a)


ATTN_KP = 256


ATTN_VP = 144


def _attn_prompt_body(qn_ref, qr_ref, kn_ref, kr_ref, vt_ref, base_ref, o_ref, kf_sc, vf_sc, qf_sc, acc_sc):
    qi = pl.program_id(2)
    tq = qn_ref.shape[0]
    tk = vt_ref.shape[2]
    seq = kn_ref.shape[0]
    pad = ATTN_KP - QK_NOPE - QK_ROPE

    @pl.when(qi == 0)
    def _():
        def fill(i, c):
            r = pl.ds(pl.multiple_of(i * tk, tk), tk)
            kr = kr_ref[r, :]
            for hh in range(2):
                c0 = hh * ATTN_KP
                kf_sc[r, c0:c0 + QK_NOPE] = kn_ref[r, hh * QK_NOPE:(hh + 1) * QK_NOPE]
                kf_sc[r, c0 + QK_NOPE:c0 + QK_NOPE + QK_ROPE] = kr
                kf_sc[r, c0 + QK_NOPE + QK_ROPE:c0 + ATTN_KP] = jnp.zeros((tk, pad), BF16)
                vf_sc[i, hh * ATTN_VP:hh * ATTN_VP + V_HEAD, :] = vt_ref[i, hh * V_HEAD:(hh + 1) * V_HEAD, :]
                vf_sc[i, hh * ATTN_VP + V_HEAD:(hh + 1) * ATTN_VP, :] = jnp.ones((ATTN_VP - V_HEAD, tk), BF16)
            return c
        lax.fori_loop(0, seq // tk, fill, 0)

    for hh in range(2):
        c0 = hh * ATTN_KP
        qf_sc[:, c0:c0 + QK_NOPE] = qn_ref[:, hh * QK_NOPE:(hh + 1) * QK_NOPE]
        qf_sc[:, c0 + QK_NOPE:c0 + QK_NOPE + QK_ROPE] = qr_ref[:, hh * QK_ROPE:(hh + 1) * QK_ROPE]
        qf_sc[:, c0 + QK_NOPE + QK_ROPE:c0 + ATTN_KP] = jnp.zeros((tq, pad), BF16)
    acc_sc[...] = jnp.zeros(acc_sc.shape, F32)

    def block(kb, carry, diag):
        r = pl.ds(pl.multiple_of(kb * tk, tk), tk)
        q0 = 0 if diag is None else diag * tk
        nq_cols = tq - q0
        out = []
        for hh in range(2):
            m = carry[hh][:, q0:]
            st = _dot_nt(kf_sc[r, hh * ATTN_KP:(hh + 1) * ATTN_KP], qf_sc[q0:, hh * ATTN_KP:(hh + 1) * ATTN_KP])
            if diag is not None:
                kc = lax.broadcasted_iota(jnp.int32, (tk, nq_cols), 0) // CHUNK
                qc = lax.broadcasted_iota(jnp.int32, (tk, nq_cols), 1) // CHUNK
                st = jnp.where(kc <= qc, st, -jnp.inf)
            m_new = jnp.maximum(m, jnp.max(st, axis=0, keepdims=True))
            alpha = jnp.exp2(m - m_new)
            pt = jnp.exp2(st - m_new).astype(BF16)
            vt = vf_sc[kb, hh * ATTN_VP:(hh + 1) * ATTN_VP, :]
            acc_sc[hh, :, q0:] = alpha * acc_sc[hh, :, q0:] + jnp.dot(vt, pt, preferred_element_type=F32)
            out.append(m_new if q0 == 0 else jnp.concatenate([carry[hh][:, :q0], m_new], axis=1))
        return tuple(out)

    n_diag = tq // tk
    n_full = qi * n_diag
    init = (jnp.full((1, tq), -jnp.inf, F32),) * 2
    carry = lax.fori_loop(0, n_full, lambda kb, c: block(kb, c, None), init)
    for d in range(n_diag):
        carry = block(n_full + d, carry, d)
    for hh in range(2):
        o = acc_sc[hh, 0:V_HEAD, :] / acc_sc[hh, V_HEAD:V_HEAD + 1, :]
        o_ref[:, hh * V_HEAD:(hh + 1) * V_HEAD] = o.T.astype(o_ref.dtype)


def _attn_prompt(qn, qr, kn, krb, vt, base, *, nb, seq, heads, tq):
    tk = vt.shape[2]
    nq = seq // tq
    nk = seq // tk
    return pl.pallas_call(
        _attn_prompt_body,
        grid=(nb, heads // 2, nq),
        in_specs=[pl.BlockSpec((tq, 2 * QK_NOPE), lambda b, hp, qi: (b * nq + qi, hp)),
                  pl.BlockSpec((tq, 2 * QK_ROPE), lambda b, hp, qi: (b * nq + qi, hp)),
                  pl.BlockSpec((seq, 2 * QK_NOPE), lambda b, hp, qi: (b, hp)),
                  pl.BlockSpec((seq, QK_ROPE), lambda b, hp, qi: (b, 0)),
                  pl.BlockSpec((nk, 2 * V_HEAD, tk), lambda b, hp, qi: (b, hp, 0)),
                  pl.BlockSpec(memory_space=pl.ANY)],
        out_specs=pl.BlockSpec((tq, 2 * V_HEAD), lambda b, hp, qi: (b * nq + qi, hp)),
        out_shape=jax.ShapeDtypeStruct(base.shape, base.dtype),
        input_output_aliases={5: 0},
        scratch_shapes=[pltpu.VMEM((seq, 2 * ATTN_KP), BF16), pltpu.VMEM((nk, 2 * ATTN_VP, tk), BF16),
                        pltpu.VMEM((tq, 2 * ATTN_KP), BF16), pltpu.VMEM((2, ATTN_VP, tq), F32)],
        compiler_params=_cparams(("parallel", "parallel", "arbitrary")),
        name="attn_prompt",
    )(qn, qr, kn, krb, vt, base)


def _bmm_body(a_ref, w_ref, o_ref):
    o_ref[0] = jnp.dot(a_ref[0], w_ref[0], preferred_element_type=F32).astype(o_ref.dtype)


def _bmm(a, w, out_dtype, name):
    h, m, k = a.shape
    n = w.shape[2]
    return pl.pallas_call(
        _bmm_body,
        grid=(h,),
        in_specs=[pl.BlockSpec((1, m, k), lambda i: (i, 0, 0)),
                  pl.BlockSpec((1, k, n), lambda i: (i, 0, 0))],
        out_specs=pl.BlockSpec((1, m, n), lambda i: (i, 0, 0)),
        out_shape=jax.ShapeDtypeStruct((h, m, n), out_dtype),
        compiler_params=_cparams(("parallel",)),
        name=name,
    )(a, w)


def _attn_sample_body(ql_ref, qr_ref, cc_ref, ck_ref, nc_ref, nk_ref, o_ref, ccb_sc, ckb_sc):
    @pl.when(pl.program_id(1) == 0)
    def _():
        ccb_sc[...] = cc_ref[0].astype(BF16)
        ckb_sc[...] = ck_ref[0].astype(BF16)

    ql = ql_ref[0]
    qr = qr_ref[0]
    ncb = nc_ref[...].astype(BF16)
    nkb = nk_ref[...].astype(BF16)
    s1 = _dot_nt(ql, ccb_sc[...]) + _dot_nt(qr, ckb_sc[...])
    s2 = _dot_nt(ql, ncb) + _dot_nt(qr, nkb)
    m = jnp.maximum(jnp.max(s1, axis=1, keepdims=True), jnp.max(s2, axis=1, keepdims=True))
    p1 = jnp.exp2(s1 - m)
    p2 = jnp.exp2(s2 - m)
    l = jnp.sum(p1, axis=1, keepdims=True) + jnp.sum(p2, axis=1, keepdims=True)
    o = (jnp.dot(p1.astype(BF16), ccb_sc[...], preferred_element_type=F32)
         + jnp.dot(p2.astype(BF16), ncb, preferred_element_type=F32))
    o_ref[0] = (o / l).astype(o_ref.dtype)


def _attn_sample(q_lat, q_rope, cache_ckv, cache_krope, ckv, krope):
    rows = q_lat.shape[1]
    tr = 256
    new_blk = T_PROMPT // DEC_SEQ
    return pl.pallas_call(
        _attn_sample_body,
        grid=(DEC_BATCH, rows // tr),
        in_specs=[pl.BlockSpec((1, tr, KV_LORA), lambda b, r: (b, r, 0)),
                  pl.BlockSpec((1, tr, QK_ROPE), lambda b, r: (b, r, 0)),
                  pl.BlockSpec((1, PAST_LEN, KV_LORA), lambda b, r: (b, 0, 0)),
                  pl.BlockSpec((1, PAST_LEN, QK_ROPE), lambda b, r: (b, 0, 0)),
                  pl.BlockSpec((DEC_SEQ, KV_LORA), lambda b, r: (new_blk + b, 0)),
                  pl.BlockSpec((DEC_SEQ, QK_ROPE), lambda b, r: (new_blk + b, 0))],
        out_specs=pl.BlockSpec((1, tr, KV_LORA), lambda b, r: (b, r, 0)),
        out_shape=jax.ShapeDtypeStruct((DEC_BATCH, rows, KV_LORA), BF16),
        scratch_shapes=[pltpu.VMEM((PAST_LEN, KV_LORA), BF16), pltpu.VMEM((PAST_LEN, QK_ROPE), BF16)],
        compiler_params=_cparams(("parallel", "arbitrary")),
        name="attn_sample",
    )(q_lat, q_rope, cache_ckv, cache_krope, ckv, krope)


def _conv_body(x_ref, prev_ref, w_ref, b_ref, o_ref, ext_sc):
    tl = x_ref.shape[0]

    @pl.when(pl.program_id(2) == 0)
    def _():
        ext_sc[0:8, :] = prev_ref[0]

    @pl.when(pl.program_id(2) != 0)
    def _():
        ext_sc[0:8, :] = ext_sc[tl:tl + 8, :]

    ext_sc[8:8 + tl, :] = x_ref[...]
    acc = b_ref[...] + ext_sc[5:5 + tl, :] * w_ref[0:1, :]
    for k in range(1, SSD_CONV):
        acc = acc + ext_sc[5 + k:5 + k + tl, :] * w_ref[k:k + 1, :]
    o_ref[...] = (acc * _sigmoid(acc)).astype(o_ref.dtype)


def _conv_silu(big, prev8, conv_w, conv_b, *, nb, seq, tl, row_off):
    tc = 1024
    nrt = seq // tl
    rb0 = row_off // tl
    cb0 = BIG_XBC // tc
    return pl.pallas_call(
        _conv_body,
        grid=(nb, SSD_CONV_DIM // tc, nrt),
        in_specs=[pl.BlockSpec((tl, tc), lambda b, j, r: (rb0 + b * nrt + r, cb0 + j)),
                  pl.BlockSpec((1, 8, tc), lambda b, j, r: (b, 0, j)),
                  pl.BlockSpec((SSD_CONV, tc), lambda b, j, r: (0, j)),
                  pl.BlockSpec((1, tc), lambda b, j, r: (0, j))],
        out_specs=pl.BlockSpec((tl, tc), lambda b, j, r: (b * nrt + r, j)),
        out_shape=jax.ShapeDtypeStruct((nb * seq, SSD_CONV_DIM), BF16),
        scratch_shapes=[pltpu.VMEM((tl + 8, tc), F32)],
        compiler_params=_cparams(("parallel", "parallel", "arbitrary")),
        name="conv_silu",
    )(big, prev8, conv_w, conv_b.reshape(1, -1))


def _softplus(x):
    return jnp.maximum(x, 0.0) + jnp.log(1.0 + jnp.exp(-jnp.abs(x)))


def _dot_sel(sel, a, *, sel_left):
    hi = a.astype(BF16)
    r1 = a - hi.astype(F32)
    mid = r1.astype(BF16)
    lo = (r1 - mid.astype(F32)).astype(BF16)
    out = None
    for term in (hi, mid, lo):
        d = (jnp.dot(sel, term, preferred_element_type=F32) if sel_left
             else jnp.dot(term, sel, preferred_element_type=F32))
        out = d if out is None else out + d
    return out


def _ssd_body(has_h0, x_ref, b_ref, c_ref, z_ref, dt_ref, bias_ref, al_ref, dsk_ref, nw_ref, *rest):
    h0_ref = rest[0] if has_h0 else None
    y_ref, hout_ref, ht_sc = rest[-3:]
    R, P = SSD_RANK, SSD_HEADDIM
    lc = P
    W = 2 * P
    c_idx = pl.program_id(2)

    @pl.when(c_idx == 0)
    def _():
        if has_h0:
            ht_sc[...] = h0_ref[0].T
        else:
            ht_sc[...] = jnp.zeros(ht_sc.shape, F32)

    ri = lax.broadcasted_iota(jnp.int32, (lc, lc), 0)
    ci = lax.broadcasted_iota(jnp.int32, (lc, lc), 1)
    tril_b = jnp.where(ri >= ci, 1.0, 0.0).astype(BF16)
    er = lax.broadcasted_iota(jnp.int32, (R, R * P), 0)
    ec = lax.broadcasted_iota(jnp.int32, (R, R * P), 1) // P
    expand = jnp.where(er == ec, 1.0, 0.0).astype(BF16)
    li = lax.broadcasted_iota(jnp.int32, (lc, R * P), 0)
    si = lax.broadcasted_iota(jnp.int32, (lc, R * P), 1) % P
    first_head = lax.broadcasted_iota(jnp.int32, (lc, W), 1) < P
    neg_a = -jnp.exp(al_ref[0])

    ht = ht_sc[...]
    for k in range(x_ref.shape[0] // lc):
        rows = slice(k * lc, (k + 1) * lc)
        dt = _softplus(dt_ref[0, rows, :] + bias_ref[0])
        a_cs = _dot_sel(tril_b, dt * neg_a, sel_left=True)
        a_exp = _dot_sel(expand, a_cs, sel_left=False)
        dt_exp = _dot_sel(expand, dt, sel_left=False)
        a_last = a_exp[lc - 1:lc, :]
        a_key = jnp.sum(jnp.where(li == si, a_exp, 0.0), axis=0, keepdims=True)
        decay = jnp.exp(jnp.where(li >= si, a_exp - a_key, -jnp.inf))

        x = x_ref[rows, :].astype(F32)
        xdt = x * dt_exp
        bm = b_ref[rows, :].astype(BF16)
        cm = c_ref[rows, :].astype(BF16)
        cb2 = _dot_nt(cm, jnp.concatenate([bm, bm], axis=0))
        m_all = (jnp.tile(cb2, (1, R // 2)) * decay).astype(BF16)
        y_off = jnp.dot(cm, ht.astype(BF16), preferred_element_type=F32) * jnp.exp(a_exp)
        xdt_b = xdt.astype(BF16)
        y_diag = []
        for j in range(R // 2):
            xp = xdt_b[:, j * W:(j + 1) * W]
            zero = jnp.zeros_like(xp)
            stack = jnp.concatenate([jnp.where(first_head, xp, zero), jnp.where(first_head, zero, xp)], axis=0)
            y_diag.append(jnp.dot(m_all[:, j * W:(j + 1) * W], stack, preferred_element_type=F32))
        wgt = (xdt * jnp.exp(a_last - a_exp)).astype(BF16)
        states_t = lax.dot_general(bm, wgt, (((0,), (0,)), ((), ())), preferred_element_type=F32)
        ht = ht * jnp.exp(a_last) + states_t

        y = jnp.concatenate(y_diag, axis=1) + y_off + x * dsk_ref[...]
        z = z_ref[rows, :]
        y = y * (z * _sigmoid(z))
        y = y * lax.rsqrt(jnp.mean(y * y, axis=-1, keepdims=True) + EPS) * nw_ref[...]
        y_ref[rows, :] = y.astype(y_ref.dtype)
    ht_sc[...] = ht

    @pl.when(c_idx == pl.num_programs(2) - 1)
    def _():
        hout_ref[0] = ht.T


def _ssd(xbc, zsrc, dt_g, h0, dt_bias, a_log, d_skip, ssd_norm_w, *, nb, seq, sub, base=None):
    lc = sub * SSD_HEADDIM
    nc = seq // lc
    G, R, GC, N = SSD_GROUPS, SSD_RANK, SSD_GCOLS, SSD_STATE
    row = lambda b, g, c: b * nc + c
    in_specs = [
        pl.BlockSpec((lc, GC), lambda b, g, c: (row(b, g, c), g)),
        pl.BlockSpec((lc, N), lambda b, g, c: (row(b, g, c), SSD_INNER // N + g)),
        pl.BlockSpec((lc, N), lambda b, g, c: (row(b, g, c), SSD_INNER // N + G + g)),
        pl.BlockSpec((lc, GC), lambda b, g, c: (row(b, g, c), g)),
        pl.BlockSpec((1, lc, R), lambda b, g, c: (g, row(b, g, c), 0)),
        pl.BlockSpec((1, 1, R), lambda b, g, c: (g, 0, 0)),
        pl.BlockSpec((1, 1, R), lambda b, g, c: (g, 0, 0)),
        pl.BlockSpec((1, GC), lambda b, g, c: (0, g)),
        pl.BlockSpec((1, GC), lambda b, g, c: (0, g)),
    ]
    args = [xbc, xbc, xbc, zsrc, dt_g, dt_bias.reshape(G, 1, R), a_log.reshape(G, 1, R),
            jnp.repeat(d_skip, SSD_HEADDIM).reshape(1, SSD_INNER), ssd_norm_w.reshape(1, SSD_INNER)]
    if h0 is not None:
        in_specs.append(pl.BlockSpec((1, GC, N), lambda b, g, c: (b, g, 0)))
        args.append(h0)
    aliases = {}
    y_shape = jax.ShapeDtypeStruct((nb * seq, SSD_INNER), BF16)
    if base is not None:
        aliases = {len(args): 0}
        in_specs.append(pl.BlockSpec(memory_space=pl.ANY))
        args.append(base)
        y_shape = jax.ShapeDtypeStruct(base.shape, base.dtype)
    return pl.pallas_call(
        functools.partial(_ssd_body, h0 is not None),
        grid=(nb, G, nc),
        in_specs=in_specs,
        out_specs=[pl.BlockSpec((lc, GC), lambda b, g, c: (row(b, g, c), g)),
                   pl.BlockSpec((1, GC, N), lambda b, g, c: (b, g, 0))],
        out_shape=[y_shape, jax.ShapeDtypeStruct((nb, SSD_HEADS * SSD_HEADDIM, N), F32)],
        input_output_aliases=aliases,
        scratch_shapes=[pltpu.VMEM((N, GC), F32)],
        compiler_params=_cparams(("parallel", "parallel", "arbitrary")),
        name="ssd",
    )(*args)


def _norm_router_body(xp_ref, xs_ref, mix_ref, w_ref, wr_ref, br_ref, h_ref, xn_ref, lg_ref):
    h = _stacked_x(xp_ref, xs_ref) + mix_ref[...]
    h_ref[...] = h
    xn = _rms(h, w_ref[...])
    xn_ref[...] = xn.astype(BF16)
    lg_ref[...] = jnp.dot(xn, wr_ref[...], precision=HI, preferred_element_type=F32) + br_ref[...]


def _norm_router(x_p, x_s, mix, norm_w, w_router, b_router):
    m, d = mix.shape
    n = w_router.shape[1]
    row = pl.BlockSpec((ROW_TILE, d), lambda i: (i, 0))
    return pl.pallas_call(
        _norm_router_body,
        grid=(m // ROW_TILE,),
        in_specs=_stacked_x_specs() + [row,
                                       pl.BlockSpec((1, d), lambda i: (0, 0)),
                                       pl.BlockSpec((d, n), lambda i: (0, 0)),
                                       pl.BlockSpec((1, n), lambda i: (0, 0))],
        out_specs=[row, row, pl.BlockSpec((ROW_TILE, n), lambda i: (i, 0))],
        out_shape=[jax.ShapeDtypeStruct((m, d), F32), jax.ShapeDtypeStruct((m, d), BF16),
                   jax.ShapeDtypeStruct((m, n), F32)],
        compiler_params=_cparams(("parallel",)),
        name="norm_router",
    )(x_p, x_s, mix, norm_w.reshape(1, d), w_router, b_router)


def _moe_up_body(be_ref, nused_ref, win_ref, x_ref, wg_ref, wu_ref, o_ref):
    blk = pl.program_id(1)

    @pl.when(blk < nused_ref[0])
    def _():
        x = x_ref[...]
        g = jnp.dot(x, wg_ref[0].astype(BF16), preferred_element_type=F32)
        u = jnp.dot(x, wu_ref[0].astype(BF16), preferred_element_type=F32)
        o_ref[...] = (g * _sigmoid(g) * u).astype(o_ref.dtype)

    @pl.when(blk >= nused_ref[0])
    def _():
        o_ref[...] = jnp.zeros(o_ref.shape, o_ref.dtype)


def _moe_up(be, nused, win, xs, w_gate, w_up):
    d = xs.shape[1]
    nblk = be.shape[0]
    grid_spec = pltpu.PrefetchScalarGridSpec(
        num_scalar_prefetch=3,
        grid=(D_EXPERT // MOE_TF, nblk),
        in_specs=[pl.BlockSpec((pl.Element(MOE_TM), pl.Element(d)),
                               lambda f, i, be, nu, win: (win[i] * MOE_GATHER_ALIGN, 0)),
                  pl.BlockSpec((1, d, MOE_TF), lambda f, i, be, nu, win: (be[i], 0, f)),
                  pl.BlockSpec((1, d, MOE_TF), lambda f, i, be, nu, win: (be[i], 0, f))],
        out_specs=pl.BlockSpec((MOE_TM, MOE_TF), lambda f, i, be, nu, win: (i, f)),
    )
    return pl.pallas_call(
        _moe_up_body,
        grid_spec=grid_spec,
        out_shape=jax.ShapeDtypeStruct((nblk * MOE_TM, D_EXPERT), BF16),
        compiler_params=_cparams(("arbitrary", "arbitrary")),
        name="moe_up",
    )(be, nused, win, xs, w_gate, w_up)


def _moe_down_body(be_ref, nused_ref, h_ref, wd_ref, rw_ref, o_ref):
    blk = pl.program_id(1)

    @pl.when(blk < nused_ref[0])
    def _():
        y = jnp.dot(h_ref[...], wd_ref[0].astype(BF16), preferred_element_type=F32)
        o_ref[...] = (y * rw_ref[...]).astype(o_ref.dtype)

    @pl.when(blk >= nused_ref[0])
    def _():
        o_ref[...] = jnp.zeros(o_ref.shape, o_ref.dtype)


def _moe_down(be, nused, hact, w_down, row_w):
    p, f = hact.shape
    d = w_down.shape[2]
    nblk = p // MOE_TM
    grid_spec = pltpu.PrefetchScalarGridSpec(
        num_scalar_prefetch=2,
        grid=(d // MOE_TN, nblk),
        in_specs=[pl.BlockSpec((MOE_TM, f), lambda n, i, be, nu: (i, 0)),
                  pl.BlockSpec((1, f, MOE_TN), lambda n, i, be, nu: (be[i], 0, n)),
                  pl.BlockSpec((MOE_TM, 1), lambda n, i, be, nu: (i, 0))],
        out_specs=pl.BlockSpec((MOE_TM, MOE_TN), lambda n, i, be, nu: (i, n)),
    )
    return pl.pallas_call(
        _moe_down_body,
        grid_spec=grid_spec,
        out_shape=jax.ShapeDtypeStruct((p, d), BF16),
        compiler_params=_cparams(("arbitrary", "arbitrary")),
        name="moe_down",
    )(be, nused, hact, w_down, row_w)


def _route(logits):
    t = logits.shape[0]
    g_logits = logits[:, :N_GROUPS]
    g_sel = jnp.argmax(g_logits, axis=-1)
    g_w = jnp.max(jax.nn.softmax(g_logits, axis=-1), axis=-1)
    e_logits = logits[:, N_GROUPS:N_GROUPS + N_EXPERTS].reshape(t, N_GROUPS, EXPERTS_PER_GROUP)
    e_in = jnp.take_along_axis(e_logits, g_sel[:, None, None], axis=1)[:, 0]
    e_val, e_idx = lax.top_k(e_in, TOP_K)
    e_w = jax.nn.softmax(e_val, axis=-1) * g_w[:, None]
    expert_id = (g_sel[:, None] * EXPERTS_PER_GROUP + e_idx).reshape(-1).astype(jnp.int32)
    a = t * TOP_K
    e_sorted, order = lax.sort((expert_id, jnp.arange(a, dtype=jnp.int32)), num_keys=1, is_stable=True)
    experts = jnp.arange(N_EXPERTS, dtype=jnp.int32)
    start = jnp.searchsorted(e_sorted, experts, side='left').astype(jnp.int32)
    counts = jnp.searchsorted(e_sorted, experts, side='right').astype(jnp.int32) - start
    padded = (counts + MOE_TM - 1) // MOE_TM * MOE_TM
    pad_end = jnp.cumsum(padded)
    pad_start = pad_end - padded
    shift = pad_start - start
    jumps = shift - jnp.concatenate([jnp.zeros((1,), jnp.int32), shift[:-1]])
    marks = jnp.zeros((a + 1,), jnp.int32).at[start].add(jumps)
    dest = jnp.arange(a, dtype=jnp.int32) + jnp.cumsum(marks)[:a]
    pos = lax.sort((order, dest), num_keys=1)[1].reshape(t, TOP_K)
    nblk = a // MOE_TM + N_EXPERTS
    nused = (pad_end[-1] // MOE_TM).astype(jnp.int32)
    blk = jnp.arange(nblk, dtype=jnp.int32)
    owner = lambda ends, offs: jnp.minimum(jnp.sum((ends[None, :] <= offs[:, None]).astype(jnp.int32), axis=1),
                                           N_EXPERTS - 1)
    be = owner(pad_end, blk * MOE_TM)
    be = jnp.where(blk < nused, be, be[jnp.maximum(nused - 1, 0)])
    per_row = lambda v: jnp.broadcast_to(v[be][:, None], (nblk, MOE_TM)).reshape(-1)
    rank = jnp.arange(nblk * MOE_TM, dtype=jnp.int32) - per_row(pad_start)
    valid = rank < per_row(counts)
    src = jnp.clip(per_row(start) + rank, 0, a - 1)
    row_w = jnp.where(valid, e_w.reshape(-1)[order[src]], 0.0)
    g = MOE_GATHER_ALIGN
    tight = (counts + g - 1) // g * g
    tight_end = jnp.cumsum(tight)
    tight_start = tight_end - tight
    n_mini = MOE_GATHER_ROWS // g
    mini_e = owner(tight_end, jnp.arange(n_mini, dtype=jnp.int32) * g)
    per_mini = lambda v: jnp.broadcast_to(v[mini_e][:, None], (n_mini, g)).reshape(-1)
    grank = jnp.arange(MOE_GATHER_ROWS, dtype=jnp.int32) - per_mini(tight_start)
    gvalid = (grank >= 0) & (grank < per_mini(counts))
    gather_token = jnp.where(gvalid, order[jnp.clip(per_mini(start) + grank, 0, a - 1)] // TOP_K, 0)
    win = (tight_start[be] + blk * MOE_TM - pad_start[be]) // g
    win = jnp.where(blk < nused, win, 0).astype(jnp.int32)
    return gather_token, win, row_w, pos, be, nused.reshape(1)


def _rope_tables():
    half = QK_ROPE // 2
    inv_freq = ROPE_THETA ** (-jnp.arange(half, dtype=F32) / half)
    pos = jnp.concatenate([jnp.tile(jnp.arange(SEQ), BATCH),
                           jnp.tile(PAST_LEN + jnp.arange(DEC_SEQ), DEC_BATCH)]).astype(F32)
    ang = pos[:, None] * inv_freq[None, :]
    cos, sin = jnp.cos(ang), jnp.sin(ang)
    cos2 = jnp.concatenate([cos, cos], axis=1)
    sin2 = jnp.concatenate([-sin, sin], axis=1)
    return cos2, sin2


def kernel(x_prompt, x_sample, cache_ckv, cache_krope, state_conv, state_ssm, norm1_w, w_in, q_norm_w, kv_norm_w, w_uq, w_ukv, conv_w, conv_b, dt_bias, a_log, d_skip, ssd_norm_w, w_mla_o, w_ssd_o, w_out, norm2_w, w_group, b_group, w_erouter, b_erouter, w_gate, w_up, w_down, final_norm_w):
    swap = np.concatenate([np.arange(QK_ROPE // 2, QK_ROPE), np.arange(QK_ROPE // 2)])
    x_p2 = x_prompt.reshape(T_PROMPT, D_MODEL)
    x_s2 = x_sample.reshape(T_SAMPLE, D_MODEL)
    cos2, sin2 = _rope_tables()
    cos128, sin128 = jnp.tile(cos2, (1, 2)), jnp.tile(sin2, (1, 2))

    wi = w_in[0]
    w_kr = wi[:, OFF_KR:OFF_KR + QK_ROPE]
    w_small = jnp.concatenate([wi[:, OFF_CQ:OFF_KR], wi[:, OFF_DT:OFF_DT + SSD_HEADS], w_kr, w_kr[:, swap]],
                              axis=1).astype(BF16)
    w_big = wi[:, OFF_Z:OFF_DT].astype(BF16)
    w_gates = wi[:, OFF_GMLA:].astype(BF16)
    wq = w_uq[0].reshape(Q_LORA, MLA_HEADS, QK_NOPE + QK_ROPE)
    wq_nope = wq[:, :, :QK_NOPE].reshape(Q_LORA, MLA_HEADS * QK_NOPE).astype(BF16)
    wq_rope = wq[:, :, QK_NOPE:]
    wq_r = wq_rope.reshape(Q_LORA, MLA_HEADS * QK_ROPE).astype(BF16)
    wq_rs = wq_rope[:, :, swap].reshape(Q_LORA, MLA_HEADS * QK_ROPE).astype(BF16)
    wkv3 = w_ukv[0].reshape(KV_LORA, MLA_HEADS, QK_NOPE + V_HEAD)
    w_uk_all = wkv3[:, :, :QK_NOPE].reshape(KV_LORA, MLA_HEADS * QK_NOPE).astype(BF16)
    w_uv_all_t = jnp.transpose(wkv3[:, :, QK_NOPE:], (1, 2, 0)).reshape(MLA_HEADS * V_HEAD, KV_LORA).astype(BF16)
    w_uk_t = jnp.transpose(wkv3[:, :, :QK_NOPE], (1, 2, 0)).astype(BF16)
    w_uv_h = jnp.transpose(wkv3[:, :, QK_NOPE:], (1, 0, 2)).astype(BF16)

    u = _rms_rows(x_p2, x_s2, norm1_w[0], BF16)
    small = _mm(u, w_small, tm=MM_TM, tn=SM_N // 2, out_dtype=F32, name="proj_small")
    big = _mm(u, w_big, tm=MM_TM, tn=512, out_dtype=F32, name="proj_big")
    gates = _mm(u, w_gates, tm=MM_TM, tn=1024, out_dtype=BF16, name="proj_gates")

    cqn, ckv, ckv_b, krope, krope_b = _mla_prep(small, q_norm_w[0], kv_norm_w[0], cos2, sin2)
    qn = _mm(cqn, wq_nope, tm=MM_TM, tn=1024, out_dtype=BF16, epilogue=lambda acc: acc * QSCALE, name="q_nope")
    qr = _q_rope(cqn, wq_r, wq_rs, cos128, sin128, tm=MM_TM, tn=1024)
    kn = _mm(ckv_b, w_uk_all, tm=MM_TM, tn=1024, out_dtype=BF16, name="k_up")
    vt = _mm_t(w_uv_all_t, ckv_b, rows=T_PROMPT, tm=ATTN_TK, tn=1024, out_dtype=BF16, name="v_up_t")

    qn_s = qn[T_PROMPT:].reshape(T_SAMPLE, MLA_HEADS, QK_NOPE).transpose(1, 0, 2)
    q_lat = _bmm(qn_s, w_uk_t, BF16, "q_absorb")
    q_lat = q_lat.reshape(MLA_HEADS, DEC_BATCH, DEC_SEQ, KV_LORA).transpose(1, 0, 2, 3)
    q_lat = q_lat.reshape(DEC_BATCH, MLA_HEADS * DEC_SEQ, KV_LORA)
    qr_s = qr[T_PROMPT:].reshape(DEC_BATCH, DEC_SEQ, MLA_HEADS, QK_ROPE).transpose(0, 2, 1, 3)
    qr_s = qr_s.reshape(DEC_BATCH, MLA_HEADS * DEC_SEQ, QK_ROPE)
    o_lat = _attn_sample(q_lat, qr_s, cache_ckv[0], cache_krope[0], ckv, krope)
    o_lat = o_lat.reshape(DEC_BATCH, MLA_HEADS, DEC_SEQ, KV_LORA).transpose(1, 0, 2, 3)
    o_lat = o_lat.reshape(MLA_HEADS, T_SAMPLE, KV_LORA)
    o_mla_s = _bmm(o_lat, w_uv_h, BF16, "v_absorb")
    o_mla_s = o_mla_s.transpose(1, 0, 2).reshape(T_SAMPLE, MLA_HEADS * V_HEAD)
    o_mla = _attn_prompt(qn, qr, kn, krope_b, vt, jnp.pad(o_mla_s, ((T_PROMPT, 0), (0, 0))),
                         nb=BATCH, seq=SEQ, heads=MLA_HEADS, tq=ATTN_TQ)

    prev_p = jnp.zeros((BATCH, 8, SSD_CONV_DIM), F32)
    prev_s = jnp.concatenate([jnp.zeros((DEC_BATCH, 8 - (SSD_CONV - 1), SSD_CONV_DIM), F32), state_conv[0]], axis=1)
    xbc_p = _conv_silu(big, prev_p, conv_w[0], conv_b[0], nb=BATCH, seq=SEQ, tl=512, row_off=0)
    xbc_s = _conv_silu(big, prev_s, conv_w[0], conv_b[0], nb=DEC_BATCH, seq=DEC_SEQ, tl=DEC_SEQ, row_off=T_PROMPT)
    dt_raw = small[:, SM_DT:SM_DT + SSD_HEADS]
    by_group = lambda d: d.reshape(d.shape[0], SSD_GROUPS, SSD_RANK).transpose(1, 0, 2)

    def pad_seq(a, fill):
        a = a.reshape(DEC_BATCH, DEC_SEQ, a.shape[-1])
        a = jnp.pad(a, ((0, 0), (0, CHUNK - DEC_SEQ), (0, 0)), constant_values=fill)
        return a.reshape(DEC_BATCH * CHUNK, a.shape[-1])

    h0_s = state_ssm[0].reshape(DEC_BATCH, SSD_HEADS * SSD_HEADDIM, SSD_STATE)
    y_s, ssm_s = _ssd(pad_seq(xbc_s, 0.0), pad_seq(big[T_PROMPT:, BIG_Z:BIG_Z + SSD_INNER], 0.0),
                      by_group(pad_seq(dt_raw[T_PROMPT:], -jnp.inf)), h0_s, dt_bias[0], a_log[0], d_skip[0],
                      ssd_norm_w[0], nb=DEC_BATCH, seq=CHUNK, sub=1)
    y_s = y_s.reshape(DEC_BATCH, CHUNK, SSD_INNER)[:, :DEC_SEQ].reshape(T_SAMPLE, SSD_INNER)
    o_ssd, ssm_p = _ssd(xbc_p, big, by_group(dt_raw[:T_PROMPT]), None, dt_bias[0], a_log[0], d_skip[0],
                        ssd_norm_w[0], nb=BATCH, seq=SEQ, sub=SSD_SUB, base=jnp.pad(y_s, ((T_PROMPT, 0), (0, 0))))

    gate = lambda acc, g: _sigmoid(g.astype(F32)) * acc
    m1 = _mm(o_mla, w_mla_o[0].astype(BF16), tm=640, tn=1024, out_dtype=F32,
             extras=[(gates, GATE_MLA // 1024)], epilogue=gate, name="mla_out")
    merged = _mm(o_ssd, w_ssd_o[0].astype(BF16), tm=640, tn=512, out_dtype=BF16,
                 extras=[(gates, GATE_SSD // 512), (m1, 0)],
                 epilogue=lambda acc, g, m: m + _sigmoid(g.astype(F32)) * acc, name="ssd_out")
    mix = _mm(merged, w_out[0].astype(BF16), tm=640, tn=1024, out_dtype=F32, name="out_proj")

    n_r = 128
    w_router = jnp.concatenate([w_group[0], w_erouter[0],
                                jnp.zeros((D_MODEL, n_r - N_GROUPS - N_EXPERTS), F32)], axis=1)
    b_router = jnp.concatenate([b_group[0], b_erouter[0], jnp.zeros((n_r - N_GROUPS - N_EXPERTS,), F32)])
    h, xn, logits = _norm_router(x_p2, x_s2, mix, norm2_w[0], w_router, b_router.reshape(1, n_r))
    gather_token, win, row_w, pos, be, nused = _route(logits)
    xs = jnp.take(xn, gather_token, axis=0, mode="clip")
    hact = _moe_up(be, nused, win, xs, w_gate[0], w_up[0])
    yb = _moe_down(be, nused, hact, w_down[0], row_w.reshape(-1, 1))
    y_p2, y_s2 = _final_norm(h, jnp.take(yb, pos[:, 0], axis=0, mode="clip"),
                             jnp.take(yb, pos[:, 1], axis=0, mode="clip"), final_norm_w)

    y_prompt = y_p2.reshape(BATCH, SEQ, D_MODEL)
    y_sample = y_s2.reshape(DEC_BATCH, DEC_SEQ, D_MODEL)
    ckv_p = ckv[:T_PROMPT].reshape(1, BATCH, SEQ, KV_LORA)
    ckv_s = ckv[T_PROMPT:].reshape(1, DEC_BATCH, DEC_SEQ, KV_LORA)
    kr_p = krope[:T_PROMPT].reshape(1, BATCH, SEQ, QK_ROPE)
    kr_s = krope[T_PROMPT:].reshape(1, DEC_BATCH, DEC_SEQ, QK_ROPE)
    tail = SSD_CONV - 1
    conv_p = jnp.stack([lax.slice(big, ((b + 1) * SEQ - tail, BIG_XBC), ((b + 1) * SEQ, BIG_XBC + SSD_CONV_DIM))
                        for b in range(BATCH)])[None]
    conv_s = jnp.stack([lax.slice(big, (T_PROMPT + (b + 1) * DEC_SEQ - tail, BIG_XBC),
                                  (T_PROMPT + (b + 1) * DEC_SEQ, BIG_XBC + SSD_CONV_DIM))
                        for b in range(DEC_BATCH)])[None]
    ssm_p = ssm_p.reshape(1, BATCH, SSD_HEADS, SSD_HEADDIM, SSD_STATE)
    ssm_s = ssm_s.reshape(1, DEC_BATCH, SSD_HEADS, SSD_HEADDIM, SSD_STATE)
    return (y_prompt, y_sample, ckv_p, kr_p, conv_p, ssm_p, ckv_s, kr_s, conv_s, ssm_s)
```

```python
import functools
import math

import numpy as np
import jax
import jax.numpy as jnp
from jax import lax
from jax.experimental import pallas as pl
from jax.experimental.pallas import tpu as pltpu

F32 = jnp.float32
BF16 = jnp.bfloat16
HI = lax.Precision.HIGHEST

D_MODEL = 4096
BATCH = 2
SEQ = 8192
DEC_BATCH = 8
DEC_SEQ = 32
PAST_LEN = 4096
CHUNK = 64
EPS = 1e-6
MLA_HEADS = 32
Q_LORA = 1024
KV_LORA = 512
QK_NOPE = 128
QK_ROPE = 64
V_HEAD = 128
ROPE_THETA = 10000.0
SCALE = (QK_NOPE + QK_ROPE) ** -0.5
QSCALE = SCALE * math.log2(math.e)
SSD_INNER = 2 * D_MODEL
SSD_HEADDIM = 64
SSD_HEADS = SSD_INNER // SSD_HEADDIM
SSD_STATE = 128
SSD_GROUPS = 8
SSD_RANK = SSD_HEADS // SSD_GROUPS
SSD_GCOLS = SSD_RANK * SSD_HEADDIM
SSD_CONV = 4
SSD_CONV_DIM = SSD_INNER + 2 * SSD_GROUPS * SSD_STATE
N_GROUPS = 8
EXPERTS_PER_GROUP = 8
N_EXPERTS = N_GROUPS * EXPERTS_PER_GROUP
TOP_K = 2
D_EXPERT = 1024

T_PROMPT = BATCH * SEQ
T_SAMPLE = DEC_BATCH * DEC_SEQ
T_ALL = T_PROMPT + T_SAMPLE

OFF_CQ = 0
OFF_CKV = OFF_CQ + Q_LORA
OFF_KR = OFF_CKV + KV_LORA
OFF_Z = OFF_KR + QK_ROPE
OFF_XBC = OFF_Z + SSD_INNER
OFF_DT = OFF_XBC + SSD_CONV_DIM
OFF_GMLA = OFF_DT + SSD_HEADS
OFF_GSSD = OFF_GMLA + D_MODEL
N_IN = OFF_GSSD + D_MODEL

BIG_Z = 0
BIG_XBC = SSD_INNER
BIG_N = BIG_XBC + SSD_CONV_DIM
GATE_MLA = 0
GATE_SSD = D_MODEL
SM_CQ = 0
SM_CKV = Q_LORA
SM_DT = SM_CKV + KV_LORA
SM_KR = SM_DT + SSD_HEADS
SM_KRS = SM_KR + QK_ROPE
SM_N = SM_KRS + QK_ROPE

V7X_VMEM_LIMIT = 58 * 1024 * 1024
ROW_TILE = 256
MM_TM = 1280
MOE_TM = 512
SSD_SUB = 8
MOE_GATHER_ALIGN = 16
MOE_GATHER_ROWS = T_ALL * TOP_K + N_EXPERTS * MOE_GATHER_ALIGN + MOE_TM
MOE_TF = 512
MOE_TN = 4096
ATTN_TQ = 2048
ATTN_TK = 512


def _cparams(sem):
    return pltpu.CompilerParams(dimension_semantics=sem, vmem_limit_bytes=V7X_VMEM_LIMIT)


def _dot_nt(a, b):
    return lax.dot_general(a, b, (((1,), (1,)), ((), ())), preferred_element_type=F32)


def _rms(x, w):
    return x * lax.rsqrt(jnp.mean(x * x, axis=-1, keepdims=True) + EPS) * w


def _sigmoid(x):
    return 0.5 * jnp.tanh(0.5 * x) + 0.5


def _stacked_x_specs():
    n_p = T_PROMPT // ROW_TILE
    return [pl.BlockSpec((ROW_TILE, D_MODEL), lambda i: (jnp.minimum(i, n_p - 1), 0)),
            pl.BlockSpec((ROW_TILE, D_MODEL), lambda i: (0, 0))]


def _stacked_x(xp_ref, xs_ref):
    is_prompt = pl.program_id(0) < T_PROMPT // ROW_TILE
    return jnp.where(is_prompt, xp_ref[...], xs_ref[...])


def _rms_rows_body(xp_ref, xs_ref, w_ref, o_ref):
    o_ref[...] = _rms(_stacked_x(xp_ref, xs_ref), w_ref[...]).astype(o_ref.dtype)


def _rms_rows(x_p, x_s, w, out_dtype):
    d = D_MODEL
    return pl.pallas_call(
        _rms_rows_body,
        grid=(T_ALL // ROW_TILE,),
        in_specs=_stacked_x_specs() + [pl.BlockSpec((1, d), lambda i: (0, 0))],
        out_specs=pl.BlockSpec((ROW_TILE, d), lambda i: (i, 0)),
        out_shape=jax.ShapeDtypeStruct((T_ALL, d), out_dtype),
        compiler_params=_cparams(("parallel",)),
        name="rms_rows",
    )(x_p, x_s, w.reshape(1, d))


def _final_body(h_ref, ya_ref, yb_ref, w_ref, op_ref, os_ref):
    y = ya_ref[...].astype(F32) + yb_ref[...].astype(F32)
    out = _rms(h_ref[...] + y, w_ref[...])
    is_prompt = pl.program_id(0) < T_PROMPT // ROW_TILE

    @pl.when(is_prompt)
    def _():
        op_ref[...] = out

    @pl.when(jnp.logical_not(is_prompt))
    def _():
        os_ref[...] = out


def _final_norm(h, y_a, y_b, w):
    m, d = h.shape
    n_p = T_PROMPT // ROW_TILE
    row = pl.BlockSpec((ROW_TILE, d), lambda i: (i, 0))
    return pl.pallas_call(
        _final_body,
        grid=(m // ROW_TILE,),
        in_specs=[row, row, row, pl.BlockSpec((1, d), lambda i: (0, 0))],
        out_specs=[pl.BlockSpec((ROW_TILE, d), lambda i: (jnp.minimum(i, n_p - 1), 0)),
                   pl.BlockSpec((ROW_TILE, d), lambda i: (0, 0))],
        out_shape=[jax.ShapeDtypeStruct((T_PROMPT, d), F32), jax.ShapeDtypeStruct((T_SAMPLE, d), F32)],
        compiler_params=_cparams(("arbitrary",)),
        name="final_norm",
    )(h, y_a, y_b, w.reshape(1, d))


def _mm_body(epilogue, a_ref, w_ref, *rest):
    o_ref = rest[-1]
    acc = jnp.dot(a_ref[...], w_ref[...], preferred_element_type=F32)
    if epilogue is not None:
        acc = epilogue(acc, *[r[...] for r in rest[:-1]])
    o_ref[...] = acc.astype(o_ref.dtype)


def _mm(a, w, *, tm, tn, out_dtype, extras=(), epilogue=None, name="mm"):
    m, k = a.shape
    n = w.shape[1]
    in_specs = [pl.BlockSpec((tm, k), lambda i, j: (i, 0)),
                pl.BlockSpec((k, tn), lambda i, j: (0, j))]
    args = [a, w]
    for arr, off in extras:
        in_specs.append(pl.BlockSpec((tm, tn), functools.partial(lambda i, j, off: (i, j + off), off=off)))
        args.append(arr)
    return pl.pallas_call(
        functools.partial(_mm_body, epilogue),
        grid=(m // tm, n // tn),
        in_specs=in_specs,
        out_specs=pl.BlockSpec((tm, tn), lambda i, j: (i, j)),
        out_shape=jax.ShapeDtypeStruct((m, n), out_dtype),
        compiler_params=_cparams(("parallel", "parallel")),
        name=name,
    )(*args)


def _mla_prep_body(s_ref, qw_ref, kvw_ref, cos_ref, sin_ref, cqn_ref, ckv_ref, ckvb_ref, kr_ref, krb_ref):
    cqn_ref[...] = _rms(s_ref[:, SM_CQ:SM_CQ + Q_LORA], qw_ref[...]).astype(BF16)
    c = _rms(s_ref[:, SM_CKV:SM_CKV + KV_LORA], kvw_ref[...])
    ckv_ref[...] = c
    ckvb_ref[...] = c.astype(BF16)
    kr = (s_ref[:, SM_KR:SM_KR + QK_ROPE] * cos_ref[...]
          + s_ref[:, SM_KRS:SM_KRS + QK_ROPE] * sin_ref[...])
    kr_ref[...] = kr
    krb_ref[...] = kr.astype(BF16)


def _mla_prep(small, q_norm_w, kv_norm_w, cos2, sin2):
    m = small.shape[0]
    row = lambda n: pl.BlockSpec((ROW_TILE, n), lambda i: (i, 0))
    vec = lambda n: pl.BlockSpec((1, n), lambda i: (0, 0))
    return pl.pallas_call(
        _mla_prep_body,
        grid=(m // ROW_TILE,),
        in_specs=[row(SM_N), vec(Q_LORA), vec(KV_LORA), row(QK_ROPE), row(QK_ROPE)],
        out_specs=[row(Q_LORA), row(KV_LORA), row(KV_LORA), row(QK_ROPE), row(QK_ROPE)],
        out_shape=[jax.ShapeDtypeStruct((m, Q_LORA), BF16),
                   jax.ShapeDtypeStruct((m, KV_LORA), F32),
                   jax.ShapeDtypeStruct((m, KV_LORA), BF16),
                   jax.ShapeDtypeStruct((m, QK_ROPE), F32),
                   jax.ShapeDtypeStruct((m, QK_ROPE), BF16)],
        compiler_params=_cparams(("parallel",)),
        name="mla_prep",
    )(small, q_norm_w.reshape(1, -1), kv_norm_w.reshape(1, -1), cos2, sin2)


def _q_rope_body(a_ref, w_ref, ws_ref, cos_ref, sin_ref, o_ref):
    reps = o_ref.shape[1] // cos_ref.shape[1]
    a = a_ref[...]
    r = jnp.dot(a, w_ref[...], preferred_element_type=F32)
    rs = jnp.dot(a, ws_ref[...], preferred_element_type=F32)
    cos = jnp.tile(cos_ref[...], (1, reps))
    sin = jnp.tile(sin_ref[...], (1, reps))
    o_ref[...] = ((r * cos + rs * sin) * QSCALE).astype(o_ref.dtype)


def _q_rope(cqn, w_r, w_rs, cos128, sin128, *, tm, tn):
    m, k = cqn.shape
    n = w_r.shape[1]
    return pl.pallas_call(
        _q_rope_body,
        grid=(m // tm, n // tn),
        in_specs=[pl.BlockSpec((tm, k), lambda i, j: (i, 0)),
                  pl.BlockSpec((k, tn), lambda i, j: (0, j)),
                  pl.BlockSpec((k, tn), lambda i, j: (0, j)),
                  pl.BlockSpec((tm, 128), lambda i, j: (i, 0)),
                  pl.BlockSpec((tm, 128), lambda i, j: (i, 0))],
        out_specs=pl.BlockSpec((tm, tn), lambda i, j: (i, j)),
        out_shape=jax.ShapeDtypeStruct((m, n), BF16),
        compiler_params=_cparams(("parallel", "parallel")),
        name="q_rope",
    )(cqn, w_r, w_rs, cos128, sin128)


def _mm_t_body(w_ref, a_ref, o_ref):
    o_ref[0] = _dot_nt(w_ref[...], a_ref[...]).astype(o_ref.dtype)


def _mm_t(w_t, a, *, rows, tm, tn, out_dtype, name):
    n, k = w_t.shape
    return pl.pallas_call(
        _mm_t_body,
        grid=(rows // tm, n // tn),
        in_specs=[pl.BlockSpec((tn, k), lambda i, j: (j, 0)),
                  pl.BlockSpec((tm, k), lambda i, j: (i, 0))],
        out_specs=pl.BlockSpec((1, tn, tm), lambda i, j: (i, j, 0)),
        out_shape=jax.ShapeDtypeStruct((rows // tm, n, tm), out_dtype),
        compiler_params=_cparams(("parallel", "parallel")),
        name=name,
    )(w_t, a)


ATTN_KP = 256


ATTN_VP = 144


def _attn_prompt_body(qn_ref, qr_ref, kn_ref, kr_ref, vt_ref, base_ref, o_ref, kf_sc, vf_sc, qf_sc, acc_sc):
    qi = pl.program_id(2)
    tq = qn_ref.shape[0]
    tk = vt_ref.shape[2]
    seq = kn_ref.shape[0]
    pad = ATTN_KP - QK_NOPE - QK_ROPE

    @pl.when(qi == 0)
    def _():
        def fill(i, c):
            r = pl.ds(pl.multiple_of(i * tk, tk), tk)
            kr = kr_ref[r, :]
            for hh in range(2):
                c0 = hh * ATTN_KP
                kf_sc[r, c0:c0 + QK_NOPE] = kn_ref[r, hh * QK_NOPE:(hh + 1) * QK_NOPE]
                kf_sc[r, c0 + QK_NOPE:c0 + QK_NOPE + QK_ROPE] = kr
                kf_sc[r, c0 + QK_NOPE + QK_ROPE:c0 + ATTN_KP] = jnp.zeros((tk, pad), BF16)
                vf_sc[i, hh * ATTN_VP:hh * ATTN_VP + V_HEAD, :] = vt_ref[i, hh * V_HEAD:(hh + 1) * V_HEAD, :]
                vf_sc[i, hh * ATTN_VP + V_HEAD:(hh + 1) * ATTN_VP, :] = jnp.ones((ATTN_VP - V_HEAD, tk), BF16)
            return c
        lax.fori_loop(0, seq // tk, fill, 0)

    for hh in range(2):
        c0 = hh * ATTN_KP
        qf_sc[:, c0:c0 + QK_NOPE] = qn_ref[:, hh * QK_NOPE:(hh + 1) * QK_NOPE]
        qf_sc[:, c0 + QK_NOPE:c0 + QK_NOPE + QK_ROPE] = qr_ref[:, hh * QK_ROPE:(hh + 1) * QK_ROPE]
        qf_sc[:, c0 + QK_NOPE + QK_ROPE:c0 + ATTN_KP] = jnp.zeros((tq, pad), BF16)
    acc_sc[...] = jnp.zeros(acc_sc.shape, F32)

    def block(kb, carry, diag):
        r = pl.ds(pl.multiple_of(kb * tk, tk), tk)
        q0 = 0 if diag is None else diag * tk
        nq_cols = tq - q0
        out = []
        for hh in range(2):
            m = carry[hh][:, q0:]
            st = _dot_nt(kf_sc[r, hh * ATTN_KP:(hh + 1) * ATTN_KP], qf_sc[q0:, hh * ATTN_KP:(hh + 1) * ATTN_KP])
            if diag is not None:
                kc = lax.broadcasted_iota(jnp.int32, (tk, nq_cols), 0) // CHUNK
                qc = lax.broadcasted_iota(jnp.int32, (tk, nq_cols), 1) // CHUNK
                st = jnp.where(kc <= qc, st, -jnp.inf)
            m_new = jnp.maximum(m, jnp.max(st, axis=0, keepdims=True))
            alpha = jnp.exp2(m - m_new)
            pt = jnp.exp2(st - m_new).astype(BF16)
            vt = vf_sc[kb, hh * ATTN_VP:(hh + 1) * ATTN_VP, :]
            acc_sc[hh, :, q0:] = alpha * acc_sc[hh, :, q0:] + jnp.dot(vt, pt, preferred_element_type=F32)
            out.append(m_new if q0 == 0 else jnp.concatenate([carry[hh][:, :q0], m_new], axis=1))
        return tuple(out)

    n_diag = tq // tk
    n_full = qi * n_diag
    init = (jnp.full((1, tq), -jnp.inf, F32),) * 2
    carry = lax.fori_loop(0, n_full, lambda kb, c: block(kb, c, None), init)
    for d in range(n_diag):
        carry = block(n_full + d, carry, d)
    for hh in range(2):
        o = acc_sc[hh, 0:V_HEAD, :] / acc_sc[hh, V_HEAD:V_HEAD + 1, :]
        o_ref[:, hh * V_HEAD:(hh + 1) * V_HEAD] = o.T.astype(o_ref.dtype)


def _attn_prompt(qn, qr, kn, krb, vt, base, *, nb, seq, heads, tq):
    tk = vt.shape[2]
    nq = seq // tq
    nk = seq // tk
    return pl.pallas_call(
        _attn_prompt_body,
        grid=(nb, heads // 2, nq),
        in_specs=[pl.BlockSpec((tq, 2 * QK_NOPE), lambda b, hp, qi: (b * nq + qi, hp)),
                  pl.BlockSpec((tq, 2 * QK_ROPE), lambda b, hp, qi: (b * nq + qi, hp)),
                  pl.BlockSpec((seq, 2 * QK_NOPE), lambda b, hp, qi: (b, hp)),
                  pl.BlockSpec((seq, QK_ROPE), lambda b, hp, qi: (b, 0)),
                  pl.BlockSpec((nk, 2 * V_HEAD, tk), lambda b, hp, qi: (b, hp, 0)),
                  pl.BlockSpec(memory_space=pl.ANY)],
        out_specs=pl.BlockSpec((tq, 2 * V_HEAD), lambda b, hp, qi: (b * nq + qi, hp)),
        out_shape=jax.ShapeDtypeStruct(base.shape, base.dtype),
        input_output_aliases={5: 0},
        scratch_shapes=[pltpu.VMEM((seq, 2 * ATTN_KP), BF16), pltpu.VMEM((nk, 2 * ATTN_VP, tk), BF16),
                        pltpu.VMEM((tq, 2 * ATTN_KP), BF16), pltpu.VMEM((2, ATTN_VP, tq), F32)],
        compiler_params=_cparams(("parallel", "parallel", "arbitrary")),
        name="attn_prompt",
    )(qn, qr, kn, krb, vt, base)


def _bmm_body(a_ref, w_ref, o_ref):
    o_ref[0] = jnp.dot(a_ref[0], w_ref[0], preferred_element_type=F32).astype(o_ref.dtype)


def _bmm(a, w, out_dtype, name):
    h, m, k = a.shape
    n = w.shape[2]
    return pl.pallas_call(
        _bmm_body,
        grid=(h,),
        in_specs=[pl.BlockSpec((1, m, k), lambda i: (i, 0, 0)),
                  pl.BlockSpec((1, k, n), lambda i: (i, 0, 0))],
        out_specs=pl.BlockSpec((1, m, n), lambda i: (i, 0, 0)),
        out_shape=jax.ShapeDtypeStruct((h, m, n), out_dtype),
        compiler_params=_cparams(("parallel",)),
        name=name,
    )(a, w)


def _attn_sample_body(ql_ref, qr_ref, cc_ref, ck_ref, nc_ref, nk_ref, o_ref, ccb_sc, ckb_sc):
    @pl.when(pl.program_id(1) == 0)
    def _():
        ccb_sc[...] = cc_ref[0].astype(BF16)
        ckb_sc[...] = ck_ref[0].astype(BF16)

    ql = ql_ref[0]
    qr = qr_ref[0]
    ncb = nc_ref[...].astype(BF16)
    nkb = nk_ref[...].astype(BF16)
    s1 = _dot_nt(ql, ccb_sc[...]) + _dot_nt(qr, ckb_sc[...])
    s2 = _dot_nt(ql, ncb) + _dot_nt(qr, nkb)
    m = jnp.maximum(jnp.max(s1, axis=1, keepdims=True), jnp.max(s2, axis=1, keepdims=True))
    p1 = jnp.exp2(s1 - m)
    p2 = jnp.exp2(s2 - m)
    l = jnp.sum(p1, axis=1, keepdims=True) + jnp.sum(p2, axis=1, keepdims=True)
    o = (jnp.dot(p1.astype(BF16), ccb_sc[...], preferred_element_type=F32)
         + jnp.dot(p2.astype(BF16), ncb, preferred_element_type=F32))
    o_ref[0] = (o / l).astype(o_ref.dtype)


def _attn_sample(q_lat, q_rope, cache_ckv, cache_krope, ckv, krope):
    rows = q_lat.shape[1]
    tr = 256
    new_blk = T_PROMPT // DEC_SEQ
    return pl.pallas_call(
        _attn_sample_body,
        grid=(DEC_BATCH, rows // tr),
        in_specs=[pl.BlockSpec((1, tr, KV_LORA), lambda b, r: (b, r, 0)),
                  pl.BlockSpec((1, tr, QK_ROPE), lambda b, r: (b, r, 0)),
                  pl.BlockSpec((1, PAST_LEN, KV_LORA), lambda b, r: (b, 0, 0)),
                  pl.BlockSpec((1, PAST_LEN, QK_ROPE), lambda b, r: (b, 0, 0)),
                  pl.BlockSpec((DEC_SEQ, KV_LORA), lambda b, r: (new_blk + b, 0)),
                  pl.BlockSpec((DEC_SEQ, QK_ROPE), lambda b, r: (new_blk + b, 0))],
        out_specs=pl.BlockSpec((1, tr, KV_LORA), lambda b, r: (b, r, 0)),
        out_shape=jax.ShapeDtypeStruct((DEC_BATCH, rows, KV_LORA), BF16),
        scratch_shapes=[pltpu.VMEM((PAST_LEN, KV_LORA), BF16), pltpu.VMEM((PAST_LEN, QK_ROPE), BF16)],
        compiler_params=_cparams(("parallel", "arbitrary")),
        name="attn_sample",
    )(q_lat, q_rope, cache_ckv, cache_krope, ckv, krope)


def _conv_body(x_ref, prev_ref, w_ref, b_ref, o_ref, ext_sc):
    tl = x_ref.shape[0]

    @pl.when(pl.program_id(2) == 0)
    def _():
        ext_sc[0:8, :] = prev_ref[0]

    @pl.when(pl.program_id(2) != 0)
    def _():
        ext_sc[0:8, :] = ext_sc[tl:tl + 8, :]

    ext_sc[8:8 + tl, :] = x_ref[...]
    e = ext_sc[...]
    acc = b_ref[...] + e[8:8 + tl, :] * w_ref[SSD_CONV - 1:SSD_CONV, :]
    for k in range(SSD_CONV - 1):
        acc = acc + pltpu.roll(e, SSD_CONV - 1 - k, axis=0)[8:8 + tl, :] * w_ref[k:k + 1, :]
    o_ref[...] = (acc * _sigmoid(acc)).astype(o_ref.dtype)


def _conv_silu(big, prev8, conv_w, conv_b, *, nb, seq, tl, row_off):
    tc = 1024
    nrt = seq // tl
    rb0 = row_off // tl
    cb0 = BIG_XBC // tc
    return pl.pallas_call(
        _conv_body,
        grid=(nb, SSD_CONV_DIM // tc, nrt),
        in_specs=[pl.BlockSpec((tl, tc), lambda b, j, r: (rb0 + b * nrt + r, cb0 + j)),
                  pl.BlockSpec((1, 8, tc), lambda b, j, r: (b, 0, j)),
                  pl.BlockSpec((SSD_CONV, tc), lambda b, j, r: (0, j)),
                  pl.BlockSpec((1, tc), lambda b, j, r: (0, j))],
        out_specs=pl.BlockSpec((tl, tc), lambda b, j, r: (b * nrt + r, j)),
        out_shape=jax.ShapeDtypeStruct((nb * seq, SSD_CONV_DIM), BF16),
        scratch_shapes=[pltpu.VMEM((tl + 8, tc), F32)],
        compiler_params=_cparams(("parallel", "parallel", "arbitrary")),
        name="conv_silu",
    )(big, prev8, conv_w, conv_b.reshape(1, -1))


def _softplus(x):
    return jnp.maximum(x, 0.0) + jnp.log(1.0 + jnp.exp(-jnp.abs(x)))


def _dot_sel(sel, a, *, sel_left):
    hi = a.astype(BF16)
    r1 = a - hi.astype(F32)
    mid = r1.astype(BF16)
    lo = (r1 - mid.astype(F32)).astype(BF16)
    out = None
    for term in (hi, mid, lo):
        d = (jnp.dot(sel, term, preferred_element_type=F32) if sel_left
             else jnp.dot(term, sel, preferred_element_type=F32))
        out = d if out is None else out + d
    return out


def _ssd_body(has_h0, x_ref, b_ref, c_ref, z_ref, dt_ref, bias_ref, al_ref, dsk_ref, nw_ref, *rest):
    h0_ref = rest[0] if has_h0 else None
    y_ref, hout_ref, ht_sc = rest[-3:]
    R, P = SSD_RANK, SSD_HEADDIM
    lc = P
    W = 2 * P
    c_idx = pl.program_id(2)

    @pl.when(c_idx == 0)
    def _():
        if has_h0:
            ht_sc[...] = h0_ref[0].T
        else:
            ht_sc[...] = jnp.zeros(ht_sc.shape, F32)

    ri = lax.broadcasted_iota(jnp.int32, (lc, lc), 0)
    ci = lax.broadcasted_iota(jnp.int32, (lc, lc), 1)
    tril_b = jnp.where(ri >= ci, 1.0, 0.0).astype(BF16)
    er = lax.broadcasted_iota(jnp.int32, (R, R * P), 0)
    ec = lax.broadcasted_iota(jnp.int32, (R, R * P), 1) // P
    expand = jnp.where(er == ec, 1.0, 0.0).astype(BF16)
    li = lax.broadcasted_iota(jnp.int32, (lc, R * P), 0)
    si = lax.broadcasted_iota(jnp.int32, (lc, R * P), 1) % P
    first_head = lax.broadcasted_iota(jnp.int32, (lc, W), 1) < P
    neg_a = -jnp.exp(al_ref[0])

    ht = ht_sc[...]
    for k in range(x_ref.shape[0] // lc):
        rows = slice(k * lc, (k + 1) * lc)
        dt = _softplus(dt_ref[0, rows, :] + bias_ref[0])
        a_cs = _dot_sel(tril_b, dt * neg_a, sel_left=True)
        a_exp = _dot_sel(expand, a_cs, sel_left=False)
        dt_exp = _dot_sel(expand, dt, sel_left=False)
        a_last = a_exp[lc - 1:lc, :]
        a_key = jnp.sum(jnp.where(li == si, a_exp, 0.0), axis=0, keepdims=True)
        decay = jnp.exp(jnp.where(li >= si, a_exp - a_key, -jnp.inf))

        x = x_ref[rows, :].astype(F32)
        xdt = x * dt_exp
        bm = b_ref[rows, :].astype(BF16)
        cm = c_ref[rows, :].astype(BF16)
        cb2 = _dot_nt(cm, jnp.concatenate([bm, bm], axis=0))
        m_all = (jnp.tile(cb2, (1, R // 2)) * decay).astype(BF16)
        y_off = jnp.dot(cm, ht.astype(BF16), preferred_element_type=F32) * jnp.exp(a_exp)
        xdt_b = xdt.astype(BF16)
        y_diag = []
        for j in range(R // 2):
            xp = xdt_b[:, j * W:(j + 1) * W]
            zero = jnp.zeros_like(xp)
            stack = jnp.concatenate([jnp.where(first_head, xp, zero), jnp.where(first_head, zero, xp)], axis=0)
            y_diag.append(jnp.dot(m_all[:, j * W:(j + 1) * W], stack, preferred_element_type=F32))
        wgt = (xdt * jnp.exp(a_last - a_exp)).astype(BF16)
        states_t = lax.dot_general(bm, wgt, (((0,), (0,)), ((), ())), preferred_element_type=F32)
        ht = ht * jnp.exp(a_last) + states_t

        y = jnp.concatenate(y_diag, axis=1) + y_off + x * dsk_ref[...]
        z = z_ref[rows, :]
        y = y * (z * _sigmoid(z))
        y = y * lax.rsqrt(jnp.mean(y * y, axis=-1, keepdims=True) + EPS) * nw_ref[...]
        y_ref[rows, :] = y.astype(y_ref.dtype)
    ht_sc[...] = ht

    @pl.when(c_idx == pl.num_programs(2) - 1)
    def _():
        hout_ref[0] = ht.T


def _ssd(xbc, zsrc, dt_g, h0, dt_bias, a_log, d_skip, ssd_norm_w, *, nb, seq, sub, base=None):
    lc = sub * SSD_HEADDIM
    nc = seq // lc
    G, R, GC, N = SSD_GROUPS, SSD_RANK, SSD_GCOLS, SSD_STATE
    row = lambda b, g, c: b * nc + c
    in_specs = [
        pl.BlockSpec((lc, GC), lambda b, g, c: (row(b, g, c), g)),
        pl.BlockSpec((lc, N), lambda b, g, c: (row(b, g, c), SSD_INNER // N + g)),
        pl.BlockSpec((lc, N), lambda b, g, c: (row(b, g, c), SSD_INNER // N + G + g)),
        pl.BlockSpec((lc, GC), lambda b, g, c: (row(b, g, c), g)),
        pl.BlockSpec((1, lc, R), lambda b, g, c: (g, row(b, g, c), 0)),
        pl.BlockSpec((1, 1, R), lambda b, g, c: (g, 0, 0)),
        pl.BlockSpec((1, 1, R), lambda b, g, c: (g, 0, 0)),
        pl.BlockSpec((1, GC), lambda b, g, c: (0, g)),
        pl.BlockSpec((1, GC), lambda b, g, c: (0, g)),
    ]
    args = [xbc, xbc, xbc, zsrc, dt_g, dt_bias.reshape(G, 1, R), a_log.reshape(G, 1, R),
            jnp.repeat(d_skip, SSD_HEADDIM).reshape(1, SSD_INNER), ssd_norm_w.reshape(1, SSD_INNER)]
    if h0 is not None:
        in_specs.append(pl.BlockSpec((1, GC, N), lambda b, g, c: (b, g, 0)))
        args.append(h0)
    aliases = {}
    y_shape = jax.ShapeDtypeStruct((nb * seq, SSD_INNER), BF16)
    if base is not None:
        aliases = {len(args): 0}
        in_specs.append(pl.BlockSpec(memory_space=pl.ANY))
        args.append(base)
        y_shape = jax.ShapeDtypeStruct(base.shape, base.dtype)
    return pl.pallas_call(
        functools.partial(_ssd_body, h0 is not None),
        grid=(nb, G, nc),
        in_specs=in_specs,
        out_specs=[pl.BlockSpec((lc, GC), lambda b, g, c: (row(b, g, c), g)),
                   pl.BlockSpec((1, GC, N), lambda b, g, c: (b, g, 0))],
        out_shape=[y_shape, jax.ShapeDtypeStruct((nb, SSD_HEADS * SSD_HEADDIM, N), F32)],
        input_output_aliases=aliases,
        scratch_shapes=[pltpu.VMEM((N, GC), F32)],
        compiler_params=_cparams(("parallel", "parallel", "arbitrary")),
        name="ssd",
    )(*args)


def _norm_router_body(xp_ref, xs_ref, mix_ref, w_ref, wr_ref, br_ref, h_ref, xn_ref, lg_ref):
    h = _stacked_x(xp_ref, xs_ref) + mix_ref[...]
    h_ref[...] = h
    xn = _rms(h, w_ref[...])
    xn_ref[...] = xn.astype(BF16)
    lg_ref[...] = jnp.dot(xn, wr_ref[...], precision=HI, preferred_element_type=F32) + br_ref[...]


def _norm_router(x_p, x_s, mix, norm_w, w_router, b_router):
    m, d = mix.shape
    n = w_router.shape[1]
    row = pl.BlockSpec((ROW_TILE, d), lambda i: (i, 0))
    return pl.pallas_call(
        _norm_router_body,
        grid=(m // ROW_TILE,),
        in_specs=_stacked_x_specs() + [row,
                                       pl.BlockSpec((1, d), lambda i: (0, 0)),
                                       pl.BlockSpec((d, n), lambda i: (0, 0)),
                                       pl.BlockSpec((1, n), lambda i: (0, 0))],
        out_specs=[row, row, pl.BlockSpec((ROW_TILE, n), lambda i: (i, 0))],
        out_shape=[jax.ShapeDtypeStruct((m, d), F32), jax.ShapeDtypeStruct((m, d), BF16),
                   jax.ShapeDtypeStruct((m, n), F32)],
        compiler_params=_cparams(("parallel",)),
        name="norm_router",
    )(x_p, x_s, mix, norm_w.reshape(1, d), w_router, b_router)


def _moe_up_body(be_ref, nused_ref, win_ref, x_ref, wg_ref, wu_ref, o_ref):
    blk = pl.program_id(1)

    @pl.when(blk < nused_ref[0])
    def _():
        x = x_ref[...]
        g = jnp.dot(x, wg_ref[0].astype(BF16), preferred_element_type=F32)
        u = jnp.dot(x, wu_ref[0].astype(BF16), preferred_element_type=F32)
        o_ref[...] = (g * _sigmoid(g) * u).astype(o_ref.dtype)

    @pl.when(blk >= nused_ref[0])
    def _():
        o_ref[...] = jnp.zeros(o_ref.shape, o_ref.dtype)


def _moe_up(be, nused, win, xs, w_gate, w_up):
    d = xs.shape[1]
    nblk = be.shape[0]
    grid_spec = pltpu.PrefetchScalarGridSpec(
        num_scalar_prefetch=3,
        grid=(D_EXPERT // MOE_TF, nblk),
        in_specs=[pl.BlockSpec((pl.Element(MOE_TM), pl.Element(d)),
                               lambda f, i, be, nu, win: (win[i] * MOE_GATHER_ALIGN, 0)),
                  pl.BlockSpec((1, d, MOE_TF), lambda f, i, be, nu, win: (be[i], 0, f)),
                  pl.BlockSpec((1, d, MOE_TF), lambda f, i, be, nu, win: (be[i], 0, f))],
        out_specs=pl.BlockSpec((MOE_TM, MOE_TF), lambda f, i, be, nu, win: (i, f)),
    )
    return pl.pallas_call(
        _moe_up_body,
        grid_spec=grid_spec,
        out_shape=jax.ShapeDtypeStruct((nblk * MOE_TM, D_EXPERT), BF16),
        compiler_params=_cparams(("arbitrary", "arbitrary")),
        name="moe_up",
    )(be, nused, win, xs, w_gate, w_up)


def _moe_down_body(be_ref, nused_ref, h_ref, wd_ref, rw_ref, o_ref):
    blk = pl.program_id(1)

    @pl.when(blk < nused_ref[0])
    def _():
        y = jnp.dot(h_ref[...], wd_ref[0].astype(BF16), preferred_element_type=F32)
        o_ref[...] = (y * rw_ref[...]).astype(o_ref.dtype)

    @pl.when(blk >= nused_ref[0])
    def _():
        o_ref[...] = jnp.zeros(o_ref.shape, o_ref.dtype)


def _moe_down(be, nused, hact, w_down, row_w):
    p, f = hact.shape
    d = w_down.shape[2]
    nblk = p // MOE_TM
    grid_spec = pltpu.PrefetchScalarGridSpec(
        num_scalar_prefetch=2,
        grid=(d // MOE_TN, nblk),
        in_specs=[pl.BlockSpec((MOE_TM, f), lambda n, i, be, nu: (i, 0)),
                  pl.BlockSpec((1, f, MOE_TN), lambda n, i, be, nu: (be[i], 0, n)),
                  pl.BlockSpec((MOE_TM, 1), lambda n, i, be, nu: (i, 0))],
        out_specs=pl.BlockSpec((MOE_TM, MOE_TN), lambda n, i, be, nu: (i, n)),
    )
    return pl.pallas_call(
        _moe_down_body,
        grid_spec=grid_spec,
        out_shape=jax.ShapeDtypeStruct((p, d), BF16),
        compiler_params=_cparams(("arbitrary", "arbitrary")),
        name="moe_down",
    )(be, nused, hact, w_down, row_w)


def _route(logits):
    t = logits.shape[0]
    g_logits = logits[:, :N_GROUPS]
    g_sel = jnp.argmax(g_logits, axis=-1)
    g_w = jnp.max(jax.nn.softmax(g_logits, axis=-1), axis=-1)
    e_logits = logits[:, N_GROUPS:N_GROUPS + N_EXPERTS].reshape(t, N_GROUPS, EXPERTS_PER_GROUP)
    e_in = jnp.take_along_axis(e_logits, g_sel[:, None, None], axis=1)[:, 0]
    e_val, e_idx = lax.top_k(e_in, TOP_K)
    e_w = jax.nn.softmax(e_val, axis=-1) * g_w[:, None]
    expert_id = (g_sel[:, None] * EXPERTS_PER_GROUP + e_idx).reshape(-1).astype(jnp.int32)
    a = t * TOP_K
    e_sorted, order = lax.sort((expert_id, jnp.arange(a, dtype=jnp.int32)), num_keys=1, is_stable=True)
    experts = jnp.arange(N_EXPERTS, dtype=jnp.int32)
    start = jnp.searchsorted(e_sorted, experts, side='left').astype(jnp.int32)
    counts = jnp.searchsorted(e_sorted, experts, side='right').astype(jnp.int32) - start
    padded = (counts + MOE_TM - 1) // MOE_TM * MOE_TM
    pad_end = jnp.cumsum(padded)
    pad_start = pad_end - padded
    shift = pad_start - start
    jumps = shift - jnp.concatenate([jnp.zeros((1,), jnp.int32), shift[:-1]])
    marks = jnp.zeros((a + 1,), jnp.int32).at[start].add(jumps)
    dest = jnp.arange(a, dtype=jnp.int32) + jnp.cumsum(marks)[:a]
    pos = lax.sort((order, dest), num_keys=1)[1].reshape(t, TOP_K)
    nblk = a // MOE_TM + N_EXPERTS
    nused = (pad_end[-1] // MOE_TM).astype(jnp.int32)
    blk = jnp.arange(nblk, dtype=jnp.int32)
    owner = lambda ends, offs: jnp.minimum(jnp.sum((ends[None, :] <= offs[:, None]).astype(jnp.int32), axis=1),
                                           N_EXPERTS - 1)
    be = owner(pad_end, blk * MOE_TM)
    be = jnp.where(blk < nused, be, be[jnp.maximum(nused - 1, 0)])
    per_row = lambda v: jnp.broadcast_to(v[be][:, None], (nblk, MOE_TM)).reshape(-1)
    rank = jnp.arange(nblk * MOE_TM, dtype=jnp.int32) - per_row(pad_start)
    valid = rank < per_row(counts)
    src = jnp.clip(per_row(start) + rank, 0, a - 1)
    row_w = jnp.where(valid, e_w.reshape(-1)[order[src]], 0.0)
    g = MOE_GATHER_ALIGN
    tight = (counts + g - 1) // g * g
    tight_end = jnp.cumsum(tight)
    tight_start = tight_end - tight
    n_mini = MOE_GATHER_ROWS // g
    mini_e = owner(tight_end, jnp.arange(n_mini, dtype=jnp.int32) * g)
    per_mini = lambda v: jnp.broadcast_to(v[mini_e][:, None], (n_mini, g)).reshape(-1)
    grank = jnp.arange(MOE_GATHER_ROWS, dtype=jnp.int32) - per_mini(tight_start)
    gvalid = (grank >= 0) & (grank < per_mini(counts))
    gather_token = jnp.where(gvalid, order[jnp.clip(per_mini(start) + grank, 0, a - 1)] // TOP_K, 0)
    win = (tight_start[be] + blk * MOE_TM - pad_start[be]) // g
    win = jnp.where(blk < nused, win, 0).astype(jnp.int32)
    return gather_token, win, row_w, pos, be, nused.reshape(1)


def _rope_tables():
    half = QK_ROPE // 2
    inv_freq = ROPE_THETA ** (-jnp.arange(half, dtype=F32) / half)
    pos = jnp.concatenate([jnp.tile(jnp.arange(SEQ), BATCH),
                           jnp.tile(PAST_LEN + jnp.arange(DEC_SEQ), DEC_BATCH)]).astype(F32)
    ang = pos[:, None] * inv_freq[None, :]
    cos, sin = jnp.cos(ang), jnp.sin(ang)
    cos2 = jnp.concatenate([cos, cos], axis=1)
    sin2 = jnp.concatenate([-sin, sin], axis=1)
    return cos2, sin2


def kernel(x_prompt, x_sample, cache_ckv, cache_krope, state_conv, state_ssm, norm1_w, w_in, q_norm_w, kv_norm_w, w_uq, w_ukv, conv_w, conv_b, dt_bias, a_log, d_skip, ssd_norm_w, w_mla_o, w_ssd_o, w_out, norm2_w, w_group, b_group, w_erouter, b_erouter, w_gate, w_up, w_down, final_norm_w):
    swap = np.concatenate([np.arange(QK_ROPE // 2, QK_ROPE), np.arange(QK_ROPE // 2)])
    x_p2 = x_prompt.reshape(T_PROMPT, D_MODEL)
    x_s2 = x_sample.reshape(T_SAMPLE, D_MODEL)
    cos2, sin2 = _rope_tables()
    cos128, sin128 = jnp.tile(cos2, (1, 2)), jnp.tile(sin2, (1, 2))

    wi = w_in[0]
    w_kr = wi[:, OFF_KR:OFF_KR + QK_ROPE]
    w_small = jnp.concatenate([wi[:, OFF_CQ:OFF_KR], wi[:, OFF_DT:OFF_DT + SSD_HEADS], w_kr, w_kr[:, swap]],
                              axis=1).astype(BF16)
    w_big = wi[:, OFF_Z:OFF_DT].astype(BF16)
    w_gates = wi[:, OFF_GMLA:].astype(BF16)
    wq = w_uq[0].reshape(Q_LORA, MLA_HEADS, QK_NOPE + QK_ROPE)
    wq_nope = wq[:, :, :QK_NOPE].reshape(Q_LORA, MLA_HEADS * QK_NOPE).astype(BF16)
    wq_rope = wq[:, :, QK_NOPE:]
    wq_r = wq_rope.reshape(Q_LORA, MLA_HEADS * QK_ROPE).astype(BF16)
    wq_rs = wq_rope[:, :, swap].reshape(Q_LORA, MLA_HEADS * QK_ROPE).astype(BF16)
    wkv3 = w_ukv[0].reshape(KV_LORA, MLA_HEADS, QK_NOPE + V_HEAD)
    w_uk_all = wkv3[:, :, :QK_NOPE].reshape(KV_LORA, MLA_HEADS * QK_NOPE).astype(BF16)
    w_uv_all_t = jnp.transpose(wkv3[:, :, QK_NOPE:], (1, 2, 0)).reshape(MLA_HEADS * V_HEAD, KV_LORA).astype(BF16)
    w_uk_t = jnp.transpose(wkv3[:, :, :QK_NOPE], (1, 2, 0)).astype(BF16)
    w_uv_h = jnp.transpose(wkv3[:, :, QK_NOPE:], (1, 0, 2)).astype(BF16)

    u = _rms_rows(x_p2, x_s2, norm1_w[0], BF16)
    small = _mm(u, w_small, tm=MM_TM, tn=SM_N // 2, out_dtype=F32, name="proj_small")
    big = _mm(u, w_big, tm=MM_TM, tn=1024, out_dtype=F32, name="proj_big")
    gates = _mm(u, w_gates, tm=MM_TM, tn=1024, out_dtype=BF16, name="proj_gates")

    cqn, ckv, ckv_b, krope, krope_b = _mla_prep(small, q_norm_w[0], kv_norm_w[0], cos2, sin2)
    qn = _mm(cqn, wq_nope, tm=MM_TM, tn=1024, out_dtype=BF16, epilogue=lambda acc: acc * QSCALE, name="q_nope")
    qr = _q_rope(cqn, wq_r, wq_rs, cos128, sin128, tm=MM_TM, tn=1024)
    kn = _mm(ckv_b, w_uk_all, tm=MM_TM, tn=1024, out_dtype=BF16, name="k_up")
    vt = _mm_t(w_uv_all_t, ckv_b, rows=T_PROMPT, tm=ATTN_TK, tn=1024, out_dtype=BF16, name="v_up_t")

    qn_s = qn[T_PROMPT:].reshape(T_SAMPLE, MLA_HEADS, QK_NOPE).transpose(1, 0, 2)
    q_lat = _bmm(qn_s, w_uk_t, BF16, "q_absorb")
    q_lat = q_lat.reshape(MLA_HEADS, DEC_BATCH, DEC_SEQ, KV_LORA).transpose(1, 0, 2, 3)
    q_lat = q_lat.reshape(DEC_BATCH, MLA_HEADS * DEC_SEQ, KV_LORA)
    qr_s = qr[T_PROMPT:].reshape(DEC_BATCH, DEC_SEQ, MLA_HEADS, QK_ROPE).transpose(0, 2, 1, 3)
    qr_s = qr_s.reshape(DEC_BATCH, MLA_HEADS * DEC_SEQ, QK_ROPE)
    o_lat = _attn_sample(q_lat, qr_s, cache_ckv[0], cache_krope[0], ckv, krope)
    o_lat = o_lat.reshape(DEC_BATCH, MLA_HEADS, DEC_SEQ, KV_LORA).transpose(1, 0, 2, 3)
    o_lat = o_lat.reshape(MLA_HEADS, T_SAMPLE, KV_LORA)
    o_mla_s = _bmm(o_lat, w_uv_h, BF16, "v_absorb")
    o_mla_s = o_mla_s.transpose(1, 0, 2).reshape(T_SAMPLE, MLA_HEADS * V_HEAD)
    o_mla = _attn_prompt(qn, qr, kn, krope_b, vt, jnp.pad(o_mla_s, ((T_PROMPT, 0), (0, 0))),
                         nb=BATCH, seq=SEQ, heads=MLA_HEADS, tq=ATTN_TQ)

    prev_p = jnp.zeros((BATCH, 8, SSD_CONV_DIM), F32)
    prev_s = jnp.concatenate([jnp.zeros((DEC_BATCH, 8 - (SSD_CONV - 1), SSD_CONV_DIM), F32), state_conv[0]], axis=1)
    xbc_p = _conv_silu(big, prev_p, conv_w[0], conv_b[0], nb=BATCH, seq=SEQ, tl=512, row_off=0)
    xbc_s = _conv_silu(big, prev_s, conv_w[0], conv_b[0], nb=DEC_BATCH, seq=DEC_SEQ, tl=DEC_SEQ, row_off=T_PROMPT)
    dt_raw = small[:, SM_DT:SM_DT + SSD_HEADS]
    by_group = lambda d: d.reshape(d.shape[0], SSD_GROUPS, SSD_RANK).transpose(1, 0, 2)

    def pad_seq(a, fill):
        a = a.reshape(DEC_BATCH, DEC_SEQ, a.shape[-1])
        a = jnp.pad(a, ((0, 0), (0, CHUNK - DEC_SEQ), (0, 0)), constant_values=fill)
        return a.reshape(DEC_BATCH * CHUNK, a.shape[-1])

    h0_s = state_ssm[0].reshape(DEC_BATCH, SSD_HEADS * SSD_HEADDIM, SSD_STATE)
    y_s, ssm_s = _ssd(pad_seq(xbc_s, 0.0), pad_seq(big[T_PROMPT:, BIG_Z:BIG_Z + SSD_INNER], 0.0),
                      by_group(pad_seq(dt_raw[T_PROMPT:], -jnp.inf)), h0_s, dt_bias[0], a_log[0], d_skip[0],
                      ssd_norm_w[0], nb=DEC_BATCH, seq=CHUNK, sub=1)
    y_s = y_s.reshape(DEC_BATCH, CHUNK, SSD_INNER)[:, :DEC_SEQ].reshape(T_SAMPLE, SSD_INNER)
    o_ssd, ssm_p = _ssd(xbc_p, big, by_group(dt_raw[:T_PROMPT]), None, dt_bias[0], a_log[0], d_skip[0],
                        ssd_norm_w[0], nb=BATCH, seq=SEQ, sub=SSD_SUB, base=jnp.pad(y_s, ((T_PROMPT, 0), (0, 0))))

    gate = lambda acc, g: _sigmoid(g.astype(F32)) * acc
    m1 = _mm(o_mla, w_mla_o[0].astype(BF16), tm=640, tn=1024, out_dtype=F32,
             extras=[(gates, GATE_MLA // 1024)], epilogue=gate, name="mla_out")
    merged = _mm(o_ssd, w_ssd_o[0].astype(BF16), tm=640, tn=512, out_dtype=BF16,
                 extras=[(gates, GATE_SSD // 512), (m1, 0)],
                 epilogue=lambda acc, g, m: m + _sigmoid(g.astype(F32)) * acc, name="ssd_out")
    mix = _mm(merged, w_out[0].astype(BF16), tm=640, tn=1024, out_dtype=F32, name="out_proj")

    n_r = 128
    w_router = jnp.concatenate([w_group[0], w_erouter[0],
                                jnp.zeros((D_MODEL, n_r - N_GROUPS - N_EXPERTS), F32)], axis=1)
    b_router = jnp.concatenate([b_group[0], b_erouter[0], jnp.zeros((n_r - N_GROUPS - N_EXPERTS,), F32)])
    h, xn, logits = _norm_router(x_p2, x_s2, mix, norm2_w[0], w_router, b_router.reshape(1, n_r))
    gather_token, win, row_w, pos, be, nused = _route(logits)
    xs = jnp.take(xn, gather_token, axis=0, mode="clip")
    hact = _moe_up(be, nused, win, xs, w_gate[0], w_up[0])
    yb = _moe_down(be, nused, hact, w_down[0], row_w.reshape(-1, 1))
    y_p2, y_s2 = _final_norm(h, jnp.take(yb, pos[:, 0], axis=0, mode="clip"),
                             jnp.take(yb, pos[:, 1], axis=0, mode="clip"), final_norm_w)

    y_prompt = y_p2.reshape(BATCH, SEQ, D_MODEL)
    y_sample = y_s2.reshape(DEC_BATCH, DEC_SEQ, D_MODEL)
    ckv_p = ckv[:T_PROMPT].reshape(1, BATCH, SEQ, KV_LORA)
    ckv_s = ckv[T_PROMPT:].reshape(1, DEC_BATCH, DEC_SEQ, KV_LORA)
    kr_p = krope[:T_PROMPT].reshape(1, BATCH, SEQ, QK_ROPE)
    kr_s = krope[T_PROMPT:].reshape(1, DEC_BATCH, DEC_SEQ, QK_ROPE)
    tail = SSD_CONV - 1
    conv_p = jnp.stack([lax.slice(big, ((b + 1) * SEQ - tail, BIG_XBC), ((b + 1) * SEQ, BIG_XBC + SSD_CONV_DIM))
                        for b in range(BATCH)])[None]
    conv_s = jnp.stack([lax.slice(big, (T_PROMPT + (b + 1) * DEC_SEQ - tail, BIG_XBC),
                                  (T_PROMPT + (b + 1) * DEC_SEQ, BIG_XBC + SSD_CONV_DIM))
                        for b in range(DEC_BATCH)])[None]
    ssm_p = ssm_p.reshape(1, BATCH, SSD_HEADS, SSD_HEADDIM, SSD_STATE)
    ssm_s = ssm_s.reshape(1, DEC_BATCH, SSD_HEADS, SSD_HEADDIM, SSD_STATE)
    return (y_prompt, y_sample, ckv_p, kr_p, conv_p, ssm_p, ckv_s, kr_s, conv_s, ssm_s)
```
